```python
import jax, jax.numpy as jnp
from jax import lax
import numpy as np

D_MODEL = 1024
BATCH = 8
SEQ = 4096
DEPTH = 2
DEC_BATCH = 32
DEC_SEQ = 8
PAST_LEN = 16384
PAGE_SIZE = 128

N_HEADS = 8
N_KV_HEADS = 4
HEAD_DIM = 64
N_IDX_HEADS = 8
IDX_DIM = 64
TOPK_MAX = 256
Q_BLOCK = 128
ROPE_THETA = 10000.0
POOL_WINDOWS = (2, 4, 8, 16)
POOL_GROUPS = 4
POOL_WIDTH = 256
POOL_GC = POOL_WIDTH // POOL_GROUPS
POOL_STATE = 15
GMLP_WIDTH = 256
GMLP_GROUPS = 4
GMLP_GC = GMLP_WIDTH // GMLP_GROUPS
CHUNK = 128
N_BRANCH = 3
ATT_Q = N_HEADS * HEAD_DIM
ATT_KV = N_KV_HEADS * HEAD_DIM
IN_SIZES = (ATT_Q, ATT_KV, ATT_KV, N_IDX_HEADS * IDX_DIM, IDX_DIM, N_IDX_HEADS,
            POOL_WIDTH, GMLP_WIDTH, GMLP_WIDTH, N_BRANCH * D_MODEL)
N_IN = 5448
N_EXPERTS = 64
TOPK_EXPERTS = 8
N_EXPERT_GROUPS = 8
TOPK_GROUPS = 4
EXPERT_FF = 256
SHARED_FF = 256
ROUTED_SCALE = 2.5
LN_EPS = 1e-5
DN_ALPHA = (2 * DEPTH) ** 0.25
DN_BETA = (8 * DEPTH) ** -0.25

kernel_name = 'hybrid_dsa_pool_gmlp_moe_step'


def layer_norm(x, g, b):
    xf = x.astype(jnp.float32)
    mu = xf.mean(-1, keepdims=True)
    var = jnp.square(xf - mu).mean(-1, keepdims=True)
    return ((xf - mu) * lax.rsqrt(var + LN_EPS) * g.astype(jnp.float32) + b.astype(jnp.float32)).astype(x.dtype)


def rope(x, pos):
    half = x.shape[-1] // 2
    inv = ROPE_THETA ** (-jnp.arange(half, dtype=jnp.float32) / half)
    ang = pos.astype(jnp.float32)[:, None] * inv[None, :]
    cos = jnp.cos(ang)[:, None, :]
    sin = jnp.sin(ang)[:, None, :]
    xf = x.astype(jnp.float32)
    x1, x2 = xf[..., :half], xf[..., half:]
    return jnp.concatenate([x1 * cos - x2 * sin, x2 * cos + x1 * sin], axis=-1).astype(x.dtype)


def indexer_select(iq, iw, ik, q_pos, k_pos, topk):
    s = jnp.einsum('nqhd,nld->nqhl', iq, ik).astype(jnp.float32)
    score = jnp.einsum('nqhl,nqh->nql', jax.nn.relu(s), iw.astype(jnp.float32))
    allowed = k_pos[None, None, :] <= q_pos[None, :, None]
    score = jnp.where(allowed, score, -jnp.inf)
    top_val, top_idx = lax.top_k(score, topk)
    return top_idx, jnp.isfinite(top_val)


def sparse_attend(q, k_sel, v_sel, valid):
    n, qn = q.shape[:2]
    qg = q.reshape(n, qn, N_KV_HEADS, N_HEADS // N_KV_HEADS, HEAD_DIM)
    logits = jnp.einsum('nqgrd,nqkgd->nqgrk', qg, k_sel).astype(jnp.float32) * HEAD_DIM ** -0.5
    logits = jnp.where(valid[:, :, None, None, :], logits, -jnp.inf)
    p = jax.nn.softmax(logits, axis=-1)
    o = jnp.einsum('nqgrk,nqkgd->nqgrd', p.astype(v_sel.dtype), v_sel)
    return o.reshape(n, qn, ATT_Q)


def dsa_prompt(q, k, v, iq, ik, iw):
    n, t = q.shape[:2]
    topk = min(TOPK_MAX, t // 4)
    k_pos = jnp.arange(t, dtype=jnp.int32)

    def block(i):
        s0 = i * Q_BLOCK
        qb = lax.dynamic_slice_in_dim(q, s0, Q_BLOCK, axis=1)
        iqb = lax.dynamic_slice_in_dim(iq, s0, Q_BLOCK, axis=1)
        iwb = lax.dynamic_slice_in_dim(iw, s0, Q_BLOCK, axis=1)
        q_pos = s0 + jnp.arange(Q_BLOCK, dtype=jnp.int32)
        idx, valid = indexer_select(iqb, iwb, ik, q_pos, k_pos, topk)
        k_sel = jax.vmap(lambda rows, ii: rows[ii])(k, idx)
        v_sel = jax.vmap(lambda rows, ii: rows[ii])(v, idx)
        return sparse_attend(qb, k_sel, v_sel, valid)

    out = lax.map(block, jnp.arange(t // Q_BLOCK))
    return out.transpose(1, 0, 2, 3).reshape(n, t, ATT_Q)


def dsa_sample(q, k_new, v_new, iq, ik_new, iw, cache_k, cache_v, cache_ik, page_table):
    n, s = q.shape[:2]
    past = page_table.shape[1] * PAGE_SIZE
    topk = min(TOPK_MAX, (past + s) // 4)
    ik_past = cache_ik[page_table].reshape(n, past, IDX_DIM)
    ik_all = jnp.concatenate([ik_past, ik_new], axis=1)
    q_pos = past + jnp.arange(s, dtype=jnp.int32)
    k_pos = jnp.arange(past + s, dtype=jnp.int32)
    idx, valid = indexer_select(iq, iw, ik_all, q_pos, k_pos, topk)
    in_past = idx < past
    ip = jnp.minimum(idx, past - 1)
    phys = jnp.take_along_axis(page_table, (ip // PAGE_SIZE).reshape(n, -1), axis=1).reshape(idx.shape)
    off = ip % PAGE_SIZE
    inew = jnp.clip(idx - past, 0, s - 1)

    def pick(cache, new):
        from_past = cache[phys, off]
        from_new = jax.vmap(lambda rows, ii: rows[ii])(new, inew)
        return jnp.where(in_past[..., None, None], from_past, from_new)

    return sparse_attend(q, pick(cache_k, k_new), pick(cache_v, v_new), valid)


def pool_mix(p, prefix, pos, pool_w, pool_scale):
    n, t, _ = p.shape
    ext = jnp.concatenate([prefix, p], axis=1).astype(jnp.float32)
    c = jnp.cumsum(ext, axis=1)
    c = jnp.concatenate([jnp.zeros_like(c[:, :1]), c], axis=1)
    means = []
    for g, w in enumerate(POOL_WINDOWS):
        sl = slice(g * POOL_GC, (g + 1) * POOL_GC)
        hi = c[:, POOL_STATE + 1:POOL_STATE + 1 + t, sl]
        lo = c[:, POOL_STATE + 1 - w:POOL_STATE + 1 - w + t, sl]
        cnt = jnp.minimum(w, pos + 1).astype(jnp.float32)[None, :, None]
        means.append((hi - lo) / cnt)
    mean = jnp.stack(means, axis=2)
    d = (mean - p.astype(jnp.float32).reshape(n, t, POOL_GROUPS, POOL_GC)).astype(p.dtype)
    y = jnp.einsum('ntgc,gcd->ntgd', d, pool_w).reshape(n, t, POOL_WIDTH) * pool_scale
    return y, ext[:, -POOL_STATE:].astype(p.dtype)


def gmlp_chunk(u, vn, ws, bias, cl):
    n, t, _ = vn.shape
    mask = jnp.tril(jnp.ones((cl, cl), dtype=bool))
    wm = jnp.where(mask, ws[:, :cl, :cl], 0)
    vc = vn.reshape(n, t // cl, cl, GMLP_GROUPS, GMLP_GC)
    mix = jnp.einsum('gts,ncsgd->nctgd', wm, vc) + bias[:, :cl].T[None, None, :, :, None]
    return u * mix.reshape(n, t, GMLP_WIDTH)


def token_mixers(x, pos, attend, pool_prefix, chunk_len, w_in, w_ba, w_bp, w_bg, w_o,
                 pool_w, pool_scale, g_ln_g, g_ln_b, g_ws, g_b):
    n, t, _ = x.shape
    splits = np.cumsum(IN_SIZES)[:-1].tolist()
    q, k, v, iq, ik, iw, p, u, gv, gates = jnp.split(x @ w_in, splits, axis=-1)
    q = rope(q.reshape(n, t, N_HEADS, HEAD_DIM), pos)
    k = rope(k.reshape(n, t, N_KV_HEADS, HEAD_DIM), pos)
    v = v.reshape(n, t, N_KV_HEADS, HEAD_DIM)
    iq = rope(iq.reshape(n, t, N_IDX_HEADS, IDX_DIM), pos)
    ik = rope(ik[:, :, None, :], pos)[:, :, 0, :]
    a = attend(q, k, v, iq, ik, iw)
    b_out, pool_state = pool_mix(p, pool_prefix, pos, pool_w, pool_scale)
    gvn = layer_norm(gv, g_ln_g, g_ln_b)
    c_out = gmlp_chunk(u, gvn, g_ws, g_b, chunk_len)
    gs = jax.nn.sigmoid(gates.astype(jnp.float32)).astype(x.dtype).reshape(n, t, N_BRANCH, D_MODEL)
    m = gs[:, :, 0] * (a @ w_ba) + gs[:, :, 1] * (b_out @ w_bp) + gs[:, :, 2] * (c_out @ w_bg)
    return m @ w_o, k, v, ik, pool_state, gvn


def swiglu(x, w_gu, w_down):
    g, u = jnp.split(x @ w_gu, 2, axis=-1)
    return (jax.nn.silu(g) * u) @ w_down


def moe_ffn(x, router_w, router_bias, w_gu, w_down, sh_gu, sh_down):
    n, t, d = x.shape
    xf = x.reshape(n * t, d)
    scores = jax.nn.sigmoid((xf @ router_w).astype(jnp.float32))
    sel = scores + router_bias.astype(jnp.float32)
    per_group = N_EXPERTS // N_EXPERT_GROUPS
    grp_score = lax.top_k(sel.reshape(-1, N_EXPERT_GROUPS, per_group), 2)[0].sum(-1)
    _, gidx = lax.top_k(grp_score, TOPK_GROUPS)
    gmask = jax.nn.one_hot(gidx, N_EXPERT_GROUPS, dtype=jnp.float32).sum(-2) > 0
    sel = jnp.where(jnp.repeat(gmask, per_group, axis=-1), sel, -jnp.inf)
    _, eidx = lax.top_k(sel, TOPK_EXPERTS)
    wsel = jnp.take_along_axis(scores, eidx, axis=-1)
    wsel = wsel / wsel.sum(-1, keepdims=True) * ROUTED_SCALE
    gate = (jax.nn.one_hot(eidx, N_EXPERTS, dtype=jnp.float32) * wsel[..., None]).sum(-2).astype(x.dtype)
    y = swiglu(xf, sh_gu, sh_down)
    for e in range(N_EXPERTS):
        y = y + gate[:, e:e + 1] * swiglu(xf, w_gu[e], w_down[e])
    return y.reshape(n, t, d)


def setup_inputs(seed: int = 0) -> dict:
    key = jax.random.key(seed)
    ks = jax.random.split(key, 32)
    f32 = jnp.float32

    def nrm(k, shape, scale):
        return jax.random.normal(k, shape, f32) * scale

    n_pages = PAST_LEN // PAGE_SIZE
    n_used = DEC_BATCH * n_pages
    n_pool = n_used + n_used // 4
    page_table = jax.random.permutation(ks[0], n_pool)[:n_used].reshape(DEC_BATCH, n_pages).astype(jnp.int32)
    v_lo = ATT_Q + ATT_KV
    w_in = nrm(ks[1], (DEPTH, D_MODEL, N_IN), D_MODEL ** -0.5)
    w_in = w_in.at[:, :, v_lo:v_lo + ATT_KV].multiply(DN_BETA)
    return {
        'x_prompt': nrm(ks[2], (BATCH, SEQ, D_MODEL), 1.0),
        'x_sample': nrm(ks[3], (DEC_BATCH, DEC_SEQ, D_MODEL), 1.0),
        'cache_k': nrm(ks[4], (DEPTH, n_pool, PAGE_SIZE, N_KV_HEADS, HEAD_DIM), 1.0),
        'cache_v': nrm(ks[5], (DEPTH, n_pool, PAGE_SIZE, N_KV_HEADS, HEAD_DIM), 1.0),
        'cache_idx_k': nrm(ks[6], (DEPTH, n_pool, PAGE_SIZE, IDX_DIM), 1.0),
        'state_pool': nrm(ks[7], (DEPTH, DEC_BATCH, POOL_STATE, POOL_WIDTH), 1.0),
        'page_table': page_table,
        'w_in': w_in,
        'w_branch_attn': nrm(ks[8], (DEPTH, ATT_Q, D_MODEL), ATT_Q ** -0.5),
        'w_branch_pool': nrm(ks[9], (DEPTH, POOL_WIDTH, D_MODEL), POOL_WIDTH ** -0.5),
        'w_branch_gmlp': nrm(ks[10], (DEPTH, GMLP_WIDTH, D_MODEL), GMLP_WIDTH ** -0.5),
        'w_out': nrm(ks[11], (DEPTH, D_MODEL, D_MODEL), D_MODEL ** -0.5 * DN_BETA),
        'pool_w': nrm(ks[12], (DEPTH, POOL_GROUPS, POOL_GC, POOL_GC), POOL_GC ** -0.5),
        'pool_scale': 1.0 + nrm(ks[13], (DEPTH, POOL_WIDTH), 0.02),
        'gmlp_ln_g': 1.0 + nrm(ks[14], (DEPTH, GMLP_WIDTH), 0.02),
        'gmlp_ln_b': nrm(ks[15], (DEPTH, GMLP_WIDTH), 0.02),
        'gmlp_ws': nrm(ks[16], (DEPTH, GMLP_GROUPS, CHUNK, CHUNK), CHUNK ** -0.5),
        'gmlp_b': 1.0 + nrm(ks[17], (DEPTH, GMLP_GROUPS, CHUNK), 0.02),
        'ln1_g': 1.0 + nrm(ks[18], (DEPTH, D_MODEL), 0.02),
        'ln1_b': nrm(ks[19], (DEPTH, D_MODEL), 0.02),
        'router_w': nrm(ks[20], (DEPTH, D_MODEL, N_EXPERTS), D_MODEL ** -0.5),
        'router_bias': nrm(ks[21], (DEPTH, N_EXPERTS), 0.01),
        'expert_w_gu': nrm(ks[22], (DEPTH, N_EXPERTS, D_MODEL, 2 * EXPERT_FF), D_MODEL ** -0.5),
        'expert_w_down': nrm(ks[23], (DEPTH, N_EXPERTS, EXPERT_FF, D_MODEL), EXPERT_FF ** -0.5 * DN_BETA),
        'shared_w_gu': nrm(ks[24], (DEPTH, D_MODEL, 2 * SHARED_FF), D_MODEL ** -0.5),
        'shared_w_down': nrm(ks[25], (DEPTH, SHARED_FF, D_MODEL), SHARED_FF ** -0.5 * DN_BETA),
        'ln2_g': 1.0 + nrm(ks[26], (DEPTH, D_MODEL), 0.02),
        'ln2_b': nrm(ks[27], (DEPTH, D_MODEL), 0.02),
    }


def reference(x_prompt, x_sample, cache_k, cache_v, cache_idx_k, state_pool, page_table,
              w_in, w_branch_attn, w_branch_pool, w_branch_gmlp, w_out, pool_w, pool_scale,
              gmlp_ln_g, gmlp_ln_b, gmlp_ws, gmlp_b, ln1_g, ln1_b, router_w, router_bias,
              expert_w_gu, expert_w_down, shared_w_gu, shared_w_down, ln2_g, ln2_b):
    n_p, t_p, _ = x_prompt.shape
    n_s, t_s, _ = x_sample.shape
    past = page_table.shape[1] * PAGE_SIZE
    pos_p = jnp.arange(t_p, dtype=jnp.int32)
    pos_s = past + jnp.arange(t_s, dtype=jnp.int32)

    def run(x, pos, attend, prefix, cl, l):
        y, k, v, ik, pst, gvn = token_mixers(
            x, pos, attend, prefix, cl, w_in[l], w_branch_attn[l], w_branch_pool[l], w_branch_gmlp[l],
            w_out[l], pool_w[l], pool_scale[l], gmlp_ln_g[l], gmlp_ln_b[l], gmlp_ws[l], gmlp_b[l])
        x = layer_norm(DN_ALPHA * x + y, ln1_g[l], ln1_b[l])
        f = moe_ffn(x, router_w[l], router_bias[l], expert_w_gu[l], expert_w_down[l],
                    shared_w_gu[l], shared_w_down[l])
        x = layer_norm(DN_ALPHA * x + f, ln2_g[l], ln2_b[l])
        return x, k, v, ik, pst, gvn

    hp, hs = x_prompt, x_sample
    kp, vp, ikp, pp = [], [], [], []
    ks_, vs_, iks, ps, gs = [], [], [], [], []
    for l in range(DEPTH):
        attend_s = (lambda q, k, v, iq, ik, iw, l=l:
                    dsa_sample(q, k, v, iq, ik, iw, cache_k[l], cache_v[l], cache_idx_k[l], page_table))
        zero_prefix = jnp.zeros((n_p, POOL_STATE, POOL_WIDTH), hp.dtype)
        hp, k1, v1, ik1, p1, _ = run(hp, pos_p, dsa_prompt, zero_prefix, CHUNK, l)
        hs, k2, v2, ik2, p2, g2 = run(hs, pos_s, attend_s, state_pool[l], t_s, l)
        kp.append(k1); vp.append(v1); ikp.append(ik1); pp.append(p1)
        ks_.append(k2); vs_.append(v2); iks.append(ik2); ps.append(p2); gs.append(g2)
    return (hp, hs, jnp.stack(kp), jnp.stack(vp), jnp.stack(ikp), jnp.stack(ks_), jnp.stack(vs_),
            jnp.stack(iks), jnp.stack(pp), jnp.stack(ps), jnp.stack(gs))
```

```python
import functools

import jax
import jax.numpy as jnp
import numpy as np
from jax import lax
from jax.experimental import pallas as pl
from jax.experimental.pallas import tpu as pltpu

F32 = jnp.float32
BF16 = jnp.bfloat16
I32 = jnp.int32

D_MODEL = 1024
N_HEADS = 8
N_KV_HEADS = 4
HEAD_DIM = 64
N_IDX_HEADS = 8
IDX_DIM = 64
TOPK_MAX = 256
PAGE_SIZE = 128
ROPE_THETA = 10000.0
POOL_WINDOWS = (2, 4, 8, 16)
POOL_WIDTH = 256
POOL_GC = 64
POOL_STATE = 15
GMLP_WIDTH = 256
GMLP_GROUPS = 4
GMLP_GC = 64
CHUNK = 128
N_BRANCH = 3
ATT_Q = N_HEADS * HEAD_DIM
ATT_KV = N_KV_HEADS * HEAD_DIM
N_EXPERTS = 64
TOPK_EXPERTS = 8
N_EXPERT_GROUPS = 8
TOPK_GROUPS = 4
EXPERT_FF = 256
ROUTED_SCALE = 2.5
LN_EPS = 1e-5

LANES = 128
BLK = 128
INT_MIN = -2 ** 31
MASKED = -1e30
VMEM_LIMIT = 56 * 1024 * 1024

QBD_W = N_HEADS * ATT_KV
IQ_W = N_IDX_HEADS * LANES
C_Q = 0
C_K = C_Q + QBD_W
C_V = C_K + ATT_KV
C_IQ = C_V + ATT_KV
C_IKW = C_IQ + IQ_W
C_P = C_IKW + LANES
C_U = C_P + POOL_WIDTH
C_GV = C_U + GMLP_WIDTH
C_END = C_GV + GMLP_WIDTH


def _cparams(sem):
    return pltpu.CompilerParams(dimension_semantics=sem, vmem_limit_bytes=VMEM_LIMIT)


def _layer_norm(x, g, b):
    mu = jnp.mean(x, axis=-1, keepdims=True)
    xc = x - mu
    var = jnp.mean(xc * xc, axis=-1, keepdims=True)
    return xc * lax.rsqrt(var + LN_EPS) * g + b


def _sigmoid(x):
    return 1.0 / (1.0 + jnp.exp(-x))


def _dot(a, b):
    return jnp.dot(a, b, preferred_element_type=F32)


def _dot_nt(a, b):
    return lax.dot_general(a, b, (((1,), (1,)), ((), ())), preferred_element_type=F32)


def _sort_key(score):
    score = jnp.where(score == 0.0, 0.0, score)
    bits = lax.bitcast_convert_type(score, I32)
    return bits ^ ((bits >> 31) & jnp.int32(0x7FFFFFFF))


def _proj_kernel(x_ref, w_ref, cos_ref, sin_ref, lng_ref, lnb_ref,
                 qbd_ref, k_ref, v_ref, kb_ref, vb_ref, iq_ref, ikw_ref, ikb_ref,
                 p_ref, u_ref, gvn_ref):
    tm = x_ref.shape[0]
    xb = x_ref[...].astype(BF16)
    cos = cos_ref[...]
    sin = sin_ref[...]
    lane = lax.broadcasted_iota(I32, (tm, LANES), 1)
    first_half = (lane % HEAD_DIM) < (HEAD_DIM // 2)

    def mm(c0, width):
        return _dot(xb, w_ref[:, c0:c0 + width])

    def rope(z):
        partner = jnp.where(first_half, pltpu.roll(z, LANES - 32, 1), pltpu.roll(z, 32, 1))
        return z * cos + partner * sin

    def rope_wide(z):
        return jnp.concatenate([rope(z[:, s * LANES:(s + 1) * LANES]) for s in range(z.shape[1] // LANES)], axis=1)

    nblk = tm // BLK
    for h in range(N_HEADS):
        z = rope_wide(mm(C_Q + h * ATT_KV, ATT_KV)).astype(BF16)
        for b in range(nblk):
            qbd_ref[b, h] = z[b * BLK:(b + 1) * BLK]
    k = rope_wide(mm(C_K, ATT_KV))
    k_ref[...] = k
    kb_ref[...] = k.astype(BF16)
    v = mm(C_V, ATT_KV)
    v_ref[...] = v
    vb_ref[...] = v.astype(BF16)
    for h in range(N_IDX_HEADS):
        z = rope(mm(C_IQ + h * LANES, LANES)).astype(BF16)
        for b in range(nblk):
            iq_ref[b, h] = z[b * BLK:(b + 1) * BLK]
    z = mm(C_IKW, LANES)
    is_key = lane < IDX_DIM
    ikw = jnp.where(is_key, rope(z), z)
    ikw_ref[...] = ikw
    ikb_ref[...] = jnp.where(is_key, ikw, 0.0).astype(BF16)
    p_ref[...] = mm(C_P, POOL_WIDTH)
    u_ref[...] = mm(C_U, GMLP_WIDTH)
    gvn_ref[...] = _layer_norm(mm(C_GV, GMLP_WIDTH), lng_ref[...], lnb_ref[...])


def _proj(x, w_cat, cos_t, sin_t, ln_g, ln_b, tm):
    m = x.shape[0]
    nt = cos_t.shape[0] // tm
    row = lambda w: pl.BlockSpec((tm, w), lambda i: (i, 0))
    full = lambda a: pl.BlockSpec(a.shape, lambda i: (0,) * a.ndim)
    tab = pl.BlockSpec((tm, LANES), lambda i: (i % nt, 0))
    nb = tm // BLK
    out_shape = (
        jax.ShapeDtypeStruct((m // BLK, N_HEADS, BLK, ATT_KV), BF16),
        jax.ShapeDtypeStruct((m, ATT_KV), F32),
        jax.ShapeDtypeStruct((m, ATT_KV), F32),
        jax.ShapeDtypeStruct((m, ATT_KV), BF16),
        jax.ShapeDtypeStruct((m, ATT_KV), BF16),
        jax.ShapeDtypeStruct((m // BLK, N_IDX_HEADS, BLK, LANES), BF16),
        jax.ShapeDtypeStruct((m, LANES), F32),
        jax.ShapeDtypeStruct((m, LANES), BF16),
        jax.ShapeDtypeStruct((m, POOL_WIDTH), F32),
        jax.ShapeDtypeStruct((m, GMLP_WIDTH), F32),
        jax.ShapeDtypeStruct((m, GMLP_WIDTH), F32),
    )
    out_specs = (
        pl.BlockSpec((nb, N_HEADS, BLK, ATT_KV), lambda i: (i, 0, 0, 0)),
        row(ATT_KV), row(ATT_KV), row(ATT_KV), row(ATT_KV),
        pl.BlockSpec((nb, N_IDX_HEADS, BLK, LANES), lambda i: (i, 0, 0, 0)),
        row(LANES), row(LANES), row(POOL_WIDTH), row(GMLP_WIDTH), row(GMLP_WIDTH),
    )
    return pl.pallas_call(
        _proj_kernel,
        grid=(m // tm,),
        in_specs=[row(D_MODEL), full(w_cat), tab, tab, full(ln_g), full(ln_b)],
        out_specs=out_specs,
        out_shape=out_shape,
        compiler_params=_cparams(("parallel",)),
        name="proj",
    )(x, w_cat, cos_t, sin_t, ln_g, ln_b)


def _pool_kernel(p_ref, halo_ref, d_ref, ext_ref, s2_ref, s4_ref, s8_ref, *, pos_base, halo_is_prefix):
    tp = p_ref.shape[1]
    r_end = tp + 32
    i = pl.program_id(1)
    p = p_ref[0]
    halo = halo_ref[0]
    if not halo_is_prefix:
        halo = jnp.where(i == 0, 0.0, halo)
    ext_ref[0:16, :] = jnp.zeros((16, POOL_WIDTH), F32)
    ext_ref[16:32, :] = halo
    ext_ref[32:r_end, :] = p
    s2_ref[8:r_end, :] = ext_ref[8:r_end, :] + ext_ref[7:r_end - 1, :]
    s4_ref[16:r_end, :] = s2_ref[16:r_end, :] + s2_ref[14:r_end - 2, :]
    s8_ref[24:r_end, :] = s4_ref[24:r_end, :] + s4_ref[20:r_end - 4, :]
    s16 = s8_ref[32:r_end, :] + s8_ref[24:r_end - 8, :]
    lane = lax.broadcasted_iota(I32, (tp, POOL_WIDTH), 1)
    grp = lane // POOL_GC
    win = jnp.where(grp == 0, s2_ref[32:r_end, :],
                    jnp.where(grp == 1, s4_ref[32:r_end, :],
                              jnp.where(grp == 2, s8_ref[32:r_end, :], s16)))
    width = jnp.where(grp == 0, POOL_WINDOWS[0],
                      jnp.where(grp == 1, POOL_WINDOWS[1],
                                jnp.where(grp == 2, POOL_WINDOWS[2], POOL_WINDOWS[3])))
    pos = pos_base + i * tp + lax.broadcasted_iota(I32, (tp, POOL_WIDTH), 0)
    cnt = jnp.minimum(width, pos + 1).astype(F32)
    d_ref[0] = win / cnt - p


def _pool(p3, prefix16, pos_base, tp):
    n, t, _ = p3.shape
    halo_is_prefix = prefix16 is not None
    if halo_is_prefix:
        halo = prefix16
        halo_spec = pl.BlockSpec((1, 16, POOL_WIDTH), lambda b, i: (b, 0, 0))
    else:
        halo = p3
        step = tp // 16
        halo_spec = pl.BlockSpec((1, 16, POOL_WIDTH), lambda b, i: (b, jnp.maximum(i * step - 1, 0), 0))
    rows = tp + 32
    return pl.pallas_call(
        functools.partial(_pool_kernel, pos_base=pos_base, halo_is_prefix=halo_is_prefix),
        grid=(n, t // tp),
        in_specs=[pl.BlockSpec((1, tp, POOL_WIDTH), lambda b, i: (b, i, 0)), halo_spec],
        out_specs=pl.BlockSpec((1, tp, POOL_WIDTH), lambda b, i: (b, i, 0)),
        out_shape=jax.ShapeDtypeStruct((n, t, POOL_WIDTH), F32),
        scratch_shapes=[pltpu.VMEM((rows, POOL_WIDTH), F32) for _ in range(4)],
        compiler_params=_cparams(("parallel", "parallel")),
        name="pool",
    )(p3, halo)


def _kth_largest(sk_ref, nch, rows, topk):
    def count(pred_fn):
        def body(c, cnt):
            return cnt + jnp.where(pred_fn(sk_ref[c]), 1, 0)
        cnt = lax.fori_loop(0, nch, body, jnp.zeros((rows, LANES), I32))
        return jnp.sum(cnt.astype(F32), axis=1, keepdims=True)

    def bit_body(it, ubits):
        cand = ubits | lax.shift_left(jnp.int32(1), 31 - it)
        cand_b = jnp.broadcast_to(cand ^ jnp.int32(INT_MIN), (rows, LANES))
        tot = count(lambda s: s >= cand_b)
        return jnp.where(tot >= topk, cand, ubits)

    ubits = lax.fori_loop(0, 32, bit_body, jnp.zeros((rows, 1), I32))
    thr_b = jnp.broadcast_to(ubits ^ jnp.int32(INT_MIN), (rows, LANES))
    need_b = jnp.broadcast_to(topk - count(lambda s: s > thr_b), (rows, LANES))
    return thr_b, need_b


def _select_bias(skc, thr_b, need_b, eq_before, tri):
    eq = skc == thr_b
    res = _dot(jnp.where(eq, 1.0, 0.0).astype(BF16), tri)
    rank = res[:, :LANES] + eq_before
    keep = ((skc > thr_b) | (eq & (rank <= need_b))) & (skc > jnp.int32(INT_MIN))
    return jnp.where(keep, 0.0, MASKED), eq_before + res[:, LANES:]


def _tri_matrix():
    r = np.arange(LANES)
    incl = (r[:, None] <= r[None, :]).astype(np.float32)
    return jnp.asarray(np.concatenate([incl, np.ones((LANES, LANES), np.float32)], axis=1), BF16)


def _softmax_step(logits, m_ref, l_ref, acc_ref, p_ref, v):
    m_prev = m_ref[...]
    m_new = jnp.maximum(m_prev, jnp.max(logits, axis=1, keepdims=True))
    alpha = jnp.exp(m_prev - m_new)
    p = jnp.exp(logits - m_new)
    l_ref[...] = alpha * l_ref[...] + jnp.sum(p, axis=1, keepdims=True)
    m_ref[...] = m_new
    p_ref[...] = p.astype(BF16)
    acc_ref[...] = alpha * acc_ref[...] + _dot(p_ref[...], v)


def _dsa_prompt_kernel(iq_ref, ikw_ref, q_ref, ikb_ref, kb_ref, vb_ref, tri_ref, a_ref,
                       sk_ref, wb_ref, m_ref, l_ref, acc_ref, p_ref, *, topk):
    i = pl.program_id(1)
    nch = i + 1
    rows_h = N_IDX_HEADS * BLK
    iw = ikw_ref[0]
    for h in range(N_IDX_HEADS):
        wb_ref[h] = jnp.broadcast_to(iw[:, IDX_DIM + h:IDX_DIM + h + 1], (BLK, LANES))
    iq = iq_ref[0].reshape(rows_h, LANES)
    row_id = lax.broadcasted_iota(I32, (BLK, LANES), 0)
    col_id = lax.broadcasted_iota(I32, (BLK, LANES), 1)

    def score_chunk(c, carry):
        s = _dot_nt(iq, ikb_ref[0, c])
        score = jnp.zeros((BLK, LANES), F32)
        for h in range(N_IDX_HEADS):
            score = score + jnp.maximum(s[h * BLK:(h + 1) * BLK], 0.0) * wb_ref[h]
        key = _sort_key(score)
        sk_ref[c] = jnp.where(col_id > row_id + jnp.where(c == i, 0, BLK), jnp.int32(INT_MIN), key)
        return carry

    lax.fori_loop(0, nch, score_chunk, 0)
    thr_b, need_b = _kth_largest(sk_ref, nch, BLK, topk)

    m_ref[...] = jnp.full(m_ref.shape, MASKED, F32)
    l_ref[...] = jnp.zeros(l_ref.shape, F32)
    acc_ref[...] = jnp.zeros(acc_ref.shape, F32)
    q = q_ref[0].reshape(N_HEADS * BLK, ATT_KV)
    tri = tri_ref[...]

    def attend_chunk(c, eq_before):
        bias, eq_after = _select_bias(sk_ref[c], thr_b, need_b, eq_before, tri)
        logits = _dot_nt(q, kb_ref[0, c])
        logits = (logits.reshape(N_HEADS, BLK, LANES) + bias[None]).reshape(N_HEADS * BLK, LANES)
        _softmax_step(logits, m_ref, l_ref, acc_ref, p_ref, vb_ref[0, c])
        return eq_after

    lax.fori_loop(0, nch, attend_chunk, jnp.zeros((BLK, LANES), F32))
    a = acc_ref[...] / l_ref[...]
    a_ref[0] = a.astype(BF16).reshape(N_HEADS, BLK, ATT_KV)


def _dsa_prompt(iq, ikw, qbd, ikb, kb, vb, tri, n, t, topk):
    nb = t // BLK
    ikw3 = ikw.reshape(n, t, LANES)
    ikb4 = ikb.reshape(n, nb, BLK, LANES)
    kb4 = kb.reshape(n, nb, BLK, ATT_KV)
    vb4 = vb.reshape(n, nb, BLK, ATT_KV)
    rows = N_HEADS * BLK
    return pl.pallas_call(
        functools.partial(_dsa_prompt_kernel, topk=topk),
        grid=(n, nb),
        in_specs=[
            pl.BlockSpec((1, N_IDX_HEADS, BLK, LANES), lambda b, i: (b * nb + i, 0, 0, 0)),
            pl.BlockSpec((1, BLK, LANES), lambda b, i: (b, i, 0)),
            pl.BlockSpec((1, N_HEADS, BLK, ATT_KV), lambda b, i: (b * nb + i, 0, 0, 0)),
            pl.BlockSpec((1, nb, BLK, LANES), lambda b, i: (b, 0, 0, 0)),
            pl.BlockSpec((1, nb, BLK, ATT_KV), lambda b, i: (b, 0, 0, 0)),
            pl.BlockSpec((1, nb, BLK, ATT_KV), lambda b, i: (b, 0, 0, 0)),
            pl.BlockSpec(tri.shape, lambda b, i: (0, 0)),
        ],
        out_specs=pl.BlockSpec((1, N_HEADS, BLK, ATT_KV), lambda b, i: (b * nb + i, 0, 0, 0)),
        out_shape=jax.ShapeDtypeStruct((n * nb, N_HEADS, BLK, ATT_KV), BF16),
        scratch_shapes=[
            pltpu.VMEM((nb, BLK, LANES), I32),
            pltpu.VMEM((N_IDX_HEADS, BLK, LANES), F32),
            pltpu.VMEM((rows, 1), F32),
            pltpu.VMEM((rows, 1), F32),
            pltpu.VMEM((rows, ATT_KV), F32),
            pltpu.VMEM((rows, LANES), BF16),
        ],
        compiler_params=_cparams(("parallel", "arbitrary")),
        name="dsa_prompt",
    )(iq, ikw3, qbd, ikb4, kb4, vb4, tri)


def _sample_scores_kernel(pt_ref, iq_ref, wb_ref, *refs, pages_per_step):
    del pt_ref
    page_refs = refs[:pages_per_step]
    sc_ref = refs[pages_per_step]
    iq = iq_ref[0]
    wb = wb_ref[0]
    s_q = iq.shape[0] // N_IDX_HEADS
    for r in range(pages_per_step):
        s = _dot_nt(iq, page_refs[r][0].astype(BF16))
        t = jnp.maximum(s, 0.0) * wb
        sc_ref[0, r] = jnp.sum(t.reshape(N_IDX_HEADS, s_q, LANES), axis=0)


def _sample_scores(page_table, iq_s, wb_s, cache_ik3, layer_off, pages_per_step):
    ns, npages = page_table.shape
    rows = iq_s.shape[1]
    s_q = rows // N_IDX_HEADS
    steps = npages // pages_per_step

    def page_spec(r):
        return pl.BlockSpec((1, PAGE_SIZE, IDX_DIM),
                            lambda b, j, pt: (layer_off + pt[b, j * pages_per_step + r], 0, 0))

    grid_spec = pltpu.PrefetchScalarGridSpec(
        num_scalar_prefetch=1,
        grid=(ns, steps),
        in_specs=[pl.BlockSpec((1, rows, IDX_DIM), lambda b, j, pt: (b, 0, 0)),
                  pl.BlockSpec((1, rows, LANES), lambda b, j, pt: (b, 0, 0))]
                 + [page_spec(r) for r in range(pages_per_step)],
        out_specs=pl.BlockSpec((1, pages_per_step, s_q, LANES), lambda b, j, pt: (b, j, 0, 0)),
    )
    return pl.pallas_call(
        functools.partial(_sample_scores_kernel, pages_per_step=pages_per_step),
        grid_spec=grid_spec,
        out_shape=jax.ShapeDtypeStruct((ns, npages, s_q, LANES), F32),
        compiler_params=_cparams(("parallel", "arbitrary")),
        name="sample_scores",
    )(page_table, iq_s, wb_s, *([cache_ik3] * pages_per_step))


def _sample_select_kernel(sc_ref, iq_ref, wb_ref, ikn_ref, tri_ref, bias_ref, sk_ref, *, topk, s_q):
    sb = sc_ref.shape[0]
    npages = sc_ref.shape[1]
    rows = sb * s_q

    def key_chunk(c, carry):
        sk_ref[c] = _sort_key(sc_ref[:, c].reshape(rows, LANES))
        return carry

    lax.fori_loop(0, npages, key_chunk, 0)
    row_id = lax.broadcasted_iota(I32, (s_q, LANES), 0)
    col_id = lax.broadcasted_iota(I32, (s_q, LANES), 1)
    for b in range(sb):
        s = _dot_nt(iq_ref[b], ikn_ref[b])
        t = jnp.maximum(s, 0.0) * wb_ref[b]
        score = jnp.sum(t.reshape(N_IDX_HEADS, s_q, LANES), axis=0)
        key = jnp.where(col_id <= row_id, _sort_key(score), jnp.int32(INT_MIN))
        sk_ref[npages, b * s_q:(b + 1) * s_q, :] = key
    nch = npages + 1
    thr_b, need_b = _kth_largest(sk_ref, nch, rows, topk)
    tri = tri_ref[...]

    def bias_chunk(c, eq_before):
        bias, eq_after = _select_bias(sk_ref[c], thr_b, need_b, eq_before, tri)
        bias_ref[:, c] = bias.reshape(sb, s_q, LANES)
        return eq_after

    lax.fori_loop(0, nch, bias_chunk, jnp.zeros((rows, LANES), F32))


def _sample_select(scores, iq_s, wb_s, ikn, tri, topk, sb):
    ns, npages, s_q, _ = scores.shape
    rows = iq_s.shape[1]
    return pl.pallas_call(
        functools.partial(_sample_select_kernel, topk=topk, s_q=s_q),
        grid=(ns // sb,),
        in_specs=[
            pl.BlockSpec((sb, npages, s_q, LANES), lambda g: (g, 0, 0, 0)),
            pl.BlockSpec((sb, rows, IDX_DIM), lambda g: (g, 0, 0)),
            pl.BlockSpec((sb, rows, LANES), lambda g: (g, 0, 0)),
            pl.BlockSpec((sb, PAGE_SIZE, IDX_DIM), lambda g: (g, 0, 0)),
            pl.BlockSpec(tri.shape, lambda g: (0, 0)),
        ],
        out_specs=pl.BlockSpec((sb, npages + 1, s_q, LANES), lambda g: (g, 0, 0, 0)),
        out_shape=jax.ShapeDtypeStruct((ns, npages + 1, s_q, LANES), F32),
        scratch_shapes=[pltpu.VMEM((npages + 1, sb * s_q, LANES), I32)],
        compiler_params=_cparams(("parallel",)),
        name="sample_select",
    )(scores, iq_s, wb_s, ikn, tri)


def _sample_attend_kernel(pt_ref, q_ref, bias_ref, bias_new_ref, kn_ref, vn_ref, *refs, pages_per_step, s_q):
    del pt_ref
    k_refs = refs[:pages_per_step]
    v_refs = refs[pages_per_step:2 * pages_per_step]
    a_ref, m_ref, l_ref, acc_ref, p_ref = refs[2 * pages_per_step:]
    j = pl.program_id(1)
    rows = N_HEADS * s_q

    @pl.when(j == 0)
    def _():
        m_ref[...] = jnp.full(m_ref.shape, MASKED, F32)
        l_ref[...] = jnp.zeros(l_ref.shape, F32)
        acc_ref[...] = jnp.zeros(acc_ref.shape, F32)

    q = q_ref[0]

    def step(k, v, bias):
        logits = _dot_nt(q, k.astype(BF16))
        logits = (logits.reshape(N_HEADS, s_q, LANES) + bias[None]).reshape(rows, LANES)
        _softmax_step(logits, m_ref, l_ref, acc_ref, p_ref, v.astype(BF16))

    for r in range(pages_per_step):
        step(k_refs[r][0], v_refs[r][0], bias_ref[0, r])

    @pl.when(j == pl.num_programs(1) - 1)
    def _():
        step(kn_ref[0], vn_ref[0], bias_new_ref[0, 0])
        a_ref[0] = (acc_ref[...] / l_ref[...]).astype(BF16)


def _sample_attend(page_table, q_s, bias, kn, vn, cache_k3, cache_v3, layer_off, pages_per_step):
    ns, npages = page_table.shape
    rows = q_s.shape[1]
    s_q = rows // N_HEADS
    steps = npages // pages_per_step

    def page_spec(r):
        return pl.BlockSpec((1, PAGE_SIZE, ATT_KV),
                            lambda b, j, pt: (layer_off + pt[b, j * pages_per_step + r], 0, 0))

    new_spec = pl.BlockSpec((1, PAGE_SIZE, ATT_KV), lambda b, j, pt: (b, 0, 0))
    grid_spec = pltpu.PrefetchScalarGridSpec(
        num_scalar_prefetch=1,
        grid=(ns, steps),
        in_specs=[pl.BlockSpec((1, rows, ATT_KV), lambda b, j, pt: (b, 0, 0)),
                  pl.BlockSpec((1, pages_per_step, s_q, LANES), lambda b, j, pt: (b, j, 0, 0)),
                  pl.BlockSpec((1, 1, s_q, LANES), lambda b, j, pt: (b, npages, 0, 0)),
                  new_spec, new_spec]
                 + [page_spec(r) for r in range(pages_per_step)] * 2,
        out_specs=pl.BlockSpec((1, rows, ATT_KV), lambda b, j, pt: (b, 0, 0)),
        scratch_shapes=[
            pltpu.VMEM((rows, 1), F32),
            pltpu.VMEM((rows, 1), F32),
            pltpu.VMEM((rows, ATT_KV), F32),
            pltpu.VMEM((rows, LANES), BF16),
        ],
    )
    return pl.pallas_call(
        functools.partial(_sample_attend_kernel, pages_per_step=pages_per_step, s_q=s_q),
        grid_spec=grid_spec,
        out_shape=jax.ShapeDtypeStruct((ns, rows, ATT_KV), BF16),
        compiler_params=_cparams(("parallel", "arbitrary")),
        name="sample_attend",
    )(page_table, q_s, bias, bias, kn, vn,
      *([cache_k3] * pages_per_step), *([cache_v3] * pages_per_step))


def _merge_kernel(x_ref, a_ref, d_ref, u_ref, gvn_ref, wc_ref, gb_ref, wg_ref, wba_ref, pbd_ref, psc_ref,
                  wbp_ref, wbg_ref, wo_ref, lng_ref, lnb_ref, rwt_ref, rb_ref,
                  x1_ref, gate_ref, *, alpha):
    tm = x_ref.shape[0]
    x = x_ref[...]
    xb = x.astype(BF16)

    def gate(idx):
        return _sigmoid(_dot(xb, wg_ref[:, idx * D_MODEL:(idx + 1) * D_MODEL]))

    parts = []
    for b in range(tm // BLK):
        acc = jnp.zeros((BLK, D_MODEL), F32)
        for h in range(N_HEADS):
            acc = acc + _dot(a_ref[b, h], wba_ref[h])
        parts.append(acc)
    m = gate(0) * jnp.concatenate(parts, axis=0)
    y = _dot(d_ref[...].astype(BF16), pbd_ref[...]) * psc_ref[...]
    m = m + gate(1) * _dot(y.astype(BF16), wbp_ref[...])
    gv = gvn_ref[...].astype(BF16)
    grp = lax.broadcasted_iota(I32, (tm, GMLP_WIDTH), 1) // GMLP_GC
    mix = gb_ref[...]
    for g in range(GMLP_GROUPS):
        mix = mix + jnp.where(grp == g, _dot(wc_ref[g], gv), 0.0)
    c = u_ref[...] * mix
    m = m + gate(2) * _dot(c.astype(BF16), wbg_ref[...])
    y = _dot(m.astype(BF16), wo_ref[...])
    x1 = _layer_norm(alpha * x + y, lng_ref[...], lnb_ref[...])
    x1_ref[...] = x1

    scores = _sigmoid(_dot_nt(rwt_ref[...], x1.astype(BF16)))
    sel = scores + rb_ref[...]
    per = N_EXPERTS // N_EXPERT_GROUPS
    g3 = sel.reshape(N_EXPERT_GROUPS, per, tm)
    sub = lax.broadcasted_iota(I32, (N_EXPERT_GROUPS, per, tm), 1)
    m1 = jnp.max(g3, axis=1, keepdims=True)
    first = jnp.min(jnp.where(g3 == m1, sub, per), axis=1, keepdims=True)
    m2 = jnp.max(jnp.where(sub == first, -jnp.inf, g3), axis=1, keepdims=True)
    gs = (m1 + m2).reshape(N_EXPERT_GROUPS, tm)

    def rank_of(vals, count):
        idx = lax.broadcasted_iota(I32, vals.shape, 0)
        rank = jnp.zeros(vals.shape, I32)
        for o in range(count):
            other = vals[o:o + 1]
            beats = (other > vals) | ((other == vals) & (o < idx))
            rank = rank + jnp.where(beats, 1, 0)
        return rank

    gkeep = rank_of(gs, N_EXPERT_GROUPS) < TOPK_GROUPS
    ekeep = jnp.broadcast_to(gkeep.reshape(N_EXPERT_GROUPS, 1, tm), (N_EXPERT_GROUPS, per, tm)).reshape(N_EXPERTS, tm)
    sel = jnp.where(ekeep, sel, -jnp.inf)
    chosen = rank_of(sel, N_EXPERTS) < TOPK_EXPERTS
    wsel = jnp.where(chosen, scores, 0.0)
    gate_t = wsel / jnp.sum(wsel, axis=0, keepdims=True) * ROUTED_SCALE
    gate_ref[...] = gate_t.T


def _merge(x, a_bd, d, u, gvn, wl, tm, alpha):
    m = x.shape[0]
    nb = tm // BLK
    row = lambda w: pl.BlockSpec((tm, w), lambda i: (i, 0))
    full = lambda a: pl.BlockSpec(a.shape, lambda i: (0,) * a.ndim)
    weights = [wl["wc"], wl["gb"], wl["wg"], wl["wba"], wl["pbd"], wl["psc"], wl["wbp"], wl["wbg"], wl["wo"],
               wl["ln1_g"], wl["ln1_b"], wl["rwt"], wl["rb"]]
    return pl.pallas_call(
        functools.partial(_merge_kernel, alpha=alpha),
        grid=(m // tm,),
        in_specs=[row(D_MODEL),
                  pl.BlockSpec((nb, N_HEADS, BLK, ATT_KV), lambda i: (i, 0, 0, 0)),
                  row(POOL_WIDTH), row(GMLP_WIDTH), row(GMLP_WIDTH)] + [full(w) for w in weights],
        out_specs=(row(D_MODEL), row(N_EXPERTS)),
        out_shape=(jax.ShapeDtypeStruct((m, D_MODEL), F32), jax.ShapeDtypeStruct((m, N_EXPERTS), F32)),
        compiler_params=_cparams(("parallel",)),
        name="merge",
    )(x, a_bd, d, u, gvn, *weights)


def _swiglu(xb, w_gu, w_down):
    h = _dot(xb, w_gu)
    g = h[:, :EXPERT_FF]
    act = g * _sigmoid(g) * h[:, EXPERT_FF:]
    return _dot(act.astype(BF16), w_down)


def _moe_kernel(x_ref, gate_ref, wgu_ref, wd_ref, sgu_ref, sd_ref, lng_ref, lnb_ref, o_ref, xb_ref, acc_ref,
                *, alpha):
    e = pl.program_id(1)

    @pl.when(e == 0)
    def _():
        xb_ref[...] = x_ref[...].astype(BF16)
        acc_ref[...] = _swiglu(xb_ref[...], sgu_ref[...], sd_ref[...])

    y = _swiglu(xb_ref[...], wgu_ref[0].astype(BF16), wd_ref[0].astype(BF16))
    gate = gate_ref[...]
    lane = lax.broadcasted_iota(I32, gate.shape, 1)
    gcol = jnp.sum(jnp.where(lane == e, gate, 0.0), axis=1, keepdims=True)
    acc_ref[...] += gcol * y

    @pl.when(e == pl.num_programs(1) - 1)
    def _():
        o_ref[...] = _layer_norm(alpha * x_ref[...] + acc_ref[...], lng_ref[...], lnb_ref[...])


def _moe(x1, gate, w_gu, w_down, sh_gu, sh_down, ln_g, ln_b, tm, alpha):
    m = x1.shape[0]
    full = lambda a: pl.BlockSpec(a.shape, lambda i, e: (0,) * a.ndim)
    return pl.pallas_call(
        functools.partial(_moe_kernel, alpha=alpha),
        grid=(m // tm, N_EXPERTS),
        in_specs=[
            pl.BlockSpec((tm, D_MODEL), lambda i, e: (i, 0)),
            pl.BlockSpec((tm, N_EXPERTS), lambda i, e: (i, 0)),
            pl.BlockSpec((1, D_MODEL, 2 * EXPERT_FF), lambda i, e: (e, 0, 0)),
            pl.BlockSpec((1, EXPERT_FF, D_MODEL), lambda i, e: (e, 0, 0)),
            full(sh_gu), full(sh_down), full(ln_g), full(ln_b),
        ],
        out_specs=pl.BlockSpec((tm, D_MODEL), lambda i, e: (i, 0)),
        out_shape=jax.ShapeDtypeStruct((m, D_MODEL), F32),
        scratch_shapes=[pltpu.VMEM((tm, D_MODEL), BF16), pltpu.VMEM((tm, D_MODEL), F32)],
        compiler_params=_cparams(("parallel", "arbitrary")),
        name="moe",
    )(x1, gate, w_gu, w_down, sh_gu, sh_down, ln_g, ln_b)


def _rope_tables(pos):
    half = HEAD_DIM // 2
    inv = ROPE_THETA ** (-jnp.arange(half, dtype=F32) / half)
    ang = pos.astype(F32)[:, None] * inv[None, :]
    cos, sin = jnp.cos(ang), jnp.sin(ang)
    cos_t = jnp.tile(cos, (1, LANES // half))
    sin_t = jnp.tile(jnp.concatenate([-sin, sin], axis=1), (1, LANES // HEAD_DIM))
    return cos_t, sin_t


def _block_diag(blocks):
    g, r, c = blocks.shape
    eye = jnp.eye(g, dtype=blocks.dtype)
    return jnp.einsum("grc,gh->grhc", blocks, eye).reshape(g * r, g * c)


def _layer_weights(l, w_in, w_ba, w_bp, w_bg, w_out, pool_w, pool_scale, g_ln_g, g_ln_b, g_ws, g_b,
                   ln1_g, ln1_b, router_w, router_bias, sh_gu, sh_down, ln2_g, ln2_b):
    w = w_in[l]
    sizes = (ATT_Q, ATT_KV, ATT_KV, N_IDX_HEADS * IDX_DIM, IDX_DIM, N_IDX_HEADS,
             POOL_WIDTH, GMLP_WIDTH, GMLP_WIDTH, N_BRANCH * D_MODEL)
    offs = np.concatenate([[0], np.cumsum(sizes)]).tolist()
    wq, wk, wv, wiq, wik, wiw, wp, wu, wgv, wg = [w[:, offs[j]:offs[j + 1]] for j in range(len(sizes))]
    group_of_head = jnp.asarray(np.eye(N_KV_HEADS, dtype=np.float32)[np.arange(N_HEADS) // (N_HEADS // N_KV_HEADS)])
    wq_bd = jnp.einsum("dhc,hg->dhgc", wq.reshape(D_MODEL, N_HEADS, HEAD_DIM) * HEAD_DIM ** -0.5,
                       group_of_head).reshape(D_MODEL, QBD_W)
    wiq_pad = jnp.pad(wiq.reshape(D_MODEL, N_IDX_HEADS, IDX_DIM),
                      ((0, 0), (0, 0), (0, LANES - IDX_DIM))).reshape(D_MODEL, IQ_W)
    wikw = jnp.pad(jnp.concatenate([wik, wiw], axis=1), ((0, 0), (0, LANES - IDX_DIM - N_IDX_HEADS)))
    w_cat = jnp.concatenate([wq_bd, wk, wv, wiq_pad, wikw, wp, wu, wgv], axis=1).astype(BF16)
    wba = jnp.einsum("hcd,hg->hgcd", w_ba[l].reshape(N_HEADS, HEAD_DIM, D_MODEL),
                     group_of_head).reshape(N_HEADS, ATT_KV, D_MODEL).astype(BF16)
    return dict(
        w_cat=w_cat, wg=wg.astype(BF16), wba=wba,
        pbd=_block_diag(pool_w[l]).astype(BF16), psc=pool_scale[l].reshape(1, POOL_WIDTH),
        wbp=w_bp[l].astype(BF16), wbg=w_bg[l].astype(BF16), wo=w_out[l].astype(BF16),
        g_ln_g=g_ln_g[l].reshape(1, GMLP_WIDTH), g_ln_b=g_ln_b[l].reshape(1, GMLP_WIDTH),
        ws=g_ws[l], gbias=g_b[l],
        ln1_g=ln1_g[l].reshape(1, D_MODEL), ln1_b=ln1_b[l].reshape(1, D_MODEL),
        rwt=router_w[l].T.astype(BF16), rbias=router_bias[l],
        sh_gu=sh_gu[l].astype(BF16), sh_down=sh_down[l].astype(BF16),
        ln2_g=ln2_g[l].reshape(1, D_MODEL), ln2_b=ln2_b[l].reshape(1, D_MODEL),
    )


def _chunk_mix(wl, cl, tm):
    tril = jnp.tril(jnp.ones((cl, cl), F32))
    wm = wl["ws"][:, :cl, :cl] * tril
    reps = tm // cl
    wc = jnp.stack([_block_diag(jnp.broadcast_to(wm[g], (reps, cl, cl))) for g in range(GMLP_GROUPS)])
    gb = jnp.tile(jnp.repeat(wl["gbias"][:, :cl].T, GMLP_GC, axis=1), (reps, 1))
    return wc.astype(BF16), gb


def _to_rows_by_seq(a, ns, s_q):
    nblk, heads, _, width = a.shape
    per = BLK // s_q
    a = a.reshape(nblk, heads, per, s_q, width).transpose(0, 2, 1, 3, 4)
    return a.reshape(ns, heads * s_q, width)


def _from_rows_by_seq(a, s_q):
    ns, rows, width = a.shape
    heads = rows // s_q
    per = BLK // s_q
    a = a.reshape(ns // per, per, heads, s_q, width).transpose(0, 2, 1, 3, 4)
    return a.reshape(ns // per, heads, BLK, width)


def kernel(x_prompt, x_sample, cache_k, cache_v, cache_idx_k, state_pool, page_table, w_in, w_branch_attn,
           w_branch_pool, w_branch_gmlp, w_out, pool_w, pool_scale, gmlp_ln_g, gmlp_ln_b, gmlp_ws, gmlp_b,
           ln1_g, ln1_b, router_w, router_bias, expert_w_gu, expert_w_down, shared_w_gu, shared_w_down,
           ln2_g, ln2_b):
    n_p, t_p, _ = x_prompt.shape
    n_s, t_s, _ = x_sample.shape
    depth = w_in.shape[0]
    n_pool = cache_k.shape[1]
    npages = page_table.shape[1]
    past = npages * PAGE_SIZE
    m_p, m_s = n_p * t_p, n_s * t_s
    alpha = (2 * depth) ** 0.25
    assert t_p % CHUNK == 0 and m_s % BLK == 0 and BLK % t_s == 0 and t_s <= 16

    tm_p = 256
    tm_s = m_s if m_s <= 256 else 256
    tm_moe = 1024 if m_p % 1024 == 0 else tm_p
    tp_pool = 512 if t_p % 512 == 0 else CHUNK
    pages_per_step = 8 if npages % 8 == 0 else 1
    sel_batch = 8 if n_s % 8 == 0 else 1
    topk_p = min(TOPK_MAX, t_p // 4)
    topk_s = min(TOPK_MAX, (past + t_s) // 4)

    cos_p, sin_p = _rope_tables(jnp.arange(t_p, dtype=I32))
    cos_s, sin_s = _rope_tables(past + jnp.arange(t_s, dtype=I32))
    cos_s, sin_s = jnp.tile(cos_s, (m_s // t_s, 1)), jnp.tile(sin_s, (m_s // t_s, 1))
    tri = _tri_matrix()
    cache_k3 = cache_k.reshape(depth * n_pool, PAGE_SIZE, ATT_KV)
    cache_v3 = cache_v.reshape(depth * n_pool, PAGE_SIZE, ATT_KV)
    cache_ik3 = cache_idx_k.reshape(depth * n_pool, PAGE_SIZE, IDX_DIM)

    hp = x_prompt.reshape(m_p, D_MODEL)
    hs = x_sample.reshape(m_s, D_MODEL)
    outs = {name: [] for name in ("kp", "vp", "ikp", "pp", "ks", "vs", "iks", "ps", "gs")}
    for l in range(depth):
        wl = _layer_weights(l, w_in, w_branch_attn, w_branch_pool, w_branch_gmlp, w_out, pool_w, pool_scale,
                            gmlp_ln_g, gmlp_ln_b, gmlp_ws, gmlp_b, ln1_g, ln1_b, router_w, router_bias,
                            shared_w_gu, shared_w_down, ln2_g, ln2_b)

        def finish(x, a_bd, d, u, gvn, cl, tm, tm_e):
            wc, gb = _chunk_mix(wl, cl, tm)
            wm = dict(wl, wc=wc, gb=gb, rb=jnp.broadcast_to(wl["rbias"][:, None], (N_EXPERTS, tm)))
            x1, gate = _merge(x, a_bd, d, u, gvn, wm, tm, alpha)
            return _moe(x1, gate, expert_w_gu[l], expert_w_down[l], wl["sh_gu"], wl["sh_down"],
                        wl["ln2_g"], wl["ln2_b"], tm_e, alpha)

        qbd, k, v, kb, vb, iq, ikw, ikb, p, u, gvn = _proj(hp, wl["w_cat"], cos_p, sin_p,
                                                           wl["g_ln_g"], wl["g_ln_b"], tm_p)
        p3 = p.reshape(n_p, t_p, POOL_WIDTH)
        d = _pool(p3, None, 0, tp_pool).reshape(m_p, POOL_WIDTH)
        a_bd = _dsa_prompt(iq, ikw, qbd, ikb, kb, vb, tri, n_p, t_p, topk_p)
        hp = finish(hp, a_bd, d, u, gvn, CHUNK, tm_p, tm_moe)
        outs["kp"].append(k.reshape(n_p, t_p, N_KV_HEADS, HEAD_DIM))
        outs["vp"].append(v.reshape(n_p, t_p, N_KV_HEADS, HEAD_DIM))
        outs["ikp"].append(ikw[:, :IDX_DIM].reshape(n_p, t_p, IDX_DIM))
        outs["pp"].append(p3[:, t_p - POOL_STATE:])

        qbd, k, v, kb, vb, iq, ikw, ikb, p, u, gvn = _proj(hs, wl["w_cat"], cos_s, sin_s,
                                                           wl["g_ln_g"], wl["g_ln_b"], tm_s)
        p3 = p.reshape(n_s, t_s, POOL_WIDTH)
        prefix16 = jnp.pad(state_pool[l], ((0, 0), (16 - POOL_STATE, 0), (0, 0)))
        d = _pool(p3, prefix16, past, t_s).reshape(m_s, POOL_WIDTH)
        iq_s = _to_rows_by_seq(iq, n_s, t_s)[:, :, :IDX_DIM]
        q_s = _to_rows_by_seq(qbd, n_s, t_s)
        iw = ikw[:, IDX_DIM:IDX_DIM + N_IDX_HEADS].reshape(n_s, t_s, N_IDX_HEADS)
        wb_s = jnp.broadcast_to(iw.transpose(0, 2, 1).reshape(n_s, N_IDX_HEADS * t_s, 1),
                                (n_s, N_IDX_HEADS * t_s, LANES))
        pad_rows = lambda a: jnp.pad(a.reshape(n_s, t_s, -1), ((0, 0), (0, PAGE_SIZE - t_s), (0, 0)))
        ikn = pad_rows(ikb[:, :IDX_DIM])
        scores = _sample_scores(page_table, iq_s, wb_s, cache_ik3, l * n_pool, pages_per_step)
        bias = _sample_select(scores, iq_s, wb_s, ikn, tri, topk_s, sel_batch)
        a_s = _sample_attend(page_table, q_s, bias, pad_rows(k), pad_rows(v), cache_k3, cache_v3,
                             l * n_pool, pages_per_step)
        a_bd = _from_rows_by_seq(a_s, t_s)
        hs = finish(hs, a_bd, d, u, gvn, t_s, tm_s, tm_s)
        outs["ks"].append(k.reshape(n_s, t_s, N_KV_HEADS, HEAD_DIM))
        outs["vs"].append(v.reshape(n_s, t_s, N_KV_HEADS, HEAD_DIM))
        outs["iks"].append(ikw[:, :IDX_DIM].reshape(n_s, t_s, IDX_DIM))
        outs["ps"].append(jnp.concatenate([state_pool[l], p3], axis=1)[:, -POOL_STATE:])
        outs["gs"].append(gvn.reshape(n_s, t_s, GMLP_WIDTH))

    st = lambda name: jnp.stack(outs[name])
    return (hp.reshape(n_p, t_p, D_MODEL), hs.reshape(n_s, t_s, D_MODEL),
            st("kp"), st("vp"), st("ikp"), st("ks"), st("vs"), st("iks"), st("pp"), st("ps"), st("gs"))
```

```python
import functools

import jax
import jax.numpy as jnp
import numpy as np
from jax import lax
from jax.experimental import pallas as pl
from jax.experimental.pallas import tpu as pltpu

F32 = jnp.float32
BF16 = jnp.bfloat16
I32 = jnp.int32

D_MODEL = 1024
N_HEADS = 8
N_KV_HEADS = 4
HEAD_DIM = 64
N_IDX_HEADS = 8
IDX_DIM = 64
TOPK_MAX = 256
PAGE_SIZE = 128
ROPE_THETA = 10000.0
POOL_WINDOWS = (2, 4, 8, 16)
POOL_WIDTH = 256
POOL_GC = 64
POOL_STATE = 15
GMLP_WIDTH = 256
GMLP_GROUPS = 4
GMLP_GC = 64
CHUNK = 128
N_BRANCH = 3
ATT_Q = N_HEADS * HEAD_DIM
ATT_KV = N_KV_HEADS * HEAD_DIM
N_EXPERTS = 64
TOPK_EXPERTS = 8
N_EXPERT_GROUPS = 8
TOPK_GROUPS = 4
EXPERT_FF = 256
ROUTED_SCALE = 2.5
LN_EPS = 1e-5

LANES = 128
SUBLANES = 8
BLK = 256
MOE_EXPERTS_PER_STEP = 2
INT_MIN = -2 ** 31
MASKED = -1e30
LOG2E = 1.4426950408889634
VMEM_LIMIT = 56 * 1024 * 1024

QBD_W = N_HEADS * ATT_KV
IQ_W = N_IDX_HEADS * LANES
C_Q = 0
C_K = C_Q + QBD_W
C_V = C_K + ATT_KV
C_IQ = C_V + ATT_KV
C_IKW = C_IQ + IQ_W
C_P = C_IKW + LANES
C_U = C_P + POOL_WIDTH
C_GV = C_U + GMLP_WIDTH
C_END = C_GV + GMLP_WIDTH


def _cparams(sem):
    return pltpu.CompilerParams(dimension_semantics=sem, vmem_limit_bytes=VMEM_LIMIT)


def _layer_norm(x, g, b):
    mu = jnp.mean(x, axis=-1, keepdims=True)
    xc = x - mu
    var = jnp.mean(xc * xc, axis=-1, keepdims=True)
    return xc * lax.rsqrt(var + LN_EPS) * g + b


def _sigmoid(x):
    return 1.0 / (1.0 + jnp.exp(-x))


def _dot(a, b):
    return jnp.dot(a, b, preferred_element_type=F32)


def _dot_nt(a, b):
    return lax.dot_general(a, b, (((1,), (1,)), ((), ())), preferred_element_type=F32)


def _sort_key(score):
    score = jnp.where(score == 0.0, 0.0, score)
    bits = lax.bitcast_convert_type(score, I32)
    return bits ^ ((bits >> 31) & jnp.int32(0x7FFFFFFF))


def _proj_kernel(x_ref, w_ref, wvt_ref, wiwt_ref, cos_ref, sin_ref, lng_ref, lnb_ref,
                 qbd_ref, k_ref, v_ref, kb_ref, vt_ref, iq_ref, ikw_ref, ikb_ref, iwt_ref,
                 p_ref, u_ref, gvn_ref):
    tm = x_ref.shape[0]
    xb = x_ref[...].astype(BF16)
    cos = cos_ref[...]
    sin = sin_ref[...]
    lane = lax.broadcasted_iota(I32, (tm, LANES), 1)
    first_half = (lane % HEAD_DIM) < (HEAD_DIM // 2)

    def mm(c0, width):
        return _dot(xb, w_ref[:, c0:c0 + width])

    def rope(z):
        partner = jnp.where(first_half, pltpu.roll(z, LANES - 32, 1), pltpu.roll(z, 32, 1))
        return z * cos + partner * sin

    def rope_wide(z):
        return jnp.concatenate([rope(z[:, s * LANES:(s + 1) * LANES]) for s in range(z.shape[1] // LANES)], axis=1)

    nblk = tm // BLK
    for h in range(N_HEADS):
        z = rope_wide(mm(C_Q + h * ATT_KV, ATT_KV)).astype(BF16)
        for b in range(nblk):
            qbd_ref[b, h] = z[b * BLK:(b + 1) * BLK]
    k = rope_wide(mm(C_K, ATT_KV))
    k_ref[...] = k
    kb_ref[...] = k.astype(BF16)
    v_ref[...] = mm(C_V, ATT_KV)
    vt = _dot_nt(wvt_ref[...], xb).astype(BF16)
    for b in range(nblk):
        vt_ref[b] = vt[:, b * BLK:(b + 1) * BLK]
    for h in range(N_IDX_HEADS):
        z = rope(mm(C_IQ + h * LANES, LANES)).astype(BF16)
        for b in range(nblk):
            iq_ref[b, h] = z[b * BLK:(b + 1) * BLK]
    z = mm(C_IKW, LANES)
    is_key = lane < IDX_DIM
    ikw = jnp.where(is_key, rope(z), z)
    ikw_ref[...] = ikw
    ikb_ref[...] = jnp.where(is_key, ikw, 0.0).astype(BF16)
    iwt_ref[...] = _dot_nt(wiwt_ref[...], xb)
    p_ref[...] = mm(C_P, POOL_WIDTH)
    u_ref[...] = mm(C_U, GMLP_WIDTH)
    gvn_ref[...] = _layer_norm(mm(C_GV, GMLP_WIDTH), lng_ref[...], lnb_ref[...])


def _proj(x, wl, cos_t, sin_t, tm):
    m = x.shape[0]
    nt = cos_t.shape[0] // tm
    row = lambda w: pl.BlockSpec((tm, w), lambda i: (i, 0))
    full = lambda a: pl.BlockSpec(a.shape, lambda i: (0,) * a.ndim)
    tab = pl.BlockSpec((tm, LANES), lambda i: (i % nt, 0))
    nb = tm // BLK
    out_shape = (
        jax.ShapeDtypeStruct((m // BLK, N_HEADS, BLK, ATT_KV), BF16),
        jax.ShapeDtypeStruct((m, ATT_KV), F32),
        jax.ShapeDtypeStruct((m, ATT_KV), F32),
        jax.ShapeDtypeStruct((m, ATT_KV), BF16),
        jax.ShapeDtypeStruct((m // BLK, ATT_KV, BLK), BF16),
        jax.ShapeDtypeStruct((m // BLK, N_IDX_HEADS, BLK, LANES), BF16),
        jax.ShapeDtypeStruct((m, LANES), F32),
        jax.ShapeDtypeStruct((m, LANES), BF16),
        jax.ShapeDtypeStruct((N_IDX_HEADS, m), F32),
        jax.ShapeDtypeStruct((m, POOL_WIDTH), F32),
        jax.ShapeDtypeStruct((m, GMLP_WIDTH), F32),
        jax.ShapeDtypeStruct((m, GMLP_WIDTH), F32),
    )
    out_specs = (
        pl.BlockSpec((nb, N_HEADS, BLK, ATT_KV), lambda i: (i, 0, 0, 0)),
        row(ATT_KV), row(ATT_KV), row(ATT_KV),
        pl.BlockSpec((nb, ATT_KV, BLK), lambda i: (i, 0, 0)),
        pl.BlockSpec((nb, N_IDX_HEADS, BLK, LANES), lambda i: (i, 0, 0, 0)),
        row(LANES), row(LANES),
        pl.BlockSpec((N_IDX_HEADS, tm), lambda i: (0, i)),
        row(POOL_WIDTH), row(GMLP_WIDTH), row(GMLP_WIDTH),
    )
    weights = [wl["w_cat"], wl["wvt"], wl["wiwt"]]
    return pl.pallas_call(
        _proj_kernel,
        grid=(m // tm,),
        in_specs=[row(D_MODEL)] + [full(w) for w in weights] + [tab, tab, full(wl["g_ln_g"]), full(wl["g_ln_b"])],
        out_specs=out_specs,
        out_shape=out_shape,
        compiler_params=_cparams(("parallel",)),
        name="proj",
    )(x, *weights, cos_t, sin_t, wl["g_ln_g"], wl["g_ln_b"])


def _pool_kernel(p_ref, halo_ref, d_ref, ext_ref, s2_ref, s4_ref, s8_ref, *, pos_base, halo_is_prefix):
    tp = p_ref.shape[1]
    r_end = tp + 32
    i = pl.program_id(1)
    p = p_ref[0]
    halo = halo_ref[0]
    if not halo_is_prefix:
        halo = jnp.where(i == 0, 0.0, halo)
    ext_ref[0:16, :] = jnp.zeros((16, POOL_WIDTH), F32)
    ext_ref[16:32, :] = halo
    ext_ref[32:r_end, :] = p
    s2_ref[8:r_end, :] = ext_ref[8:r_end, :] + ext_ref[7:r_end - 1, :]
    s4_ref[16:r_end, :] = s2_ref[16:r_end, :] + s2_ref[14:r_end - 2, :]
    s8_ref[24:r_end, :] = s4_ref[24:r_end, :] + s4_ref[20:r_end - 4, :]
    s16 = s8_ref[32:r_end, :] + s8_ref[24:r_end - 8, :]
    lane = lax.broadcasted_iota(I32, (tp, POOL_WIDTH), 1)
    grp = lane // POOL_GC
    win = jnp.where(grp == 0, s2_ref[32:r_end, :],
                    jnp.where(grp == 1, s4_ref[32:r_end, :],
                              jnp.where(grp == 2, s8_ref[32:r_end, :], s16)))
    width = jnp.where(grp == 0, POOL_WINDOWS[0],
                      jnp.where(grp == 1, POOL_WINDOWS[1],
                                jnp.where(grp == 2, POOL_WINDOWS[2], POOL_WINDOWS[3])))
    pos = pos_base + i * tp + lax.broadcasted_iota(I32, (tp, POOL_WIDTH), 0)
    cnt = jnp.minimum(width, pos + 1).astype(F32)
    d_ref[0] = win / cnt - p


def _pool(p3, prefix16, pos_base, tp):
    n, t, _ = p3.shape
    halo_is_prefix = prefix16 is not None
    if halo_is_prefix:
        halo = prefix16
        halo_spec = pl.BlockSpec((1, 16, POOL_WIDTH), lambda b, i: (b, 0, 0))
    else:
        halo = p3
        step = tp // 16
        halo_spec = pl.BlockSpec((1, 16, POOL_WIDTH), lambda b, i: (b, jnp.maximum(i * step - 1, 0), 0))
    rows = tp + 32
    return pl.pallas_call(
        functools.partial(_pool_kernel, pos_base=pos_base, halo_is_prefix=halo_is_prefix),
        grid=(n, t // tp),
        in_specs=[pl.BlockSpec((1, tp, POOL_WIDTH), lambda b, i: (b, i, 0)), halo_spec],
        out_specs=pl.BlockSpec((1, tp, POOL_WIDTH), lambda b, i: (b, i, 0)),
        out_shape=jax.ShapeDtypeStruct((n, t, POOL_WIDTH), F32),
        scratch_shapes=[pltpu.VMEM((rows, POOL_WIDTH), F32) for _ in range(4)],
        compiler_params=_cparams(("parallel", "parallel")),
        name="pool",
    )(p3, halo)


def _kth_largest(sk_ref, nch, topk, keys_on_lanes, unroll=1):
    rows, cols = sk_ref.shape[1:]
    if keys_on_lanes:
        vec, acc_shape = (rows, 1), (rows, cols)
        fold = lambda x: x
        total = lambda cnt: jnp.sum(cnt.astype(F32), axis=1, keepdims=True)
    else:
        vec, acc_shape = (1, cols), (SUBLANES, cols)
        fold = lambda x: jnp.sum(x.reshape(rows // SUBLANES, SUBLANES, cols), axis=0)
        total = lambda cnt: jnp.sum(cnt.astype(F32), axis=0, keepdims=True)

    def count(pred_fn):
        def body(c, cnt):
            return cnt + fold(jnp.where(pred_fn(sk_ref[c]), 1, 0))
        return total(lax.fori_loop(0, nch, body, jnp.zeros(acc_shape, I32), unroll=unroll))

    def bit_body(it, ubits):
        cand = ubits | lax.shift_left(jnp.int32(1), 31 - it)
        cand_b = jnp.broadcast_to(cand ^ jnp.int32(INT_MIN), (rows, cols))
        tot = count(lambda s: s >= cand_b)
        return jnp.where(tot >= topk, cand, ubits)

    ubits = lax.fori_loop(0, 32, bit_body, jnp.zeros(vec, I32))
    thr_b = jnp.broadcast_to(ubits ^ jnp.int32(INT_MIN), (rows, cols))
    need_b = jnp.broadcast_to(topk - count(lambda s: s > thr_b), (rows, cols))
    return thr_b, need_b


def _select_bias(skc, thr_b, need_b, eq_before, tri, keys_on_lanes):
    rows, cols = skc.shape
    eq = skc == thr_b
    eqf = jnp.where(eq, 1.0, 0.0).astype(BF16)
    if keys_on_lanes:
        res = _dot(eqf, tri)
        prefix, chunk_total = res[:, :cols], res[:, cols:]
    else:
        res = _dot(tri, eqf)
        prefix, chunk_total = res[:rows], res[rows:]
    keep = ((skc > thr_b) | (eq & (prefix + eq_before <= need_b))) & (skc > jnp.int32(INT_MIN))
    return jnp.where(keep, 0.0, MASKED), eq_before + chunk_total


def _tri_matrix(n, keys_on_lanes):
    r = np.arange(n)
    ones = np.ones((n, n), np.float32)
    if keys_on_lanes:
        return jnp.asarray(np.concatenate([(r[:, None] <= r[None, :]).astype(np.float32), ones], axis=1), BF16)
    return jnp.asarray(np.concatenate([(r[:, None] >= r[None, :]).astype(np.float32), ones], axis=0), BF16)


def _dsa_prompt_kernel(iq_ref, iwt_ref, q_ref, ikb_ref, kb_ref, vt_ref, tri_ref, a_ref,
                       sk_ref, m_ref, l_ref, acc_ref, *, topk):
    i = pl.program_id(1)
    nch = i + 1
    iq = iq_ref[0].reshape(N_IDX_HEADS * BLK, LANES)
    iwt = iwt_ref[...]
    key_id = lax.broadcasted_iota(I32, (BLK, BLK), 0)
    q_id = lax.broadcasted_iota(I32, (BLK, BLK), 1)

    def score_chunk(c, carry):
        s = _dot_nt(ikb_ref[0, c], iq)
        score = jnp.zeros((BLK, BLK), F32)
        for h in range(N_IDX_HEADS):
            score = score + jnp.maximum(s[:, h * BLK:(h + 1) * BLK], 0.0) * iwt[h:h + 1, :]
        sk_ref[c] = jnp.where(key_id > q_id + jnp.where(c == i, 0, BLK), jnp.int32(INT_MIN), _sort_key(score))
        return carry

    lax.fori_loop(0, nch, score_chunk, 0)
    thr_b, need_b = _kth_largest(sk_ref, nch, topk, keys_on_lanes=False)

    m_ref[...] = jnp.full(m_ref.shape, MASKED, F32)
    l_ref[...] = jnp.zeros(l_ref.shape, F32)
    acc_ref[...] = jnp.zeros(acc_ref.shape, F32)
    tri = tri_ref[...]
    heads_per_group = N_HEADS // N_KV_HEADS

    def attend_chunk(c, eq_before):
        bias, eq_after = _select_bias(sk_ref[c], thr_b, need_b, eq_before, tri, keys_on_lanes=False)
        kc = kb_ref[0, c]
        heads = range(N_HEADS)
        cols = [slice(h * BLK, (h + 1) * BLK) for h in heads]
        lgs = [_dot_nt(kc, q_ref[0, h]) + bias for h in heads]
        m_prev = [m_ref[:, cs] for cs in cols]
        m_new = [jnp.maximum(m_prev[h], jnp.max(lgs[h], axis=0, keepdims=True)) for h in heads]
        alpha = [jnp.exp2(m_prev[h] - m_new[h]) for h in heads]
        ps = [jnp.exp2(lgs[h] - m_new[h]) for h in heads]
        for h in heads:
            l_ref[:, cols[h]] = alpha[h] * l_ref[:, cols[h]] + jnp.sum(ps[h], axis=0, keepdims=True)
            m_ref[:, cols[h]] = m_new[h]
        for h in heads:
            g = h // heads_per_group
            vg = vt_ref[0, c, g * HEAD_DIM:(g + 1) * HEAD_DIM, :]
            acc_ref[:, cols[h]] = alpha[h] * acc_ref[:, cols[h]] + _dot(vg, ps[h].astype(BF16))
        return eq_after

    lax.fori_loop(0, nch, attend_chunk, jnp.zeros((BLK, BLK), F32))
    a_t = acc_ref[...] / l_ref[...]
    for h in range(N_HEADS):
        a_ref[0, h] = a_t[:, h * BLK:(h + 1) * BLK].T.astype(BF16)


def _dsa_prompt(iq, iwt, qbd, ikb, kb, vt, tri, n, t, topk):
    nb = t // BLK
    ikb4 = ikb.reshape(n, nb, BLK, LANES)
    kb4 = kb.reshape(n, nb, BLK, ATT_KV)
    vt4 = vt.reshape(n, nb, ATT_KV, BLK)
    cols = N_HEADS * BLK
    return pl.pallas_call(
        functools.partial(_dsa_prompt_kernel, topk=topk),
        grid=(n, nb),
        in_specs=[
            pl.BlockSpec((1, N_IDX_HEADS, BLK, LANES), lambda b, i: (b * nb + i, 0, 0, 0)),
            pl.BlockSpec((N_IDX_HEADS, BLK), lambda b, i: (0, b * nb + i)),
            pl.BlockSpec((1, N_HEADS, BLK, ATT_KV), lambda b, i: (b * nb + i, 0, 0, 0)),
            pl.BlockSpec((1, nb, BLK, LANES), lambda b, i: (b, 0, 0, 0)),
            pl.BlockSpec((1, nb, BLK, ATT_KV), lambda b, i: (b, 0, 0, 0)),
            pl.BlockSpec((1, nb, ATT_KV, BLK), lambda b, i: (b, 0, 0, 0)),
            pl.BlockSpec(tri.shape, lambda b, i: (0, 0)),
        ],
        out_specs=pl.BlockSpec((1, N_HEADS, BLK, HEAD_DIM), lambda b, i: (b * nb + i, 0, 0, 0)),
        out_shape=jax.ShapeDtypeStruct((n * nb, N_HEADS, BLK, HEAD_DIM), BF16),
        scratch_shapes=[
            pltpu.VMEM((nb, BLK, BLK), I32),
            pltpu.VMEM((1, cols), F32),
            pltpu.VMEM((1, cols), F32),
            pltpu.VMEM((HEAD_DIM, cols), F32),
        ],
        compiler_params=_cparams(("parallel", "arbitrary")),
        name="dsa_prompt",
    )(iq, iwt, qbd, ikb4, kb4, vt4, tri)


def _sample_scores_kernel(pt_ref, iq_ref, wb_ref, *refs, pages_per_step):
    del pt_ref
    page_refs = refs[:pages_per_step]
    sc_ref = refs[pages_per_step]
    iq = iq_ref[0]
    wb = wb_ref[0]
    s_q = iq.shape[0] // N_IDX_HEADS
    for r in range(pages_per_step):
        s = _dot(iq, page_refs[r][0].astype(BF16))
        t = jnp.maximum(s, 0.0) * wb
        sc_ref[0, r] = jnp.sum(t.reshape(N_IDX_HEADS, s_q, LANES), axis=0)


def _sample_scores(page_table, iq_s, wb_s, cache_ikt, layer_off, pages_per_step):
    ns, npages = page_table.shape
    rows = iq_s.shape[1]
    s_q = rows // N_IDX_HEADS
    steps = npages // pages_per_step

    def page_spec(r):
        return pl.BlockSpec((1, IDX_DIM, PAGE_SIZE),
                            lambda b, j, pt: (layer_off + pt[b, j * pages_per_step + r], 0, 0))

    grid_spec = pltpu.PrefetchScalarGridSpec(
        num_scalar_prefetch=1,
        grid=(ns, steps),
        in_specs=[pl.BlockSpec((1, rows, IDX_DIM), lambda b, j, pt: (b, 0, 0)),
                  pl.BlockSpec((1, rows, LANES), lambda b, j, pt: (b, 0, 0))]
                 + [page_spec(r) for r in range(pages_per_step)],
        out_specs=pl.BlockSpec((1, pages_per_step, s_q, LANES), lambda b, j, pt: (b, j, 0, 0)),
    )
    return pl.pallas_call(
        functools.partial(_sample_scores_kernel, pages_per_step=pages_per_step),
        grid_spec=grid_spec,
        out_shape=jax.ShapeDtypeStruct((ns, npages, s_q, LANES), F32),
        compiler_params=_cparams(("parallel", "arbitrary")),
        name="sample_scores",
    )(page_table, iq_s, wb_s, *([cache_ikt] * pages_per_step))


def _sample_select_kernel(sc_ref, iq_ref, wb_ref, ikn_ref, tri_ref, bias_ref, sk_ref, *, topk, s_q):
    sb = sc_ref.shape[0]
    npages = sc_ref.shape[1]
    rows = sb * s_q

    def key_chunk(c, carry):
        sk_ref[c] = _sort_key(sc_ref[:, c].reshape(rows, LANES))
        return carry

    lax.fori_loop(0, npages, key_chunk, 0)
    row_id = lax.broadcasted_iota(I32, (s_q, LANES), 0)
    col_id = lax.broadcasted_iota(I32, (s_q, LANES), 1)
    for b in range(sb):
        s = _dot_nt(iq_ref[b], ikn_ref[b])
        t = jnp.maximum(s, 0.0) * wb_ref[b]
        score = jnp.sum(t.reshape(N_IDX_HEADS, s_q, LANES), axis=0)
        key = jnp.where(col_id <= row_id, _sort_key(score), jnp.int32(INT_MIN))
        sk_ref[npages, b * s_q:(b + 1) * s_q, :] = key
    nch = npages + 1
    thr_b, need_b = _kth_largest(sk_ref, nch, topk, keys_on_lanes=True, unroll=4)
    tri = tri_ref[...]

    def bias_chunk(c, eq_before):
        bias, eq_after = _select_bias(sk_ref[c], thr_b, need_b, eq_before, tri, keys_on_lanes=True)
        bias_ref[:, c] = bias.reshape(sb, s_q, LANES)
        return eq_after

    lax.fori_loop(0, nch, bias_chunk, jnp.zeros((rows, LANES), F32))


def _sample_select(scores, iq_s, wb_s, ikn, tri, topk, sb):
    ns, npages, s_q, _ = scores.shape
    rows = iq_s.shape[1]
    return pl.pallas_call(
        functools.partial(_sample_select_kernel, topk=topk, s_q=s_q),
        grid=(ns // sb,),
        in_specs=[
            pl.BlockSpec((sb, npages, s_q, LANES), lambda g: (g, 0, 0, 0)),
            pl.BlockSpec((sb, rows, IDX_DIM), lambda g: (g, 0, 0)),
            pl.BlockSpec((sb, rows, LANES), lambda g: (g, 0, 0)),
            pl.BlockSpec((sb, PAGE_SIZE, IDX_DIM), lambda g: (g, 0, 0)),
            pl.BlockSpec(tri.shape, lambda g: (0, 0)),
        ],
        out_specs=pl.BlockSpec((sb, npages + 1, s_q, LANES), lambda g: (g, 0, 0, 0)),
        out_shape=jax.ShapeDtypeStruct((ns, npages + 1, s_q, LANES), F32),
        scratch_shapes=[pltpu.VMEM((npages + 1, sb * s_q, LANES), I32)],
        compiler_params=_cparams(("parallel",)),
        name="sample_select",
    )(scores, iq_s, wb_s, ikn, tri)


def _sample_attend_kernel(pt_ref, q_ref, bias_ref, bias_new_ref, kn_ref, vn_ref, *refs, pages_per_step, s_q):
    del pt_ref
    k_refs = refs[:pages_per_step]
    v_refs = refs[pages_per_step:2 * pages_per_step]
    a_ref, m_ref, l_ref, acc_ref = refs[2 * pages_per_step:]
    j = pl.program_id(1)
    rows = N_HEADS * s_q

    @pl.when(j == 0)
    def _():
        m_ref[...] = jnp.full(m_ref.shape, MASKED, F32)
        l_ref[...] = jnp.zeros(l_ref.shape, F32)
        acc_ref[...] = jnp.zeros(acc_ref.shape, F32)

    q = q_ref[0]

    def masked(lg, bias):
        return (lg.reshape(N_HEADS, s_q, LANES) + bias[None]).reshape(rows, LANES)

    def update(lgs, weighted_values):
        top = functools.reduce(jnp.maximum, lgs)
        m_prev = m_ref[...]
        m_new = jnp.maximum(m_prev, jnp.max(top, axis=1, keepdims=True))
        alpha = jnp.exp2(m_prev - m_new)
        ps = [jnp.exp2(lg - m_new) for lg in lgs]
        l_ref[...] = alpha * l_ref[...] + jnp.sum(functools.reduce(jnp.add, ps), axis=1, keepdims=True)
        m_ref[...] = m_new
        acc_ref[...] = alpha * acc_ref[...] + weighted_values([p.astype(BF16) for p in ps])

    lgs = [masked(_dot(q, k_refs[r][0].astype(BF16)), bias_ref[0, r]) for r in range(pages_per_step)]
    update(lgs, lambda ps: functools.reduce(
        jnp.add, [_dot_nt(p, v_refs[r][0].astype(BF16)) for r, p in enumerate(ps)]))

    @pl.when(j == pl.num_programs(1) - 1)
    def _():
        lg = masked(_dot_nt(q, kn_ref[0].astype(BF16)), bias_new_ref[0, 0])
        update([lg], lambda ps: _dot(ps[0], vn_ref[0].astype(BF16)))
        a_ref[0] = (acc_ref[...] / l_ref[...]).astype(BF16)


def _sample_attend(page_table, q_s, bias, kn, vn, cache_kt, cache_vt, layer_off, pages_per_step):
    ns, npages = page_table.shape
    rows = q_s.shape[1]
    s_q = rows // N_HEADS
    steps = npages // pages_per_step

    def page_spec(r):
        return pl.BlockSpec((1, ATT_KV, PAGE_SIZE),
                            lambda b, j, pt: (layer_off + pt[b, j * pages_per_step + r], 0, 0))

    new_spec = pl.BlockSpec((1, PAGE_SIZE, ATT_KV), lambda b, j, pt: (b, 0, 0))
    grid_spec = pltpu.PrefetchScalarGridSpec(
        num_scalar_prefetch=1,
        grid=(ns, steps),
        in_specs=[pl.BlockSpec((1, rows, ATT_KV), lambda b, j, pt: (b, 0, 0)),
                  pl.BlockSpec((1, pages_per_step, s_q, LANES), lambda b, j, pt: (b, j, 0, 0)),
                  pl.BlockSpec((1, 1, s_q, LANES), lambda b, j, pt: (b, npages, 0, 0)),
                  new_spec, new_spec]
                 + [page_spec(r) for r in range(pages_per_step)] * 2,
        out_specs=pl.BlockSpec((1, rows, ATT_KV), lambda b, j, pt: (b, 0, 0)),
        scratch_shapes=[
            pltpu.VMEM((rows, 1), F32),
            pltpu.VMEM((rows, 1), F32),
            pltpu.VMEM((rows, ATT_KV), F32),
        ],
    )
    return pl.pallas_call(
        functools.partial(_sample_attend_kernel, pages_per_step=pages_per_step, s_q=s_q),
        grid_spec=grid_spec,
        out_shape=jax.ShapeDtypeStruct((ns, rows, ATT_KV), BF16),
        compiler_params=_cparams(("parallel", "arbitrary")),
        name="sample_attend",
    )(page_table, q_s, bias, bias, kn, vn,
      *([cache_kt] * pages_per_step), *([cache_vt] * pages_per_step))


def _merge_kernel(x_ref, a_ref, d_ref, u_ref, gvn_ref, wc_ref, gb_ref, wg_ref, wba_ref, pbd_ref, psc_ref,
                  wbp_ref, wbg_ref, wo_ref, lng_ref, lnb_ref, rwt_ref, rb_ref,
                  x1_ref, gate_ref, *, alpha):
    tm = x_ref.shape[0]
    x = x_ref[...]
    xb = x.astype(BF16)

    def gate(idx):
        return _sigmoid(_dot(xb, wg_ref[:, idx * D_MODEL:(idx + 1) * D_MODEL]))

    parts = []
    for b in range(tm // BLK):
        acc = jnp.zeros((BLK, D_MODEL), F32)
        for h in range(N_HEADS):
            acc = acc + _dot(a_ref[b, h], wba_ref[h])
        parts.append(acc)
    m = gate(0) * (parts[0] if len(parts) == 1 else jnp.concatenate(parts, axis=0))
    y = _dot(d_ref[...].astype(BF16), pbd_ref[...]) * psc_ref[...]
    m = m + gate(1) * _dot(y.astype(BF16), wbp_ref[...])
    gv = gvn_ref[...].astype(BF16)
    grp = lax.broadcasted_iota(I32, (tm, GMLP_WIDTH), 1) // GMLP_GC
    mix = gb_ref[...]
    for g in range(GMLP_GROUPS):
        mix = mix + jnp.where(grp == g, _dot(wc_ref[g], gv), 0.0)
    c = u_ref[...] * mix
    m = m + gate(2) * _dot(c.astype(BF16), wbg_ref[...])
    y = _dot(m.astype(BF16), wo_ref[...])
    x1 = _layer_norm(alpha * x + y, lng_ref[...], lnb_ref[...])
    x1_ref[...] = x1

    scores = _sigmoid(_dot_nt(rwt_ref[...], x1.astype(BF16)))
    sel = scores + rb_ref[...]
    per = N_EXPERTS // N_EXPERT_GROUPS
    g3 = sel.reshape(N_EXPERT_GROUPS, per, tm)
    sub = lax.broadcasted_iota(I32, (N_EXPERT_GROUPS, per, tm), 1)
    m1 = jnp.max(g3, axis=1, keepdims=True)
    first = jnp.min(jnp.where(g3 == m1, sub, per), axis=1, keepdims=True)
    m2 = jnp.max(jnp.where(sub == first, -jnp.inf, g3), axis=1, keepdims=True)
    gs = (m1 + m2).reshape(N_EXPERT_GROUPS, tm)

    def rank_of(vals, count):
        idx = lax.broadcasted_iota(I32, vals.shape, 0)
        rank = jnp.zeros(vals.shape, I32)
        for o in range(count):
            other = vals[o:o + 1]
            beats = (other > vals) | ((other == vals) & (o < idx))
            rank = rank + jnp.where(beats, 1, 0)
        return rank

    gkeep = rank_of(gs, N_EXPERT_GROUPS) < TOPK_GROUPS
    ekeep = jnp.broadcast_to(gkeep.reshape(N_EXPERT_GROUPS, 1, tm), (N_EXPERT_GROUPS, per, tm)).reshape(N_EXPERTS, tm)
    sel = jnp.where(ekeep, sel, -jnp.inf)
    chosen = rank_of(sel, N_EXPERTS) < TOPK_EXPERTS
    wsel = jnp.where(chosen, scores, 0.0)
    gate_t = wsel / jnp.sum(wsel, axis=0, keepdims=True) * ROUTED_SCALE
    gate_ref[...] = gate_t.T


def _merge(x, a_bd, d, u, gvn, wl, tm, alpha):
    m = x.shape[0]
    nb = tm // BLK
    row = lambda w: pl.BlockSpec((tm, w), lambda i: (i, 0))
    full = lambda a: pl.BlockSpec(a.shape, lambda i: (0,) * a.ndim)
    weights = [wl["wc"], wl["gb"], wl["wg"], wl["wba"], wl["pbd"], wl["psc"], wl["wbp"], wl["wbg"], wl["wo"],
               wl["ln1_g"], wl["ln1_b"], wl["rwt"], wl["rb"]]
    return pl.pallas_call(
        functools.partial(_merge_kernel, alpha=alpha),
        grid=(m // tm,),
        in_specs=[row(D_MODEL),
                  pl.BlockSpec((nb, N_HEADS, BLK, HEAD_DIM), lambda i: (i, 0, 0, 0)),
                  row(POOL_WIDTH), row(GMLP_WIDTH), row(GMLP_WIDTH)] + [full(w) for w in weights],
        out_specs=(row(D_MODEL), row(N_EXPERTS)),
        out_shape=(jax.ShapeDtypeStruct((m, D_MODEL), F32), jax.ShapeDtypeStruct((m, N_EXPERTS), F32)),
        compiler_params=_cparams(("parallel",)),
        name="merge",
    )(x, a_bd, d, u, gvn, *weights)


def _swiglu_act(xb, w_gu):
    h = _dot(xb, w_gu)
    g = h[:, :EXPERT_FF]
    return g * _sigmoid(g) * h[:, EXPERT_FF:]


def _moe_kernel(x_ref, gate_ref, wgu_ref, wd_ref, sgu_ref, sd_ref, lng_ref, lnb_ref, o_ref, xb_ref, acc_ref,
                *, alpha):
    step = pl.program_id(1)

    @pl.when(step == 0)
    def _():
        xb_ref[...] = x_ref[...].astype(BF16)
        acc_ref[...] = _dot(_swiglu_act(xb_ref[...], sgu_ref[...]).astype(BF16), sd_ref[...])

    xb = xb_ref[...]
    gate = gate_ref[...]
    lane = lax.broadcasted_iota(I32, gate.shape, 1)
    acts = []
    for j in range(MOE_EXPERTS_PER_STEP):
        e = step * MOE_EXPERTS_PER_STEP + j
        gcol = jnp.sum(jnp.where(lane == e, gate, 0.0), axis=1, keepdims=True)
        acts.append((gcol * _swiglu_act(xb, wgu_ref[j].astype(BF16))).astype(BF16))
    w_down = wd_ref[...].astype(BF16).reshape(MOE_EXPERTS_PER_STEP * EXPERT_FF, D_MODEL)
    acc_ref[...] += _dot(jnp.concatenate(acts, axis=1), w_down)

    @pl.when(step == pl.num_programs(1) - 1)
    def _():
        o_ref[...] = _layer_norm(alpha * x_ref[...] + acc_ref[...], lng_ref[...], lnb_ref[...])


def _moe(x1, gate, w_gu, w_down, sh_gu, sh_down, ln_g, ln_b, tm, alpha):
    m = x1.shape[0]
    full = lambda a: pl.BlockSpec(a.shape, lambda i, e: (0,) * a.ndim)
    per = MOE_EXPERTS_PER_STEP
    return pl.pallas_call(
        functools.partial(_moe_kernel, alpha=alpha),
        grid=(m // tm, N_EXPERTS // per),
        in_specs=[
            pl.BlockSpec((tm, D_MODEL), lambda i, e: (i, 0)),
            pl.BlockSpec((tm, N_EXPERTS), lambda i, e: (i, 0)),
            pl.BlockSpec((per, D_MODEL, 2 * EXPERT_FF), lambda i, e: (e, 0, 0)),
            pl.BlockSpec((per, EXPERT_FF, D_MODEL), lambda i, e: (e, 0, 0)),
            full(sh_gu), full(sh_down), full(ln_g), full(ln_b),
        ],
        out_specs=pl.BlockSpec((tm, D_MODEL), lambda i, e: (i, 0)),
        out_shape=jax.ShapeDtypeStruct((m, D_MODEL), F32),
        scratch_shapes=[pltpu.VMEM((tm, D_MODEL), BF16), pltpu.VMEM((tm, D_MODEL), F32)],
        compiler_params=_cparams(("parallel", "arbitrary")),
        name="moe",
    )(x1, gate, w_gu, w_down, sh_gu, sh_down, ln_g, ln_b)


def _rope_tables(pos):
    half = HEAD_DIM // 2
    inv = ROPE_THETA ** (-jnp.arange(half, dtype=F32) / half)
    ang = pos.astype(F32)[:, None] * inv[None, :]
    cos, sin = jnp.cos(ang), jnp.sin(ang)
    cos_t = jnp.tile(cos, (1, LANES // half))
    sin_t = jnp.tile(jnp.concatenate([-sin, sin], axis=1), (1, LANES // HEAD_DIM))
    return cos_t, sin_t


def _block_diag(blocks):
    g, r, c = blocks.shape
    eye = jnp.eye(g, dtype=blocks.dtype)
    return jnp.einsum("grc,gh->grhc", blocks, eye).reshape(g * r, g * c)


def _layer_weights(l, w_in, w_ba, w_bp, w_bg, w_out, pool_w, pool_scale, g_ln_g, g_ln_b, g_ws, g_b,
                   ln1_g, ln1_b, router_w, router_bias, sh_gu, sh_down, ln2_g, ln2_b):
    w = w_in[l]
    sizes = (ATT_Q, ATT_KV, ATT_KV, N_IDX_HEADS * IDX_DIM, IDX_DIM, N_IDX_HEADS,
             POOL_WIDTH, GMLP_WIDTH, GMLP_WIDTH, N_BRANCH * D_MODEL)
    offs = np.concatenate([[0], np.cumsum(sizes)]).tolist()
    wq, wk, wv, wiq, wik, wiw, wp, wu, wgv, wg = [w[:, offs[j]:offs[j + 1]] for j in range(len(sizes))]
    group_of_head = jnp.asarray(np.eye(N_KV_HEADS, dtype=np.float32)[np.arange(N_HEADS) // (N_HEADS // N_KV_HEADS)])
    wq_bd = jnp.einsum("dhc,hg->dhgc", wq.reshape(D_MODEL, N_HEADS, HEAD_DIM) * (HEAD_DIM ** -0.5 * LOG2E),
                       group_of_head).reshape(D_MODEL, QBD_W)
    wiq_pad = jnp.pad(wiq.reshape(D_MODEL, N_IDX_HEADS, IDX_DIM),
                      ((0, 0), (0, 0), (0, LANES - IDX_DIM))).reshape(D_MODEL, IQ_W)
    wikw = jnp.pad(jnp.concatenate([wik, wiw], axis=1), ((0, 0), (0, LANES - IDX_DIM - N_IDX_HEADS)))
    w_cat = jnp.concatenate([wq_bd, wk, wv, wiq_pad, wikw, wp, wu, wgv], axis=1).astype(BF16)
    wba = w_ba[l].reshape(N_HEADS, HEAD_DIM, D_MODEL).astype(BF16)
    return dict(
        w_cat=w_cat, wvt=wv.T.astype(BF16), wiwt=wiw.T.astype(BF16), wg=wg.astype(BF16), wba=wba,
        pbd=_block_diag(pool_w[l]).astype(BF16), psc=pool_scale[l].reshape(1, POOL_WIDTH),
        wbp=w_bp[l].astype(BF16), wbg=w_bg[l].astype(BF16), wo=w_out[l].astype(BF16),
        g_ln_g=g_ln_g[l].reshape(1, GMLP_WIDTH), g_ln_b=g_ln_b[l].reshape(1, GMLP_WIDTH),
        ws=g_ws[l], gbias=g_b[l],
        ln1_g=ln1_g[l].reshape(1, D_MODEL), ln1_b=ln1_b[l].reshape(1, D_MODEL),
        rwt=router_w[l].T.astype(BF16), rbias=router_bias[l],
        sh_gu=sh_gu[l].astype(BF16), sh_down=sh_down[l].astype(BF16),
        ln2_g=ln2_g[l].reshape(1, D_MODEL), ln2_b=ln2_b[l].reshape(1, D_MODEL),
    )


def _chunk_mix(wl, cl, tm):
    tril = jnp.tril(jnp.ones((cl, cl), F32))
    wm = wl["ws"][:, :cl, :cl] * tril
    reps = tm // cl
    wc = jnp.stack([_block_diag(jnp.broadcast_to(wm[g], (reps, cl, cl))) for g in range(GMLP_GROUPS)])
    gb = jnp.tile(jnp.repeat(wl["gbias"][:, :cl].T, GMLP_GC, axis=1), (reps, 1))
    return wc.astype(BF16), gb


def _to_rows_by_seq(a, ns, s_q):
    nblk, heads, _, width = a.shape
    per = BLK // s_q
    a = a.reshape(nblk, heads, per, s_q, width).transpose(0, 2, 1, 3, 4)
    return a.reshape(ns, heads * s_q, width)


def _from_rows_by_seq(a, s_q):
    ns, rows, width = a.shape
    heads = rows // s_q
    per = BLK // s_q
    a = a.reshape(ns // per, per, heads, s_q, width).transpose(0, 2, 1, 3, 4)
    return a.reshape(ns // per, heads, BLK, width)


def kernel(x_prompt, x_sample, cache_k, cache_v, cache_idx_k, state_pool, page_table, w_in, w_branch_attn,
           w_branch_pool, w_branch_gmlp, w_out, pool_w, pool_scale, gmlp_ln_g, gmlp_ln_b, gmlp_ws, gmlp_b,
           ln1_g, ln1_b, router_w, router_bias, expert_w_gu, expert_w_down, shared_w_gu, shared_w_down,
           ln2_g, ln2_b):
    n_p, t_p, _ = x_prompt.shape
    n_s, t_s, _ = x_sample.shape
    depth = w_in.shape[0]
    n_pool = cache_k.shape[1]
    npages = page_table.shape[1]
    past = npages * PAGE_SIZE
    m_p, m_s = n_p * t_p, n_s * t_s
    alpha = (2 * depth) ** 0.25
    assert t_p % BLK == 0 and m_s % BLK == 0 and BLK % t_s == 0 and t_s <= 16

    tm_p = BLK
    tm_s = BLK
    tm_moe = 1024 if m_p % 1024 == 0 else tm_p
    tp_pool = 512 if t_p % 512 == 0 else BLK
    pages_per_step = 16 if npages % 16 == 0 else 1
    sel_batch = 8 if n_s % 8 == 0 else 1
    topk_p = min(TOPK_MAX, t_p // 4)
    topk_s = min(TOPK_MAX, (past + t_s) // 4)

    cos_p, sin_p = _rope_tables(jnp.arange(t_p, dtype=I32))
    cos_s, sin_s = _rope_tables(past + jnp.arange(t_s, dtype=I32))
    cos_s, sin_s = jnp.tile(cos_s, (m_s // t_s, 1)), jnp.tile(sin_s, (m_s // t_s, 1))
    tri_p = _tri_matrix(BLK, keys_on_lanes=False)
    tri_s = _tri_matrix(LANES, keys_on_lanes=True)
    cache_kt = cache_k.transpose(0, 1, 3, 4, 2).reshape(depth * n_pool, ATT_KV, PAGE_SIZE)
    cache_vt = cache_v.transpose(0, 1, 3, 4, 2).reshape(depth * n_pool, ATT_KV, PAGE_SIZE)
    cache_ikt = cache_idx_k.transpose(0, 1, 3, 2).reshape(depth * n_pool, IDX_DIM, PAGE_SIZE)

    hp = x_prompt.reshape(m_p, D_MODEL)
    hs = x_sample.reshape(m_s, D_MODEL)
    outs = {name: [] for name in ("kp", "vp", "ikp", "pp", "ks", "vs", "iks", "ps", "gs")}
    for l in range(depth):
        wl = _layer_weights(l, w_in, w_branch_attn, w_branch_pool, w_branch_gmlp, w_out, pool_w, pool_scale,
                            gmlp_ln_g, gmlp_ln_b, gmlp_ws, gmlp_b, ln1_g, ln1_b, router_w, router_bias,
                            shared_w_gu, shared_w_down, ln2_g, ln2_b)

        def finish(x, a_bd, d, u, gvn, cl, tm, tm_e):
            wc, gb = _chunk_mix(wl, cl, tm)
            wm = dict(wl, wc=wc, gb=gb, rb=jnp.broadcast_to(wl["rbias"][:, None], (N_EXPERTS, tm)))
            x1, gate = _merge(x, a_bd, d, u, gvn, wm, tm, alpha)
            return _moe(x1, gate, expert_w_gu[l], expert_w_down[l], wl["sh_gu"], wl["sh_down"],
                        wl["ln2_g"], wl["ln2_b"], tm_e, alpha)

        qbd, k, v, kb, vt, iq, ikw, ikb, iwt, p, u, gvn = _proj(hp, wl, cos_p, sin_p, tm_p)
        p3 = p.reshape(n_p, t_p, POOL_WIDTH)
        d = _pool(p3, None, 0, tp_pool).reshape(m_p, POOL_WIDTH)
        a_bd = _dsa_prompt(iq, iwt, qbd, ikb, kb, vt, tri_p, n_p, t_p, topk_p)
        hp = finish(hp, a_bd, d, u, gvn, CHUNK, tm_p, tm_moe)
        outs["kp"].append(k.reshape(n_p, t_p, N_KV_HEADS, HEAD_DIM))
        outs["vp"].append(v.reshape(n_p, t_p, N_KV_HEADS, HEAD_DIM))
        outs["ikp"].append(ikw[:, :IDX_DIM].reshape(n_p, t_p, IDX_DIM))
        outs["pp"].append(p3[:, t_p - POOL_STATE:])

        qbd, k, v, kb, vt, iq, ikw, ikb, iwt, p, u, gvn = _proj(hs, wl, cos_s, sin_s, tm_s)
        p3 = p.reshape(n_s, t_s, POOL_WIDTH)
        prefix16 = jnp.pad(state_pool[l], ((0, 0), (16 - POOL_STATE, 0), (0, 0)))
        d = _pool(p3, prefix16, past, t_s).reshape(m_s, POOL_WIDTH)
        iq_s = _to_rows_by_seq(iq, n_s, t_s)[:, :, :IDX_DIM]
        q_s = _to_rows_by_seq(qbd, n_s, t_s)
        wb_s = jnp.broadcast_to(iwt.reshape(N_IDX_HEADS, n_s, t_s).transpose(1, 0, 2).reshape(n_s, -1, 1),
                                (n_s, N_IDX_HEADS * t_s, LANES))
        pad_rows = lambda a: jnp.pad(a.reshape(n_s, t_s, -1), ((0, 0), (0, PAGE_SIZE - t_s), (0, 0)))
        ikn = pad_rows(ikb[:, :IDX_DIM])
        scores = _sample_scores(page_table, iq_s, wb_s, cache_ikt, l * n_pool, pages_per_step)
        bias = _sample_select(scores, iq_s, wb_s, ikn, tri_s, topk_s, sel_batch)
        a_s = _sample_attend(page_table, q_s, bias, pad_rows(k), pad_rows(v), cache_kt, cache_vt,
                             l * n_pool, pages_per_step)
        a_s = a_s.reshape(n_s, N_HEADS, t_s, N_KV_HEADS, HEAD_DIM)
        a_s = jnp.stack([a_s[:, h, :, h // (N_HEADS // N_KV_HEADS)] for h in range(N_HEADS)], axis=1)
        a_bd = _from_rows_by_seq(a_s.reshape(n_s, N_HEADS * t_s, HEAD_DIM), t_s)
        hs = finish(hs, a_bd, d, u, gvn, t_s, tm_s, tm_s)
        outs["ks"].append(k.reshape(n_s, t_s, N_KV_HEADS, HEAD_DIM))
        outs["vs"].append(v.reshape(n_s, t_s, N_KV_HEADS, HEAD_DIM))
        outs["iks"].append(ikw[:, :IDX_DIM].reshape(n_s, t_s, IDX_DIM))
        outs["ps"].append(jnp.concatenate([state_pool[l], p3], axis=1)[:, -POOL_STATE:])
        outs["gs"].append(gvn.reshape(n_s, t_s, GMLP_WIDTH))

    st = lambda name: jnp.stack(outs[name])
    return (hp.reshape(n_p, t_p, D_MODEL), hs.reshape(n_s, t_s, D_MODEL),
            st("kp"), st("vp"), st("ikp"), st("ks"), st("vs"), st("iks"), st("pp"), st("ps"), st("gs"))
```

```python
import functools

import jax
import jax.numpy as jnp
import numpy as np
from jax import lax
from jax.experimental import pallas as pl
from jax.experimental.pallas import tpu as pltpu

F32 = jnp.float32
BF16 = jnp.bfloat16
I32 = jnp.int32

D_MODEL = 1024
N_HEADS = 8
N_KV_HEADS = 4
HEAD_DIM = 64
N_IDX_HEADS = 8
IDX_DIM = 64
TOPK_MAX = 256
PAGE_SIZE = 128
ROPE_THETA = 10000.0
POOL_WINDOWS = (2, 4, 8, 16)
POOL_WIDTH = 256
POOL_GC = 64
POOL_STATE = 15
GMLP_WIDTH = 256
GMLP_GROUPS = 4
GMLP_GC = 64
CHUNK = 128
N_BRANCH = 3
ATT_Q = N_HEADS * HEAD_DIM
ATT_KV = N_KV_HEADS * HEAD_DIM
N_EXPERTS = 64
TOPK_EXPERTS = 8
N_EXPERT_GROUPS = 8
TOPK_GROUPS = 4
EXPERT_FF = 256
ROUTED_SCALE = 2.5
LN_EPS = 1e-5

LANES = 128
SUBLANES = 8
BLK = 256
MOE_EXPERTS_PER_STEP = 2
INT_MIN = -2 ** 31
MASKED = -1e30
LOG2E = 1.4426950408889634
VMEM_LIMIT = 56 * 1024 * 1024

C_Q = 0
C_K = C_Q + ATT_Q
C_V = C_K + ATT_KV
C_IQ = C_V + ATT_KV
C_IKW = C_IQ + N_IDX_HEADS * IDX_DIM
C_P = C_IKW + LANES
C_U = C_P + POOL_WIDTH
C_GV = C_U + GMLP_WIDTH
C_END = C_GV + GMLP_WIDTH


def _cparams(sem):
    return pltpu.CompilerParams(dimension_semantics=sem, vmem_limit_bytes=VMEM_LIMIT)


def _layer_norm(x, g, b):
    mu = jnp.mean(x, axis=-1, keepdims=True)
    xc = x - mu
    var = jnp.mean(xc * xc, axis=-1, keepdims=True)
    return xc * lax.rsqrt(var + LN_EPS) * g + b


def _sigmoid(x):
    return 1.0 / (1.0 + jnp.exp(-x))


def _dot(a, b):
    return jnp.dot(a, b, preferred_element_type=F32)


def _dot_nt(a, b):
    return lax.dot_general(a, b, (((1,), (1,)), ((), ())), preferred_element_type=F32)


def _sort_key(score):
    score = jnp.where(score == 0.0, 0.0, score)
    bits = lax.bitcast_convert_type(score, I32)
    return bits ^ ((bits >> 31) & jnp.int32(0x7FFFFFFF))


def _proj_kernel(x_ref, w_ref, wvt_ref, wiwt_ref, cos_ref, sin_ref, lng_ref, lnb_ref,
                 qbd_ref, k_ref, v_ref, kb_ref, vt_ref, iq_ref, ikw_ref, ikb_ref, iwt_ref,
                 p_ref, u_ref, gvn_ref):
    tm = x_ref.shape[0]
    xb = x_ref[...].astype(BF16)
    cos = cos_ref[...]
    sin = sin_ref[...]
    lane = lax.broadcasted_iota(I32, (tm, LANES), 1)
    first_half = (lane % HEAD_DIM) < (HEAD_DIM // 2)

    def mm(c0, width):
        return _dot(xb, w_ref[:, c0:c0 + width])

    def rope(z):
        partner = jnp.where(first_half, pltpu.roll(z, LANES - 32, 1), pltpu.roll(z, 32, 1))
        return z * cos + partner * sin

    def rope_wide(z):
        return jnp.concatenate([rope(z[:, s * LANES:(s + 1) * LANES]) for s in range(z.shape[1] // LANES)], axis=1)

    nblk = tm // BLK
    low_half = lane < HEAD_DIM
    zero_slab = jnp.zeros((tm, LANES), F32)

    def head_slabs(z):
        for s in range(z.shape[1] // LANES):
            slab = z[:, s * LANES:(s + 1) * LANES]
            swapped = pltpu.roll(slab, HEAD_DIM, 1)
            for r in range(2):
                yield 2 * s + r, (slab, swapped) if r == 0 else (swapped, slab)

    for h, (head_low, head_high) in head_slabs(rope_wide(mm(C_Q, ATT_Q))):
        g = h // (N_HEADS // N_KV_HEADS)
        own = jnp.where(low_half, head_low, 0.0) if g % 2 == 0 else jnp.where(low_half, 0.0, head_high)
        z = jnp.concatenate([own, zero_slab] if g // 2 == 0 else [zero_slab, own], axis=1).astype(BF16)
        for b in range(nblk):
            qbd_ref[b, h] = z[b * BLK:(b + 1) * BLK]
    k = rope_wide(mm(C_K, ATT_KV))
    k_ref[...] = k
    kb_ref[...] = k.astype(BF16)
    v_ref[...] = mm(C_V, ATT_KV)
    vt = _dot_nt(wvt_ref[...], xb).astype(BF16)
    for b in range(nblk):
        vt_ref[b] = vt[:, b * BLK:(b + 1) * BLK]
    for h, (head_low, _) in head_slabs(rope_wide(mm(C_IQ, N_IDX_HEADS * IDX_DIM))):
        z = jnp.where(low_half, head_low, 0.0).astype(BF16)
        for b in range(nblk):
            iq_ref[b, h] = z[b * BLK:(b + 1) * BLK]
    z = mm(C_IKW, LANES)
    is_key = lane < IDX_DIM
    ikw = jnp.where(is_key, rope(z), z)
    ikw_ref[...] = ikw
    ikb_ref[...] = jnp.where(is_key, ikw, 0.0).astype(BF16)
    iwt_ref[...] = _dot_nt(wiwt_ref[...], xb)
    p_ref[...] = mm(C_P, POOL_WIDTH)
    u_ref[...] = mm(C_U, GMLP_WIDTH)
    gvn_ref[...] = _layer_norm(mm(C_GV, GMLP_WIDTH), lng_ref[...], lnb_ref[...])


def _proj(x, wl, cos_t, sin_t, tm):
    m = x.shape[0]
    nt = cos_t.shape[0] // tm
    row = lambda w: pl.BlockSpec((tm, w), lambda i: (i, 0))
    full = lambda a: pl.BlockSpec(a.shape, lambda i: (0,) * a.ndim)
    tab = pl.BlockSpec((tm, LANES), lambda i: (i % nt, 0))
    nb = tm // BLK
    out_shape = (
        jax.ShapeDtypeStruct((m // BLK, N_HEADS, BLK, ATT_KV), BF16),
        jax.ShapeDtypeStruct((m, ATT_KV), F32),
        jax.ShapeDtypeStruct((m, ATT_KV), F32),
        jax.ShapeDtypeStruct((m, ATT_KV), BF16),
        jax.ShapeDtypeStruct((m // BLK, ATT_KV, BLK), BF16),
        jax.ShapeDtypeStruct((m // BLK, N_IDX_HEADS, BLK, LANES), BF16),
        jax.ShapeDtypeStruct((m, LANES), F32),
        jax.ShapeDtypeStruct((m, LANES), BF16),
        jax.ShapeDtypeStruct((N_IDX_HEADS, m), F32),
        jax.ShapeDtypeStruct((m, POOL_WIDTH), F32),
        jax.ShapeDtypeStruct((m, GMLP_WIDTH), F32),
        jax.ShapeDtypeStruct((m, GMLP_WIDTH), F32),
    )
    out_specs = (
        pl.BlockSpec((nb, N_HEADS, BLK, ATT_KV), lambda i: (i, 0, 0, 0)),
        row(ATT_KV), row(ATT_KV), row(ATT_KV),
        pl.BlockSpec((nb, ATT_KV, BLK), lambda i: (i, 0, 0)),
        pl.BlockSpec((nb, N_IDX_HEADS, BLK, LANES), lambda i: (i, 0, 0, 0)),
        row(LANES), row(LANES),
        pl.BlockSpec((N_IDX_HEADS, tm), lambda i: (0, i)),
        row(POOL_WIDTH), row(GMLP_WIDTH), row(GMLP_WIDTH),
    )
    weights = [wl["w_cat"], wl["wvt"], wl["wiwt"]]
    return pl.pallas_call(
        _proj_kernel,
        grid=(m // tm,),
        in_specs=[row(D_MODEL)] + [full(w) for w in weights] + [tab, tab, full(wl["g_ln_g"]), full(wl["g_ln_b"])],
        out_specs=out_specs,
        out_shape=out_shape,
        compiler_params=_cparams(("parallel",)),
        name="proj",
    )(x, *weights, cos_t, sin_t, wl["g_ln_g"], wl["g_ln_b"])


def _pool_kernel(p_ref, halo_ref, d_ref, ext_ref, s2_ref, s4_ref, s8_ref, *, pos_base, halo_is_prefix):
    tp = p_ref.shape[1]
    r_end = tp + 32
    i = pl.program_id(1)
    p = p_ref[0]
    halo = halo_ref[0]
    if not halo_is_prefix:
        halo = jnp.where(i == 0, 0.0, halo)
    ext_ref[0:16, :] = jnp.zeros((16, POOL_WIDTH), F32)
    ext_ref[16:32, :] = halo
    ext_ref[32:r_end, :] = p
    s2_ref[8:r_end, :] = ext_ref[8:r_end, :] + ext_ref[7:r_end - 1, :]
    s4_ref[16:r_end, :] = s2_ref[16:r_end, :] + s2_ref[14:r_end - 2, :]
    s8_ref[24:r_end, :] = s4_ref[24:r_end, :] + s4_ref[20:r_end - 4, :]
    s16 = s8_ref[32:r_end, :] + s8_ref[24:r_end - 8, :]
    lane = lax.broadcasted_iota(I32, (tp, POOL_WIDTH), 1)
    grp = lane // POOL_GC
    win = jnp.where(grp == 0, s2_ref[32:r_end, :],
                    jnp.where(grp == 1, s4_ref[32:r_end, :],
                              jnp.where(grp == 2, s8_ref[32:r_end, :], s16)))
    width = jnp.where(grp == 0, POOL_WINDOWS[0],
                      jnp.where(grp == 1, POOL_WINDOWS[1],
                                jnp.where(grp == 2, POOL_WINDOWS[2], POOL_WINDOWS[3])))
    pos = pos_base + i * tp + lax.broadcasted_iota(I32, (tp, POOL_WIDTH), 0)
    cnt = jnp.minimum(width, pos + 1).astype(F32)
    d_ref[0] = win / cnt - p


def _pool(p3, prefix16, pos_base, tp):
    n, t, _ = p3.shape
    halo_is_prefix = prefix16 is not None
    if halo_is_prefix:
        halo = prefix16
        halo_spec = pl.BlockSpec((1, 16, POOL_WIDTH), lambda b, i: (b, 0, 0))
    else:
        halo = p3
        step = tp // 16
        halo_spec = pl.BlockSpec((1, 16, POOL_WIDTH), lambda b, i: (b, jnp.maximum(i * step - 1, 0), 0))
    rows = tp + 32
    return pl.pallas_call(
        functools.partial(_pool_kernel, pos_base=pos_base, halo_is_prefix=halo_is_prefix),
        grid=(n, t // tp),
        in_specs=[pl.BlockSpec((1, tp, POOL_WIDTH), lambda b, i: (b, i, 0)), halo_spec],
        out_specs=pl.BlockSpec((1, tp, POOL_WIDTH), lambda b, i: (b, i, 0)),
        out_shape=jax.ShapeDtypeStruct((n, t, POOL_WIDTH), F32),
        scratch_shapes=[pltpu.VMEM((rows, POOL_WIDTH), F32) for _ in range(4)],
        compiler_params=_cparams(("parallel", "parallel")),
        name="pool",
    )(p3, halo)


def _kth_largest(sk_ref, nch, topk, keys_on_lanes, unroll=1):
    rows, cols = sk_ref.shape[1:]
    if keys_on_lanes:
        vec, acc_shape = (rows, 1), (rows, cols)
        fold = lambda x: x
        total = lambda cnt: jnp.sum(cnt.astype(F32), axis=1, keepdims=True)
    else:
        vec, acc_shape = (1, cols), (SUBLANES, cols)
        fold = lambda x: jnp.sum(x.reshape(rows // SUBLANES, SUBLANES, cols), axis=0)
        total = lambda cnt: jnp.sum(cnt.astype(F32), axis=0, keepdims=True)

    def count(pred_fn):
        def body(c, cnt):
            return cnt + fold(jnp.where(pred_fn(sk_ref[c]), 1, 0))
        return total(lax.fori_loop(0, nch, body, jnp.zeros(acc_shape, I32), unroll=unroll))

    def bit_body(it, ubits):
        cand = ubits | lax.shift_left(jnp.int32(1), 31 - it)
        cand_b = jnp.broadcast_to(cand ^ jnp.int32(INT_MIN), (rows, cols))
        tot = count(lambda s: s >= cand_b)
        return jnp.where(tot >= topk, cand, ubits)

    ubits = lax.fori_loop(0, 32, bit_body, jnp.zeros(vec, I32))
    thr = ubits ^ jnp.int32(INT_MIN)
    thr_b = jnp.broadcast_to(thr, (rows, cols))
    need_b = jnp.broadcast_to(topk - count(lambda s: s > thr_b), (rows, cols))
    surplus = jnp.where(thr == jnp.int32(INT_MIN), 0.0, count(lambda s: s >= thr_b) - topk)
    return thr_b, need_b, jnp.max(surplus) > 0.0


def _threshold_bias(skc, thr_b):
    return jnp.where((skc >= thr_b) & (skc > jnp.int32(INT_MIN)), 0.0, MASKED)


def _select_bias(skc, thr_b, need_b, eq_before, tri, keys_on_lanes):
    rows, cols = skc.shape
    eq = skc == thr_b
    eqf = jnp.where(eq, 1.0, 0.0).astype(BF16)
    if keys_on_lanes:
        res = _dot(eqf, tri)
        prefix, chunk_total = res[:, :cols], res[:, cols:]
    else:
        res = _dot(tri, eqf)
        prefix, chunk_total = res[:rows], res[rows:]
    keep = ((skc > thr_b) | (eq & (prefix + eq_before <= need_b))) & (skc > jnp.int32(INT_MIN))
    return jnp.where(keep, 0.0, MASKED), eq_before + chunk_total


def _tri_matrix(n, keys_on_lanes):
    r = np.arange(n)
    ones = np.ones((n, n), np.float32)
    if keys_on_lanes:
        return jnp.asarray(np.concatenate([(r[:, None] <= r[None, :]).astype(np.float32), ones], axis=1), BF16)
    return jnp.asarray(np.concatenate([(r[:, None] >= r[None, :]).astype(np.float32), ones], axis=0), BF16)


def _dsa_prompt_kernel(iq_ref, iwt_ref, q_ref, ikb_ref, kb_ref, vt_ref, tri_ref, a_ref,
                       sk_ref, m_ref, l_ref, acc_ref, *, topk):
    i = pl.program_id(1)
    nch = i + 1
    iq = iq_ref[0].reshape(N_IDX_HEADS * BLK, LANES)
    iwt = iwt_ref[...]
    key_id = lax.broadcasted_iota(I32, (BLK, BLK), 0)
    q_id = lax.broadcasted_iota(I32, (BLK, BLK), 1)

    def score_chunk(c, carry):
        s = _dot_nt(ikb_ref[0, c], iq)
        score = jnp.zeros((BLK, BLK), F32)
        for h in range(N_IDX_HEADS):
            score = score + jnp.maximum(s[:, h * BLK:(h + 1) * BLK], 0.0) * iwt[h:h + 1, :]
        sk_ref[c] = jnp.where(key_id > q_id + jnp.where(c == i, 0, BLK), jnp.int32(INT_MIN), _sort_key(score))
        return carry

    lax.fori_loop(0, nch, score_chunk, 0)
    thr_b, need_b, any_tie = _kth_largest(sk_ref, nch, topk, keys_on_lanes=False)

    m_ref[...] = jnp.full(m_ref.shape, MASKED, F32)
    l_ref[...] = jnp.zeros(l_ref.shape, F32)
    acc_ref[...] = jnp.zeros(acc_ref.shape, F32)
    heads_per_group = N_HEADS // N_KV_HEADS

    def attend_chunk(c, bias):
        kc = kb_ref[0, c]
        heads = range(N_HEADS)
        cols = [slice(h * BLK, (h + 1) * BLK) for h in heads]
        lgs = [_dot_nt(kc, q_ref[0, h]) + bias for h in heads]
        m_prev = [m_ref[:, cs] for cs in cols]
        m_new = [jnp.maximum(m_prev[h], jnp.max(lgs[h], axis=0, keepdims=True)) for h in heads]
        alpha = [jnp.exp2(m_prev[h] - m_new[h]) for h in heads]
        ps = [jnp.exp2(lgs[h] - m_new[h]) for h in heads]
        for h in heads:
            l_ref[:, cols[h]] = alpha[h] * l_ref[:, cols[h]] + jnp.sum(ps[h], axis=0, keepdims=True)
            m_ref[:, cols[h]] = m_new[h]
        for h in heads:
            g = h // heads_per_group
            vg = vt_ref[0, c, g * HEAD_DIM:(g + 1) * HEAD_DIM, :]
            acc_ref[:, cols[h]] = alpha[h] * acc_ref[:, cols[h]] + _dot(vg, ps[h].astype(BF16))

    @pl.when(any_tie)
    def _():
        tri = tri_ref[...]

        def body(c, eq_before):
            bias, eq_after = _select_bias(sk_ref[c], thr_b, need_b, eq_before, tri, keys_on_lanes=False)
            attend_chunk(c, bias)
            return eq_after

        lax.fori_loop(0, nch, body, jnp.zeros((BLK, BLK), F32))

    @pl.when(jnp.logical_not(any_tie))
    def _():
        def body(c, carry):
            attend_chunk(c, _threshold_bias(sk_ref[c], thr_b))
            return carry

        lax.fori_loop(0, nch, body, 0)

    a_t = acc_ref[...] / l_ref[...]
    a_t = jnp.concatenate([a_t[:, h * BLK:(h + 1) * BLK] for h in range(N_HEADS)], axis=0)
    a_ref[0] = a_t.T.astype(BF16)


def _dsa_prompt(iq, iwt, qbd, ikb, kb, vt, tri, n, t, topk):
    nb = t // BLK
    ikb4 = ikb.reshape(n, nb, BLK, LANES)
    kb4 = kb.reshape(n, nb, BLK, ATT_KV)
    vt4 = vt.reshape(n, nb, ATT_KV, BLK)
    cols = N_HEADS * BLK
    return pl.pallas_call(
        functools.partial(_dsa_prompt_kernel, topk=topk),
        grid=(n, nb),
        in_specs=[
            pl.BlockSpec((1, N_IDX_HEADS, BLK, LANES), lambda b, i: (b * nb + i, 0, 0, 0)),
            pl.BlockSpec((N_IDX_HEADS, BLK), lambda b, i: (0, b * nb + i)),
            pl.BlockSpec((1, N_HEADS, BLK, ATT_KV), lambda b, i: (b * nb + i, 0, 0, 0)),
            pl.BlockSpec((1, nb, BLK, LANES), lambda b, i: (b, 0, 0, 0)),
            pl.BlockSpec((1, nb, BLK, ATT_KV), lambda b, i: (b, 0, 0, 0)),
            pl.BlockSpec((1, nb, ATT_KV, BLK), lambda b, i: (b, 0, 0, 0)),
            pl.BlockSpec(tri.shape, lambda b, i: (0, 0)),
        ],
        out_specs=pl.BlockSpec((1, BLK, ATT_Q), lambda b, i: (b * nb + i, 0, 0)),
        out_shape=jax.ShapeDtypeStruct((n * nb, BLK, ATT_Q), BF16),
        scratch_shapes=[
            pltpu.VMEM((nb, BLK, BLK), I32),
            pltpu.VMEM((1, cols), F32),
            pltpu.VMEM((1, cols), F32),
            pltpu.VMEM((HEAD_DIM, cols), F32),
        ],
        compiler_params=_cparams(("parallel", "arbitrary")),
        name="dsa_prompt",
    )(iq, iwt, qbd, ikb4, kb4, vt4, tri)


def _sample_scores_kernel(pt_ref, iq_ref, wb_ref, *refs, pages_per_step):
    del pt_ref
    page_refs = refs[:pages_per_step]
    sc_ref = refs[pages_per_step]
    iq = iq_ref[0]
    wb = wb_ref[0]
    s_q = iq.shape[0] // N_IDX_HEADS
    for r in range(pages_per_step):
        s = _dot(iq, page_refs[r][0].astype(BF16))
        t = jnp.maximum(s, 0.0) * wb
        sc_ref[0, r] = jnp.sum(t.reshape(N_IDX_HEADS, s_q, LANES), axis=0)


def _sample_scores(page_table, iq_s, wb_s, cache_ikt, layer_off, pages_per_step):
    ns, npages = page_table.shape
    rows = iq_s.shape[1]
    s_q = rows // N_IDX_HEADS
    steps = npages // pages_per_step

    def page_spec(r):
        return pl.BlockSpec((1, IDX_DIM, PAGE_SIZE),
                            lambda b, j, pt: (layer_off + pt[b, j * pages_per_step + r], 0, 0))

    grid_spec = pltpu.PrefetchScalarGridSpec(
        num_scalar_prefetch=1,
        grid=(ns, steps),
        in_specs=[pl.BlockSpec((1, rows, IDX_DIM), lambda b, j, pt: (b, 0, 0)),
                  pl.BlockSpec((1, rows, LANES), lambda b, j, pt: (b, 0, 0))]
                 + [page_spec(r) for r in range(pages_per_step)],
        out_specs=pl.BlockSpec((1, pages_per_step, s_q, LANES), lambda b, j, pt: (b, j, 0, 0)),
    )
    return pl.pallas_call(
        functools.partial(_sample_scores_kernel, pages_per_step=pages_per_step),
        grid_spec=grid_spec,
        out_shape=jax.ShapeDtypeStruct((ns, npages, s_q, LANES), F32),
        compiler_params=_cparams(("parallel", "arbitrary")),
        name="sample_scores",
    )(page_table, iq_s, wb_s, *([cache_ikt] * pages_per_step))


def _sample_select_kernel(sc_ref, iq_ref, wb_ref, ikn_ref, tri_ref, bias_ref, sk_ref, *, topk, s_q):
    sb = sc_ref.shape[0]
    npages = sc_ref.shape[1]
    rows = sb * s_q

    def key_chunk(c, carry):
        sk_ref[c] = _sort_key(sc_ref[:, c].reshape(rows, LANES))
        return carry

    lax.fori_loop(0, npages, key_chunk, 0)
    row_id = lax.broadcasted_iota(I32, (s_q, LANES), 0)
    col_id = lax.broadcasted_iota(I32, (s_q, LANES), 1)
    for b in range(sb):
        s = _dot_nt(iq_ref[b], ikn_ref[b])
        t = jnp.maximum(s, 0.0) * wb_ref[b]
        score = jnp.sum(t.reshape(N_IDX_HEADS, s_q, LANES), axis=0)
        key = jnp.where(col_id <= row_id, _sort_key(score), jnp.int32(INT_MIN))
        sk_ref[npages, b * s_q:(b + 1) * s_q, :] = key
    nch = npages + 1
    thr_b, need_b, any_tie = _kth_largest(sk_ref, nch, topk, keys_on_lanes=True, unroll=4)

    @pl.when(any_tie)
    def _():
        tri = tri_ref[...]

        def body(c, eq_before):
            bias, eq_after = _select_bias(sk_ref[c], thr_b, need_b, eq_before, tri, keys_on_lanes=True)
            bias_ref[:, c] = bias.reshape(sb, s_q, LANES)
            return eq_after

        lax.fori_loop(0, nch, body, jnp.zeros((rows, LANES), F32))

    @pl.when(jnp.logical_not(any_tie))
    def _():
        def body(c, carry):
            bias_ref[:, c] = _threshold_bias(sk_ref[c], thr_b).reshape(sb, s_q, LANES)
            return carry

        lax.fori_loop(0, nch, body, 0, unroll=4)


def _sample_select(scores, iq_s, wb_s, ikn, tri, topk, sb):
    ns, npages, s_q, _ = scores.shape
    rows = iq_s.shape[1]
    return pl.pallas_call(
        functools.partial(_sample_select_kernel, topk=topk, s_q=s_q),
        grid=(ns // sb,),
        in_specs=[
            pl.BlockSpec((sb, npages, s_q, LANES), lambda g: (g, 0, 0, 0)),
            pl.BlockSpec((sb, rows, IDX_DIM), lambda g: (g, 0, 0)),
            pl.BlockSpec((sb, rows, LANES), lambda g: (g, 0, 0)),
            pl.BlockSpec((sb, PAGE_SIZE, IDX_DIM), lambda g: (g, 0, 0)),
            pl.BlockSpec(tri.shape, lambda g: (0, 0)),
        ],
        out_specs=pl.BlockSpec((sb, npages + 1, s_q, LANES), lambda g: (g, 0, 0, 0)),
        out_shape=jax.ShapeDtypeStruct((ns, npages + 1, s_q, LANES), F32),
        scratch_shapes=[pltpu.VMEM((npages + 1, sb * s_q, LANES), I32)],
        compiler_params=_cparams(("parallel",)),
        name="sample_select",
    )(scores, iq_s, wb_s, ikn, tri)


def _sample_attend_kernel(pt_ref, q_ref, bias_ref, bias_new_ref, kn_ref, vn_ref, *refs, pages_per_step, s_q):
    del pt_ref
    k_refs = refs[:pages_per_step]
    v_refs = refs[pages_per_step:2 * pages_per_step]
    a_ref, m_ref, l_ref, acc_ref = refs[2 * pages_per_step:]
    j = pl.program_id(1)
    rows = N_HEADS * s_q

    @pl.when(j == 0)
    def _():
        m_ref[...] = jnp.full(m_ref.shape, MASKED, F32)
        l_ref[...] = jnp.zeros(l_ref.shape, F32)
        acc_ref[...] = jnp.zeros(acc_ref.shape, F32)

    q = q_ref[0]

    def masked(lg, bias):
        return (lg.reshape(N_HEADS, s_q, LANES) + bias[None]).reshape(rows, LANES)

    def update(lgs, weighted_values):
        top = functools.reduce(jnp.maximum, lgs)
        m_prev = m_ref[...]
        m_new = jnp.maximum(m_prev, jnp.max(top, axis=1, keepdims=True))
        alpha = jnp.exp2(m_prev - m_new)
        ps = [jnp.exp2(lg - m_new) for lg in lgs]
        l_ref[...] = alpha * l_ref[...] + jnp.sum(functools.reduce(jnp.add, ps), axis=1, keepdims=True)
        m_ref[...] = m_new
        acc_ref[...] = alpha * acc_ref[...] + weighted_values([p.astype(BF16) for p in ps])

    lgs = [masked(_dot(q, k_refs[r][0].astype(BF16)), bias_ref[0, r]) for r in range(pages_per_step)]
    update(lgs, lambda ps: functools.reduce(
        jnp.add, [_dot_nt(p, v_refs[r][0].astype(BF16)) for r, p in enumerate(ps)]))

    @pl.when(j == pl.num_programs(1) - 1)
    def _():
        lg = masked(_dot_nt(q, kn_ref[0].astype(BF16)), bias_new_ref[0, 0])
        update([lg], lambda ps: _dot(ps[0], vn_ref[0].astype(BF16)))
        a_ref[0] = (acc_ref[...] / l_ref[...]).astype(BF16)


def _sample_attend(page_table, q_s, bias, kn, vn, cache_kt, cache_vt, layer_off, pages_per_step):
    ns, npages = page_table.shape
    rows = q_s.shape[1]
    s_q = rows // N_HEADS
    steps = npages // pages_per_step

    def page_spec(r):
        return pl.BlockSpec((1, ATT_KV, PAGE_SIZE),
                            lambda b, j, pt: (layer_off + pt[b, j * pages_per_step + r], 0, 0))

    new_spec = pl.BlockSpec((1, PAGE_SIZE, ATT_KV), lambda b, j, pt: (b, 0, 0))
    grid_spec = pltpu.PrefetchScalarGridSpec(
        num_scalar_prefetch=1,
        grid=(ns, steps),
        in_specs=[pl.BlockSpec((1, rows, ATT_KV), lambda b, j, pt: (b, 0, 0)),
                  pl.BlockSpec((1, pages_per_step, s_q, LANES), lambda b, j, pt: (b, j, 0, 0)),
                  pl.BlockSpec((1, 1, s_q, LANES), lambda b, j, pt: (b, npages, 0, 0)),
                  new_spec, new_spec]
                 + [page_spec(r) for r in range(pages_per_step)] * 2,
        out_specs=pl.BlockSpec((1, rows, ATT_KV), lambda b, j, pt: (b, 0, 0)),
        scratch_shapes=[
            pltpu.VMEM((rows, 1), F32),
            pltpu.VMEM((rows, 1), F32),
            pltpu.VMEM((rows, ATT_KV), F32),
        ],
    )
    return pl.pallas_call(
        functools.partial(_sample_attend_kernel, pages_per_step=pages_per_step, s_q=s_q),
        grid_spec=grid_spec,
        out_shape=jax.ShapeDtypeStruct((ns, rows, ATT_KV), BF16),
        compiler_params=_cparams(("parallel", "arbitrary")),
        name="sample_attend",
    )(page_table, q_s, bias, bias, kn, vn,
      *([cache_kt] * pages_per_step), *([cache_vt] * pages_per_step))


def _merge_kernel(x_ref, a_ref, d_ref, u_ref, gvn_ref, wc_ref, gb_ref, wg_ref, wba_ref, pbd_ref, psc_ref,
                  wbp_ref, wbg_ref, wo_ref, lng_ref, lnb_ref, rwt_ref, rb_ref,
                  x1_ref, gate_ref, *, alpha):
    tm = x_ref.shape[0]
    x = x_ref[...]
    xb = x.astype(BF16)

    def gate(idx):
        return _sigmoid(_dot(xb, wg_ref[:, idx * D_MODEL:(idx + 1) * D_MODEL]))

    m = gate(0) * _dot(a_ref[...], wba_ref[...])
    y = _dot(d_ref[...].astype(BF16), pbd_ref[...]) * psc_ref[...]
    m = m + gate(1) * _dot(y.astype(BF16), wbp_ref[...])
    gv = gvn_ref[...].astype(BF16)
    grp = lax.broadcasted_iota(I32, (tm, GMLP_WIDTH), 1) // GMLP_GC
    mix = gb_ref[...]
    for g in range(GMLP_GROUPS):
        mix = mix + jnp.where(grp == g, _dot(wc_ref[g], gv), 0.0)
    c = u_ref[...] * mix
    m = m + gate(2) * _dot(c.astype(BF16), wbg_ref[...])
    y = _dot(m.astype(BF16), wo_ref[...])
    x1 = _layer_norm(alpha * x + y, lng_ref[...], lnb_ref[...])
    x1_ref[...] = x1

    scores = _sigmoid(_dot_nt(rwt_ref[...], x1.astype(BF16)))
    sel = scores + rb_ref[...]
    per = N_EXPERTS // N_EXPERT_GROUPS
    g3 = sel.reshape(N_EXPERT_GROUPS, per, tm)
    sub = lax.broadcasted_iota(I32, (N_EXPERT_GROUPS, per, tm), 1)
    m1 = jnp.max(g3, axis=1, keepdims=True)
    first = jnp.min(jnp.where(g3 == m1, sub, per), axis=1, keepdims=True)
    m2 = jnp.max(jnp.where(sub == first, -jnp.inf, g3), axis=1, keepdims=True)
    gs = (m1 + m2).reshape(N_EXPERT_GROUPS, tm)

    def rank_of(vals, count):
        idx = lax.broadcasted_iota(I32, vals.shape, 0)
        rank = jnp.zeros(vals.shape, I32)
        for o in range(count):
            other = vals[o:o + 1]
            beats = (other > vals) | ((other == vals) & (o < idx))
            rank = rank + jnp.where(beats, 1, 0)
        return rank

    gkeep = rank_of(gs, N_EXPERT_GROUPS) < TOPK_GROUPS
    ekeep = jnp.broadcast_to(gkeep.reshape(N_EXPERT_GROUPS, 1, tm), (N_EXPERT_GROUPS, per, tm)).reshape(N_EXPERTS, tm)
    sel = jnp.where(ekeep, sel, -jnp.inf)
    chosen = rank_of(sel, N_EXPERTS) < TOPK_EXPERTS
    wsel = jnp.where(chosen, scores, 0.0)
    gate_t = wsel / jnp.sum(wsel, axis=0, keepdims=True) * ROUTED_SCALE
    gate_ref[...] = gate_t.T


def _merge(x, a_bd, d, u, gvn, wl, tm, alpha):
    m = x.shape[0]
    row = lambda w: pl.BlockSpec((tm, w), lambda i: (i, 0))
    full = lambda a: pl.BlockSpec(a.shape, lambda i: (0,) * a.ndim)
    weights = [wl["wc"], wl["gb"], wl["wg"], wl["wba"], wl["pbd"], wl["psc"], wl["wbp"], wl["wbg"], wl["wo"],
               wl["ln1_g"], wl["ln1_b"], wl["rwt"], wl["rb"]]
    return pl.pallas_call(
        functools.partial(_merge_kernel, alpha=alpha),
        grid=(m // tm,),
        in_specs=[row(D_MODEL), row(ATT_Q), row(POOL_WIDTH), row(GMLP_WIDTH), row(GMLP_WIDTH)]
                 + [full(w) for w in weights],
        out_specs=(row(D_MODEL), row(N_EXPERTS)),
        out_shape=(jax.ShapeDtypeStruct((m, D_MODEL), F32), jax.ShapeDtypeStruct((m, N_EXPERTS), F32)),
        compiler_params=_cparams(("parallel",)),
        name="merge",
    )(x, a_bd, d, u, gvn, *weights)


def _swiglu_act(xb, w_gu):
    h = _dot(xb, w_gu)
    g = h[:, :EXPERT_FF]
    return g * _sigmoid(g) * h[:, EXPERT_FF:]


def _moe_kernel(x_ref, gate_ref, wgu_ref, wd_ref, sgu_ref, sd_ref, lng_ref, lnb_ref, o_ref, xb_ref, acc_ref,
                *, alpha):
    step = pl.program_id(1)

    @pl.when(step == 0)
    def _():
        xb_ref[...] = x_ref[...].astype(BF16)
        acc_ref[...] = _dot(_swiglu_act(xb_ref[...], sgu_ref[...]).astype(BF16), sd_ref[...])

    xb = xb_ref[...]
    gate = gate_ref[...]
    lane = lax.broadcasted_iota(I32, gate.shape, 1)
    acts = []
    for j in range(MOE_EXPERTS_PER_STEP):
        e = step * MOE_EXPERTS_PER_STEP + j
        gcol = jnp.sum(jnp.where(lane == e, gate, 0.0), axis=1, keepdims=True)
        acts.append((gcol * _swiglu_act(xb, wgu_ref[j].astype(BF16))).astype(BF16))
    w_down = wd_ref[...].astype(BF16).reshape(MOE_EXPERTS_PER_STEP * EXPERT_FF, D_MODEL)
    acc_ref[...] += _dot(jnp.concatenate(acts, axis=1), w_down)

    @pl.when(step == pl.num_programs(1) - 1)
    def _():
        o_ref[...] = _layer_norm(alpha * x_ref[...] + acc_ref[...], lng_ref[...], lnb_ref[...])


def _moe(x1, gate, w_gu, w_down, sh_gu, sh_down, ln_g, ln_b, tm, alpha):
    m = x1.shape[0]
    full = lambda a: pl.BlockSpec(a.shape, lambda i, e: (0,) * a.ndim)
    per = MOE_EXPERTS_PER_STEP
    return pl.pallas_call(
        functools.partial(_moe_kernel, alpha=alpha),
        grid=(m // tm, N_EXPERTS // per),
        in_specs=[
            pl.BlockSpec((tm, D_MODEL), lambda i, e: (i, 0)),
            pl.BlockSpec((tm, N_EXPERTS), lambda i, e: (i, 0)),
            pl.BlockSpec((per, D_MODEL, 2 * EXPERT_FF), lambda i, e: (e, 0, 0)),
            pl.BlockSpec((per, EXPERT_FF, D_MODEL), lambda i, e: (e, 0, 0)),
            full(sh_gu), full(sh_down), full(ln_g), full(ln_b),
        ],
        out_specs=pl.BlockSpec((tm, D_MODEL), lambda i, e: (i, 0)),
        out_shape=jax.ShapeDtypeStruct((m, D_MODEL), F32),
        scratch_shapes=[pltpu.VMEM((tm, D_MODEL), BF16), pltpu.VMEM((tm, D_MODEL), F32)],
        compiler_params=_cparams(("parallel", "arbitrary")),
        name="moe",
    )(x1, gate, w_gu, w_down, sh_gu, sh_down, ln_g, ln_b)


def _rope_tables(pos):
    half = HEAD_DIM // 2
    inv = ROPE_THETA ** (-jnp.arange(half, dtype=F32) / half)
    ang = pos.astype(F32)[:, None] * inv[None, :]
    cos, sin = jnp.cos(ang), jnp.sin(ang)
    cos_t = jnp.tile(cos, (1, LANES // half))
    sin_t = jnp.tile(jnp.concatenate([-sin, sin], axis=1), (1, LANES // HEAD_DIM))
    return cos_t, sin_t


def _block_diag(blocks):
    g, r, c = blocks.shape
    eye = jnp.eye(g, dtype=blocks.dtype)
    return jnp.einsum("grc,gh->grhc", blocks, eye).reshape(g * r, g * c)


def _layer_weights(l, w_in, w_ba, w_bp, w_bg, w_out, pool_w, pool_scale, g_ln_g, g_ln_b, g_ws, g_b,
                   ln1_g, ln1_b, router_w, router_bias, sh_gu, sh_down, ln2_g, ln2_b):
    w = w_in[l]
    sizes = (ATT_Q, ATT_KV, ATT_KV, N_IDX_HEADS * IDX_DIM, IDX_DIM, N_IDX_HEADS,
             POOL_WIDTH, GMLP_WIDTH, GMLP_WIDTH, N_BRANCH * D_MODEL)
    offs = np.concatenate([[0], np.cumsum(sizes)]).tolist()
    wq, wk, wv, wiq, wik, wiw, wp, wu, wgv, wg = [w[:, offs[j]:offs[j + 1]] for j in range(len(sizes))]
    wq = wq * (HEAD_DIM ** -0.5 * LOG2E)
    wikw = jnp.pad(jnp.concatenate([wik, wiw], axis=1), ((0, 0), (0, LANES - IDX_DIM - N_IDX_HEADS)))
    w_cat = jnp.concatenate([wq, wk, wv, wiq, wikw, wp, wu, wgv], axis=1).astype(BF16)
    return dict(
        w_cat=w_cat, wvt=wv.T.astype(BF16), wiwt=wiw.T.astype(BF16), wg=wg.astype(BF16),
        wba=w_ba[l].astype(BF16),
        pbd=_block_diag(pool_w[l]).astype(BF16), psc=pool_scale[l].reshape(1, POOL_WIDTH),
        wbp=w_bp[l].astype(BF16), wbg=w_bg[l].astype(BF16), wo=w_out[l].astype(BF16),
        g_ln_g=g_ln_g[l].reshape(1, GMLP_WIDTH), g_ln_b=g_ln_b[l].reshape(1, GMLP_WIDTH),
        ws=g_ws[l], gbias=g_b[l],
        ln1_g=ln1_g[l].reshape(1, D_MODEL), ln1_b=ln1_b[l].reshape(1, D_MODEL),
        rwt=router_w[l].T.astype(BF16), rbias=router_bias[l],
        sh_gu=sh_gu[l].astype(BF16), sh_down=sh_down[l].astype(BF16),
        ln2_g=ln2_g[l].reshape(1, D_MODEL), ln2_b=ln2_b[l].reshape(1, D_MODEL),
    )


def _chunk_mix(wl, cl, tm):
    tril = jnp.tril(jnp.ones((cl, cl), F32))
    wm = wl["ws"][:, :cl, :cl] * tril
    reps = tm // cl
    wc = jnp.stack([_block_diag(jnp.broadcast_to(wm[g], (reps, cl, cl))) for g in range(GMLP_GROUPS)])
    gb = jnp.tile(jnp.repeat(wl["gbias"][:, :cl].T, GMLP_GC, axis=1), (reps, 1))
    return wc.astype(BF16), gb


def _to_rows_by_seq(a, ns, s_q):
    nblk, heads, _, width = a.shape
    per = BLK // s_q
    a = a.reshape(nblk, heads, per, s_q, width).transpose(0, 2, 1, 3, 4)
    return a.reshape(ns, heads * s_q, width)


def kernel(x_prompt, x_sample, cache_k, cache_v, cache_idx_k, state_pool, page_table, w_in, w_branch_attn,
           w_branch_pool, w_branch_gmlp, w_out, pool_w, pool_scale, gmlp_ln_g, gmlp_ln_b, gmlp_ws, gmlp_b,
           ln1_g, ln1_b, router_w, router_bias, expert_w_gu, expert_w_down, shared_w_gu, shared_w_down,
           ln2_g, ln2_b):
    n_p, t_p, _ = x_prompt.shape
    n_s, t_s, _ = x_sample.shape
    depth = w_in.shape[0]
    n_pool = cache_k.shape[1]
    npages = page_table.shape[1]
    past = npages * PAGE_SIZE
    m_p, m_s = n_p * t_p, n_s * t_s
    alpha = (2 * depth) ** 0.25
    assert t_p % BLK == 0 and m_s % BLK == 0 and BLK % t_s == 0 and t_s <= 16

    tm_p = BLK
    tm_s = BLK
    tm_moe = 1024 if m_p % 1024 == 0 else tm_p
    tp_pool = 512 if t_p % 512 == 0 else BLK
    pages_per_step = 16 if npages % 16 == 0 else 1
    score_pages_per_step = 32 if npages % 32 == 0 else pages_per_step
    sel_batch = 8 if n_s % 8 == 0 else 1
    topk_p = min(TOPK_MAX, t_p // 4)
    topk_s = min(TOPK_MAX, (past + t_s) // 4)

    cos_p, sin_p = _rope_tables(jnp.arange(t_p, dtype=I32))
    cos_s, sin_s = _rope_tables(past + jnp.arange(t_s, dtype=I32))
    cos_s, sin_s = jnp.tile(cos_s, (m_s // t_s, 1)), jnp.tile(sin_s, (m_s // t_s, 1))
    tri_p = _tri_matrix(BLK, keys_on_lanes=False)
    tri_s = _tri_matrix(LANES, keys_on_lanes=True)
    cache_kt = cache_k.transpose(0, 1, 3, 4, 2).reshape(depth * n_pool, ATT_KV, PAGE_SIZE)
    cache_vt = cache_v.transpose(0, 1, 3, 4, 2).reshape(depth * n_pool, ATT_KV, PAGE_SIZE)
    cache_ikt = cache_idx_k.transpose(0, 1, 3, 2).reshape(depth * n_pool, IDX_DIM, PAGE_SIZE)

    hp = x_prompt.reshape(m_p, D_MODEL)
    hs = x_sample.reshape(m_s, D_MODEL)
    outs = {name: [] for name in ("kp", "vp", "ikp", "pp", "ks", "vs", "iks", "ps", "gs")}
    for l in range(depth):
        wl = _layer_weights(l, w_in, w_branch_attn, w_branch_pool, w_branch_gmlp, w_out, pool_w, pool_scale,
                            gmlp_ln_g, gmlp_ln_b, gmlp_ws, gmlp_b, ln1_g, ln1_b, router_w, router_bias,
                            shared_w_gu, shared_w_down, ln2_g, ln2_b)

        def finish(x, a_bd, d, u, gvn, cl, tm, tm_e):
            wc, gb = _chunk_mix(wl, cl, tm)
            wm = dict(wl, wc=wc, gb=gb, rb=jnp.broadcast_to(wl["rbias"][:, None], (N_EXPERTS, tm)))
            x1, gate = _merge(x, a_bd, d, u, gvn, wm, tm, alpha)
            return _moe(x1, gate, expert_w_gu[l], expert_w_down[l], wl["sh_gu"], wl["sh_down"],
                        wl["ln2_g"], wl["ln2_b"], tm_e, alpha)

        qbd, k, v, kb, vt, iq, ikw, ikb, iwt, p, u, gvn = _proj(hp, wl, cos_p, sin_p, tm_p)
        p3 = p.reshape(n_p, t_p, POOL_WIDTH)
        d = _pool(p3, None, 0, tp_pool).reshape(m_p, POOL_WIDTH)
        a = _dsa_prompt(iq, iwt, qbd, ikb, kb, vt, tri_p, n_p, t_p, topk_p).reshape(m_p, ATT_Q)
        hp = finish(hp, a, d, u, gvn, CHUNK, tm_p, tm_moe)
        outs["kp"].append(k.reshape(n_p, t_p, N_KV_HEADS, HEAD_DIM))
        outs["vp"].append(v.reshape(n_p, t_p, N_KV_HEADS, HEAD_DIM))
        outs["ikp"].append(ikw[:, :IDX_DIM].reshape(n_p, t_p, IDX_DIM))
        outs["pp"].append(p3[:, t_p - POOL_STATE:])

        qbd, k, v, kb, vt, iq, ikw, ikb, iwt, p, u, gvn = _proj(hs, wl, cos_s, sin_s, tm_s)
        p3 = p.reshape(n_s, t_s, POOL_WIDTH)
        prefix16 = jnp.pad(state_pool[l], ((0, 0), (16 - POOL_STATE, 0), (0, 0)))
        d = _pool(p3, prefix16, past, t_s).reshape(m_s, POOL_WIDTH)
        iq_s = _to_rows_by_seq(iq, n_s, t_s)[:, :, :IDX_DIM]
        q_s = _to_rows_by_seq(qbd, n_s, t_s)
        wb_s = jnp.broadcast_to(iwt.reshape(N_IDX_HEADS, n_s, t_s).transpose(1, 0, 2).reshape(n_s, -1, 1),
                                (n_s, N_IDX_HEADS * t_s, LANES))
        pad_rows = lambda a: jnp.pad(a.reshape(n_s, t_s, -1), ((0, 0), (0, PAGE_SIZE - t_s), (0, 0)))
        ikn = pad_rows(ikb[:, :IDX_DIM])
        scores = _sample_scores(page_table, iq_s, wb_s, cache_ikt, l * n_pool, score_pages_per_step)
        bias = _sample_select(scores, iq_s, wb_s, ikn, tri_s, topk_s, sel_batch)
        a_s = _sample_attend(page_table, q_s, bias, pad_rows(k), pad_rows(v), cache_kt, cache_vt,
                             l * n_pool, pages_per_step)
        a_s = a_s.reshape(n_s, N_HEADS, t_s, N_KV_HEADS, HEAD_DIM)
        a_s = jnp.stack([a_s[:, h, :, h // (N_HEADS // N_KV_HEADS)] for h in range(N_HEADS)], axis=2)
        hs = finish(hs, a_s.reshape(m_s, ATT_Q), d, u, gvn, t_s, tm_s, tm_s)
        outs["ks"].append(k.reshape(n_s, t_s, N_KV_HEADS, HEAD_DIM))
        outs["vs"].append(v.reshape(n_s, t_s, N_KV_HEADS, HEAD_DIM))
        outs["iks"].append(ikw[:, :IDX_DIM].reshape(n_s, t_s, IDX_DIM))
        outs["ps"].append(jnp.concatenate([state_pool[l], p3], axis=1)[:, -POOL_STATE:])
        outs["gs"].append(gvn.reshape(n_s, t_s, GMLP_WIDTH))

    st = lambda name: jnp.stack(outs[name])
    return (hp.reshape(n_p, t_p, D_MODEL), hs.reshape(n_s, t_s, D_MODEL),
            st("kp"), st("vp"), st("ikp"), st("ks"), st("vs"), st("iks"), st("pp"), st("ps"), st("gs"))
```

```python
import functools

import jax
import jax.numpy as jnp
import numpy as np
from jax import lax
from jax.experimental import pallas as pl
from jax.experimental.pallas import tpu as pltpu

F32 = jnp.float32
BF16 = jnp.bfloat16
I32 = jnp.int32
I16 = jnp.int16

D_MODEL = 1024
N_HEADS = 8
N_KV_HEADS = 4
HEAD_DIM = 64
N_IDX_HEADS = 8
IDX_DIM = 64
TOPK_MAX = 256
PAGE_SIZE = 128
ROPE_THETA = 10000.0
POOL_WINDOWS = (2, 4, 8, 16)
POOL_WIDTH = 256
POOL_GC = 64
POOL_STATE = 15
GMLP_WIDTH = 256
GMLP_GROUPS = 4
GMLP_GC = 64
CHUNK = 128
N_BRANCH = 3
ATT_Q = N_HEADS * HEAD_DIM
ATT_KV = N_KV_HEADS * HEAD_DIM
N_EXPERTS = 64
TOPK_EXPERTS = 8
N_EXPERT_GROUPS = 8
TOPK_GROUPS = 4
EXPERT_FF = 256
ROUTED_SCALE = 2.5
LN_EPS = 1e-5

LANES = 128
SUBLANES = 8
PACK16 = 16
HALF16 = 2 ** 15
BLK = 256
MOE_EXPERTS_PER_STEP = 2
INT_MIN = -2 ** 31
MASKED = -1e30
LOG2E = 1.4426950408889634
VMEM_LIMIT = 56 * 1024 * 1024

C_Q = 0
C_K = C_Q + ATT_Q
C_V = C_K + ATT_KV
C_IQ = C_V + ATT_KV
C_IKW = C_IQ + N_IDX_HEADS * IDX_DIM
C_P = C_IKW + LANES
C_U = C_P + POOL_WIDTH
C_GV = C_U + GMLP_WIDTH
C_END = C_GV + GMLP_WIDTH


def _cparams(sem):
    return pltpu.CompilerParams(dimension_semantics=sem, vmem_limit_bytes=VMEM_LIMIT)


def _layer_norm(x, g, b):
    mu = jnp.mean(x, axis=-1, keepdims=True)
    xc = x - mu
    var = jnp.mean(xc * xc, axis=-1, keepdims=True)
    return xc * lax.rsqrt(var + LN_EPS) * g + b


def _sigmoid(x):
    return 1.0 / (1.0 + jnp.exp(-x))


def _dot(a, b):
    return jnp.dot(a, b, preferred_element_type=F32)


def _dot_nt(a, b):
    return lax.dot_general(a, b, (((1,), (1,)), ((), ())), preferred_element_type=F32)


def _sort_key(score):
    score = jnp.where(score == 0.0, 0.0, score)
    bits = lax.bitcast_convert_type(score, I32)
    return bits ^ ((bits >> 31) & jnp.int32(0x7FFFFFFF))


def _proj_kernel(x_ref, w_ref, wvt_ref, wiwt_ref, cos_ref, sin_ref, lng_ref, lnb_ref,
                 qbd_ref, k_ref, v_ref, kb_ref, vt_ref, iq_ref, ikw_ref, ikb_ref, iwt_ref,
                 p_ref, u_ref, gvn_ref):
    tm = x_ref.shape[0]
    xb = x_ref[...].astype(BF16)
    cos = cos_ref[...]
    sin = sin_ref[...]
    lane = lax.broadcasted_iota(I32, (tm, LANES), 1)
    first_half = (lane % HEAD_DIM) < (HEAD_DIM // 2)

    def mm(c0, width):
        return _dot(xb, w_ref[:, c0:c0 + width])

    def rope(z):
        partner = jnp.where(first_half, pltpu.roll(z, LANES - 32, 1), pltpu.roll(z, 32, 1))
        return z * cos + partner * sin

    def rope_wide(z):
        return jnp.concatenate([rope(z[:, s * LANES:(s + 1) * LANES]) for s in range(z.shape[1] // LANES)], axis=1)

    nblk = tm // BLK
    low_half = lane < HEAD_DIM
    zero_slab = jnp.zeros((tm, LANES), F32)

    def head_slabs(z):
        for s in range(z.shape[1] // LANES):
            slab = z[:, s * LANES:(s + 1) * LANES]
            swapped = pltpu.roll(slab, HEAD_DIM, 1)
            for r in range(2):
                yield 2 * s + r, (slab, swapped) if r == 0 else (swapped, slab)

    for h, (head_low, head_high) in head_slabs(rope_wide(mm(C_Q, ATT_Q))):
        g = h // (N_HEADS // N_KV_HEADS)
        own = jnp.where(low_half, head_low, 0.0) if g % 2 == 0 else jnp.where(low_half, 0.0, head_high)
        z = jnp.concatenate([own, zero_slab] if g // 2 == 0 else [zero_slab, own], axis=1).astype(BF16)
        for b in range(nblk):
            qbd_ref[b, h] = z[b * BLK:(b + 1) * BLK]
    k = rope_wide(mm(C_K, ATT_KV))
    k_ref[...] = k
    kb_ref[...] = k.astype(BF16)
    v_ref[...] = mm(C_V, ATT_KV)
    vt = _dot_nt(wvt_ref[...], xb).astype(BF16)
    for b in range(nblk):
        vt_ref[b] = vt[:, b * BLK:(b + 1) * BLK]
    for h, (head_low, _) in head_slabs(rope_wide(mm(C_IQ, N_IDX_HEADS * IDX_DIM))):
        z = jnp.where(low_half, head_low, 0.0).astype(BF16)
        for b in range(nblk):
            iq_ref[b, h] = z[b * BLK:(b + 1) * BLK]
    z = mm(C_IKW, LANES)
    is_key = lane < IDX_DIM
    ikw = jnp.where(is_key, rope(z), z)
    ikw_ref[...] = ikw
    ikb_ref[...] = jnp.where(is_key, ikw, 0.0).astype(BF16)
    iwt_ref[...] = _dot_nt(wiwt_ref[...], xb)
    p_ref[...] = mm(C_P, POOL_WIDTH)
    u_ref[...] = mm(C_U, GMLP_WIDTH)
    gvn_ref[...] = _layer_norm(mm(C_GV, GMLP_WIDTH), lng_ref[...], lnb_ref[...])


def _proj(x, wl, cos_t, sin_t, tm):
    m = x.shape[0]
    nt = cos_t.shape[0] // tm
    row = lambda w: pl.BlockSpec((tm, w), lambda i: (i, 0))
    full = lambda a: pl.BlockSpec(a.shape, lambda i: (0,) * a.ndim)
    tab = pl.BlockSpec((tm, LANES), lambda i: (i % nt, 0))
    nb = tm // BLK
    out_shape = (
        jax.ShapeDtypeStruct((m // BLK, N_HEADS, BLK, ATT_KV), BF16),
        jax.ShapeDtypeStruct((m, ATT_KV), F32),
        jax.ShapeDtypeStruct((m, ATT_KV), F32),
        jax.ShapeDtypeStruct((m, ATT_KV), BF16),
        jax.ShapeDtypeStruct((m // BLK, ATT_KV, BLK), BF16),
        jax.ShapeDtypeStruct((m // BLK, N_IDX_HEADS, BLK, LANES), BF16),
        jax.ShapeDtypeStruct((m, LANES), F32),
        jax.ShapeDtypeStruct((m, LANES), BF16),
        jax.ShapeDtypeStruct((N_IDX_HEADS, m), F32),
        jax.ShapeDtypeStruct((m, POOL_WIDTH), F32),
        jax.ShapeDtypeStruct((m, GMLP_WIDTH), F32),
        jax.ShapeDtypeStruct((m, GMLP_WIDTH), F32),
    )
    out_specs = (
        pl.BlockSpec((nb, N_HEADS, BLK, ATT_KV), lambda i: (i, 0, 0, 0)),
        row(ATT_KV), row(ATT_KV), row(ATT_KV),
        pl.BlockSpec((nb, ATT_KV, BLK), lambda i: (i, 0, 0)),
        pl.BlockSpec((nb, N_IDX_HEADS, BLK, LANES), lambda i: (i, 0, 0, 0)),
        row(LANES), row(LANES),
        pl.BlockSpec((N_IDX_HEADS, tm), lambda i: (0, i)),
        row(POOL_WIDTH), row(GMLP_WIDTH), row(GMLP_WIDTH),
    )
    weights = [wl["w_cat"], wl["wvt"], wl["wiwt"]]
    return pl.pallas_call(
        _proj_kernel,
        grid=(m // tm,),
        in_specs=[row(D_MODEL)] + [full(w) for w in weights] + [tab, tab, full(wl["g_ln_g"]), full(wl["g_ln_b"])],
        out_specs=out_specs,
        out_shape=out_shape,
        compiler_params=_cparams(("parallel",)),
        name="proj",
    )(x, *weights, cos_t, sin_t, wl["g_ln_g"], wl["g_ln_b"])


def _pool_kernel(p_ref, halo_ref, d_ref, ext_ref, s2_ref, s4_ref, s8_ref, *, pos_base, halo_is_prefix):
    tp = p_ref.shape[1]
    r_end = tp + 32
    i = pl.program_id(1)
    p = p_ref[0]
    halo = halo_ref[0]
    if not halo_is_prefix:
        halo = jnp.where(i == 0, 0.0, halo)
    ext_ref[0:16, :] = jnp.zeros((16, POOL_WIDTH), F32)
    ext_ref[16:32, :] = halo
    ext_ref[32:r_end, :] = p
    s2_ref[8:r_end, :] = ext_ref[8:r_end, :] + ext_ref[7:r_end - 1, :]
    s4_ref[16:r_end, :] = s2_ref[16:r_end, :] + s2_ref[14:r_end - 2, :]
    s8_ref[24:r_end, :] = s4_ref[24:r_end, :] + s4_ref[20:r_end - 4, :]
    s16 = s8_ref[32:r_end, :] + s8_ref[24:r_end - 8, :]
    lane = lax.broadcasted_iota(I32, (tp, POOL_WIDTH), 1)
    grp = lane // POOL_GC
    win = jnp.where(grp == 0, s2_ref[32:r_end, :],
                    jnp.where(grp == 1, s4_ref[32:r_end, :],
                              jnp.where(grp == 2, s8_ref[32:r_end, :], s16)))
    width = jnp.where(grp == 0, POOL_WINDOWS[0],
                      jnp.where(grp == 1, POOL_WINDOWS[1],
                                jnp.where(grp == 2, POOL_WINDOWS[2], POOL_WINDOWS[3])))
    pos = pos_base + i * tp + lax.broadcasted_iota(I32, (tp, POOL_WIDTH), 0)
    cnt = jnp.minimum(width, pos + 1).astype(F32)
    d_ref[0] = win / cnt - p


def _pool(p3, prefix16, pos_base, tp):
    n, t, _ = p3.shape
    halo_is_prefix = prefix16 is not None
    if halo_is_prefix:
        halo = prefix16
        halo_spec = pl.BlockSpec((1, 16, POOL_WIDTH), lambda b, i: (b, 0, 0))
    else:
        halo = p3
        step = tp // 16
        halo_spec = pl.BlockSpec((1, 16, POOL_WIDTH), lambda b, i: (b, jnp.maximum(i * step - 1, 0), 0))
    rows = tp + 32
    return pl.pallas_call(
        functools.partial(_pool_kernel, pos_base=pos_base, halo_is_prefix=halo_is_prefix),
        grid=(n, t // tp),
        in_specs=[pl.BlockSpec((1, tp, POOL_WIDTH), lambda b, i: (b, i, 0)), halo_spec],
        out_specs=pl.BlockSpec((1, tp, POOL_WIDTH), lambda b, i: (b, i, 0)),
        out_shape=jax.ShapeDtypeStruct((n, t, POOL_WIDTH), F32),
        scratch_shapes=[pltpu.VMEM((rows, POOL_WIDTH), F32) for _ in range(4)],
        compiler_params=_cparams(("parallel", "parallel")),
        name="pool",
    )(p3, halo)


def _kth_largest(sk_ref, hi_ref, lo_ref, nch, topk, keys_on_lanes, unroll=1):
    rows, cols = sk_ref.shape[1:]
    if keys_on_lanes:
        vec = (rows, 1)
        fold, acc_shape = (lambda x: x), (rows, cols)
        fold16, acc16_shape = fold, acc_shape
        total = lambda cnt: jnp.sum(cnt.astype(F32), axis=1, keepdims=True)
    else:
        vec = (1, cols)
        fold, acc_shape = (lambda x: jnp.sum(x.reshape(rows // SUBLANES, SUBLANES, cols), axis=0)), (SUBLANES, cols)
        fold16 = lambda x: functools.reduce(jnp.add, [x[g * PACK16:(g + 1) * PACK16] for g in range(rows // PACK16)])
        acc16_shape = (PACK16, cols)
        total = lambda cnt: jnp.sum(cnt.astype(F32), axis=0, keepdims=True)

    def over_chunks(body, init):
        if isinstance(nch, int):
            return lax.fori_loop(0, nch, body, init, unroll=unroll)
        carry = lax.fori_loop(0, nch // 2, lambda j, carry: body(2 * j + 1, body(2 * j, carry)), init)
        return lax.cond(nch % 2 == 1, lambda carry: body(nch - 1, carry), lambda carry: carry, carry)

    def count(pred_fn):
        def body(c, cnt):
            return cnt + fold(jnp.where(pred_fn(sk_ref[c]), 1, 0))
        return total(over_chunks(body, jnp.zeros(acc_shape, I32)))

    def count16(ref, pred_fn):
        def body(c, cnt):
            return cnt + fold16(jnp.where(pred_fn(ref[c]), jnp.int16(1), jnp.int16(0)))
        return total(over_chunks(body, jnp.zeros(acc16_shape, I16)))

    def broadcast16(v):
        return jnp.broadcast_to(v.astype(I16), (rows, cols))

    def search16(ref, wanted):
        def bit_body(it, ubits):
            cand = ubits | lax.shift_left(jnp.int32(1), 15 - it)
            cand_b = broadcast16(cand - HALF16)
            tot = count16(ref, lambda s: s >= cand_b)
            return jnp.where(tot >= wanted, cand, ubits)
        return lax.fori_loop(0, 16, bit_body, jnp.zeros(vec, I32))

    hi = search16(hi_ref, topk) - HALF16
    hi_b = broadcast16(hi)
    wanted_lo = topk - count16(hi_ref, lambda s: s > hi_b)

    def low_halves(c, carry):
        low = ((sk_ref[c] & jnp.int32(0xFFFF)) - HALF16).astype(I16)
        lo_ref[c] = jnp.where(hi_ref[c] == hi_b, low, jnp.int16(-HALF16))
        return carry

    lax.fori_loop(0, nch, low_halves, 0)
    thr = hi * (2 * HALF16) + search16(lo_ref, wanted_lo)
    thr_b = jnp.broadcast_to(thr, (rows, cols))
    need_b = jnp.broadcast_to(topk - count(lambda s: s > thr_b), (rows, cols))
    surplus = jnp.where(thr == jnp.int32(INT_MIN), 0.0, count(lambda s: s >= thr_b) - topk)
    return thr_b, need_b, jnp.max(surplus) > 0.0


def _threshold_bias(skc, thr_b):
    return jnp.where((skc >= thr_b) & (skc > jnp.int32(INT_MIN)), 0.0, MASKED)


def _select_bias(skc, thr_b, need_b, eq_before, tri, keys_on_lanes):
    rows, cols = skc.shape
    eq = skc == thr_b
    eqf = jnp.where(eq, 1.0, 0.0).astype(BF16)
    if keys_on_lanes:
        res = _dot(eqf, tri)
        prefix, chunk_total = res[:, :cols], res[:, cols:]
    else:
        res = _dot(tri, eqf)
        prefix, chunk_total = res[:rows], res[rows:]
    keep = ((skc > thr_b) | (eq & (prefix + eq_before <= need_b))) & (skc > jnp.int32(INT_MIN))
    return jnp.where(keep, 0.0, MASKED), eq_before + chunk_total


def _tri_matrix(n, keys_on_lanes):
    r = np.arange(n)
    ones = np.ones((n, n), np.float32)
    if keys_on_lanes:
        return jnp.asarray(np.concatenate([(r[:, None] <= r[None, :]).astype(np.float32), ones], axis=1), BF16)
    return jnp.asarray(np.concatenate([(r[:, None] >= r[None, :]).astype(np.float32), ones], axis=0), BF16)


def _dsa_prompt_kernel(iq_ref, iwt_ref, q_ref, ikb_ref, kb_ref, vt_ref, tri_ref, a_ref,
                       sk_ref, hi_ref, lo_ref, m_ref, l_ref, acc_ref, *, topk):
    i = pl.program_id(1)
    nch = i + 1
    iq = iq_ref[0].reshape(N_IDX_HEADS * BLK, LANES)
    iwt = iwt_ref[...]
    key_id = lax.broadcasted_iota(I32, (BLK, BLK), 0)
    q_id = lax.broadcasted_iota(I32, (BLK, BLK), 1)

    def score_chunk(c, carry):
        s = _dot_nt(ikb_ref[0, c], iq)
        score = jnp.zeros((BLK, BLK), F32)
        for h in range(N_IDX_HEADS):
            score = score + jnp.maximum(s[:, h * BLK:(h + 1) * BLK], 0.0) * iwt[h:h + 1, :]
        key = jnp.where(key_id > q_id + jnp.where(c == i, 0, BLK), jnp.int32(INT_MIN), _sort_key(score))
        sk_ref[c] = key
        hi_ref[c] = (key >> 16).astype(I16)
        return carry

    lax.fori_loop(0, nch, score_chunk, 0)
    thr_b, need_b, any_tie = _kth_largest(sk_ref, hi_ref, lo_ref, nch, topk, keys_on_lanes=False)

    m_ref[...] = jnp.full(m_ref.shape, MASKED, F32)
    l_ref[...] = jnp.zeros(l_ref.shape, F32)
    acc_ref[...] = jnp.zeros(acc_ref.shape, F32)
    heads_per_group = N_HEADS // N_KV_HEADS

    def attend_chunk(c, bias):
        kc = kb_ref[0, c]
        heads = range(N_HEADS)
        cols = [slice(h * BLK, (h + 1) * BLK) for h in heads]
        lgs = [_dot_nt(kc, q_ref[0, h]) + bias for h in heads]
        m_prev = [m_ref[:, cs] for cs in cols]
        m_new = [jnp.maximum(m_prev[h], jnp.max(lgs[h], axis=0, keepdims=True)) for h in heads]
        alpha = [jnp.exp2(m_prev[h] - m_new[h]) for h in heads]
        ps = [jnp.exp2(lgs[h] - m_new[h]) for h in heads]
        for h in heads:
            l_ref[:, cols[h]] = alpha[h] * l_ref[:, cols[h]] + jnp.sum(ps[h], axis=0, keepdims=True)
            m_ref[:, cols[h]] = m_new[h]
        for h in heads:
            g = h // heads_per_group
            vg = vt_ref[0, c, g * HEAD_DIM:(g + 1) * HEAD_DIM, :]
            acc_ref[:, cols[h]] = alpha[h] * acc_ref[:, cols[h]] + _dot(vg, ps[h].astype(BF16))

    @pl.when(any_tie)
    def _():
        tri = tri_ref[...]

        def body(c, eq_before):
            bias, eq_after = _select_bias(sk_ref[c], thr_b, need_b, eq_before, tri, keys_on_lanes=False)
            attend_chunk(c, bias)
            return eq_after

        lax.fori_loop(0, nch, body, jnp.zeros((BLK, BLK), F32))

    @pl.when(jnp.logical_not(any_tie))
    def _():
        def body(c, carry):
            attend_chunk(c, _threshold_bias(sk_ref[c], thr_b))
            return carry

        lax.fori_loop(0, nch, body, 0)

    a_t = acc_ref[...] / l_ref[...]
    a_t = jnp.concatenate([a_t[:, h * BLK:(h + 1) * BLK] for h in range(N_HEADS)], axis=0)
    a_ref[0] = a_t.T.astype(BF16)


def _dsa_prompt(iq, iwt, qbd, ikb, kb, vt, tri, n, t, topk):
    nb = t // BLK
    ikb4 = ikb.reshape(n, nb, BLK, LANES)
    kb4 = kb.reshape(n, nb, BLK, ATT_KV)
    vt4 = vt.reshape(n, nb, ATT_KV, BLK)
    cols = N_HEADS * BLK
    return pl.pallas_call(
        functools.partial(_dsa_prompt_kernel, topk=topk),
        grid=(n, nb),
        in_specs=[
            pl.BlockSpec((1, N_IDX_HEADS, BLK, LANES), lambda b, i: (b * nb + i, 0, 0, 0)),
            pl.BlockSpec((N_IDX_HEADS, BLK), lambda b, i: (0, b * nb + i)),
            pl.BlockSpec((1, N_HEADS, BLK, ATT_KV), lambda b, i: (b * nb + i, 0, 0, 0)),
            pl.BlockSpec((1, nb, BLK, LANES), lambda b, i: (b, 0, 0, 0)),
            pl.BlockSpec((1, nb, BLK, ATT_KV), lambda b, i: (b, 0, 0, 0)),
            pl.BlockSpec((1, nb, ATT_KV, BLK), lambda b, i: (b, 0, 0, 0)),
            pl.BlockSpec(tri.shape, lambda b, i: (0, 0)),
        ],
        out_specs=pl.BlockSpec((1, BLK, ATT_Q), lambda b, i: (b * nb + i, 0, 0)),
        out_shape=jax.ShapeDtypeStruct((n * nb, BLK, ATT_Q), BF16),
        scratch_shapes=[
            pltpu.VMEM((nb, BLK, BLK), I32),
            pltpu.VMEM((nb, BLK, BLK), I16),
            pltpu.VMEM((nb, BLK, BLK), I16),
            pltpu.VMEM((1, cols), F32),
            pltpu.VMEM((1, cols), F32),
            pltpu.VMEM((HEAD_DIM, cols), F32),
        ],
        compiler_params=_cparams(("parallel", "arbitrary")),
        name="dsa_prompt",
    )(iq, iwt, qbd, ikb4, kb4, vt4, tri)


def _sample_scores_kernel(pt_ref, iq_ref, wb_ref, *refs, pages_per_step):
    del pt_ref
    page_refs = refs[:pages_per_step]
    sc_ref = refs[pages_per_step]
    iq = iq_ref[0]
    wb = wb_ref[0]
    s_q = iq.shape[0] // N_IDX_HEADS
    for r in range(pages_per_step):
        s = _dot(iq, page_refs[r][0].astype(BF16))
        t = jnp.maximum(s, 0.0) * wb
        sc_ref[0, r] = jnp.sum(t.reshape(N_IDX_HEADS, s_q, LANES), axis=0)


def _sample_scores(page_table, iq_s, wb_s, cache_ikt, layer_off, pages_per_step):
    ns, npages = page_table.shape
    rows = iq_s.shape[1]
    s_q = rows // N_IDX_HEADS
    steps = npages // pages_per_step

    def page_spec(r):
        return pl.BlockSpec((1, IDX_DIM, PAGE_SIZE),
                            lambda b, j, pt: (layer_off + pt[b, j * pages_per_step + r], 0, 0))

    grid_spec = pltpu.PrefetchScalarGridSpec(
        num_scalar_prefetch=1,
        grid=(ns, steps),
        in_specs=[pl.BlockSpec((1, rows, IDX_DIM), lambda b, j, pt: (b, 0, 0)),
                  pl.BlockSpec((1, rows, LANES), lambda b, j, pt: (b, 0, 0))]
                 + [page_spec(r) for r in range(pages_per_step)],
        out_specs=pl.BlockSpec((1, pages_per_step, s_q, LANES), lambda b, j, pt: (b, j, 0, 0)),
    )
    return pl.pallas_call(
        functools.partial(_sample_scores_kernel, pages_per_step=pages_per_step),
        grid_spec=grid_spec,
        out_shape=jax.ShapeDtypeStruct((ns, npages, s_q, LANES), F32),
        compiler_params=_cparams(("parallel", "arbitrary")),
        name="sample_scores",
    )(page_table, iq_s, wb_s, *([cache_ikt] * pages_per_step))


def _sample_select_kernel(sc_ref, iq_ref, wb_ref, ikn_ref, tri_ref, bias_ref, sk_ref, hi_ref, lo_ref,
                          *, topk, s_q):
    sb = sc_ref.shape[0]
    npages = sc_ref.shape[1]
    rows = sb * s_q

    def key_chunk(c, carry):
        key = _sort_key(sc_ref[:, c].reshape(rows, LANES))
        sk_ref[c] = key
        hi_ref[c] = (key >> 16).astype(I16)
        return carry

    lax.fori_loop(0, npages, key_chunk, 0)
    row_id = lax.broadcasted_iota(I32, (s_q, LANES), 0)
    col_id = lax.broadcasted_iota(I32, (s_q, LANES), 1)
    new_keys = []
    for b in range(sb):
        s = _dot_nt(iq_ref[b], ikn_ref[b])
        t = jnp.maximum(s, 0.0) * wb_ref[b]
        score = jnp.sum(t.reshape(N_IDX_HEADS, s_q, LANES), axis=0)
        new_keys.append(jnp.where(col_id <= row_id, _sort_key(score), jnp.int32(INT_MIN)))
    key = jnp.concatenate(new_keys, axis=0)
    sk_ref[npages] = key
    hi_ref[npages] = (key >> 16).astype(I16)
    nch = npages + 1
    thr_b, need_b, any_tie = _kth_largest(sk_ref, hi_ref, lo_ref, nch, topk, keys_on_lanes=True, unroll=4)

    @pl.when(any_tie)
    def _():
        tri = tri_ref[...]

        def body(c, eq_before):
            bias, eq_after = _select_bias(sk_ref[c], thr_b, need_b, eq_before, tri, keys_on_lanes=True)
            bias_ref[:, c] = bias.reshape(sb, s_q, LANES)
            return eq_after

        lax.fori_loop(0, nch, body, jnp.zeros((rows, LANES), F32))

    @pl.when(jnp.logical_not(any_tie))
    def _():
        def body(c, carry):
            bias_ref[:, c] = _threshold_bias(sk_ref[c], thr_b).reshape(sb, s_q, LANES)
            return carry

        lax.fori_loop(0, nch, body, 0, unroll=4)


def _sample_select(scores, iq_s, wb_s, ikn, tri, topk, sb):
    ns, npages, s_q, _ = scores.shape
    rows = iq_s.shape[1]
    return pl.pallas_call(
        functools.partial(_sample_select_kernel, topk=topk, s_q=s_q),
        grid=(ns // sb,),
        in_specs=[
            pl.BlockSpec((sb, npages, s_q, LANES), lambda g: (g, 0, 0, 0)),
            pl.BlockSpec((sb, rows, IDX_DIM), lambda g: (g, 0, 0)),
            pl.BlockSpec((sb, rows, LANES), lambda g: (g, 0, 0)),
            pl.BlockSpec((sb, PAGE_SIZE, IDX_DIM), lambda g: (g, 0, 0)),
            pl.BlockSpec(tri.shape, lambda g: (0, 0)),
        ],
        out_specs=pl.BlockSpec((sb, npages + 1, s_q, LANES), lambda g: (g, 0, 0, 0)),
        out_shape=jax.ShapeDtypeStruct((ns, npages + 1, s_q, LANES), F32),
        scratch_shapes=[pltpu.VMEM((npages + 1, sb * s_q, LANES), dt) for dt in (I32, I16, I16)],
        compiler_params=_cparams(("parallel",)),
        name="sample_select",
    )(scores, iq_s, wb_s, ikn, tri)


def _sample_attend_kernel(pt_ref, q_ref, bias_ref, bias_new_ref, kn_ref, vn_ref, *refs, pages_per_step, s_q):
    del pt_ref
    k_refs = refs[:pages_per_step]
    v_refs = refs[pages_per_step:2 * pages_per_step]
    a_ref, m_ref, l_ref, acc_ref = refs[2 * pages_per_step:]
    j = pl.program_id(1)
    rows = N_HEADS * s_q

    @pl.when(j == 0)
    def _():
        m_ref[...] = jnp.full(m_ref.shape, MASKED, F32)
        l_ref[...] = jnp.zeros(l_ref.shape, F32)
        acc_ref[...] = jnp.zeros(acc_ref.shape, F32)

    q = q_ref[0]

    def masked(lg, bias):
        return (lg.reshape(N_HEADS, s_q, LANES) + bias[None]).reshape(rows, LANES)

    def update(lgs, weighted_values):
        top = functools.reduce(jnp.maximum, lgs)
        m_prev = m_ref[...]
        m_new = jnp.maximum(m_prev, jnp.max(top, axis=1, keepdims=True))
        alpha = jnp.exp2(m_prev - m_new)
        ps = [jnp.exp2(lg - m_new) for lg in lgs]
        l_ref[...] = alpha * l_ref[...] + jnp.sum(functools.reduce(jnp.add, ps), axis=1, keepdims=True)
        m_ref[...] = m_new
        acc_ref[...] = alpha * acc_ref[...] + weighted_values([p.astype(BF16) for p in ps])

    lgs = [masked(_dot(q, k_refs[r][0].astype(BF16)), bias_ref[0, r]) for r in range(pages_per_step)]
    update(lgs, lambda ps: functools.reduce(
        jnp.add, [_dot_nt(p, v_refs[r][0].astype(BF16)) for r, p in enumerate(ps)]))

    @pl.when(j == pl.num_programs(1) - 1)
    def _():
        lg = masked(_dot_nt(q, kn_ref[0].astype(BF16)), bias_new_ref[0, 0])
        update([lg], lambda ps: _dot(ps[0], vn_ref[0].astype(BF16)))
        a_ref[0] = (acc_ref[...] / l_ref[...]).astype(BF16)


def _sample_attend(page_table, q_s, bias, kn, vn, cache_kt, cache_vt, layer_off, pages_per_step):
    ns, npages = page_table.shape
    rows = q_s.shape[1]
    s_q = rows // N_HEADS
    steps = npages // pages_per_step

    def page_spec(r):
        return pl.BlockSpec((1, ATT_KV, PAGE_SIZE),
                            lambda b, j, pt: (layer_off + pt[b, j * pages_per_step + r], 0, 0))

    new_spec = pl.BlockSpec((1, PAGE_SIZE, ATT_KV), lambda b, j, pt: (b, 0, 0))
    grid_spec = pltpu.PrefetchScalarGridSpec(
        num_scalar_prefetch=1,
        grid=(ns, steps),
        in_specs=[pl.BlockSpec((1, rows, ATT_KV), lambda b, j, pt: (b, 0, 0)),
                  pl.BlockSpec((1, pages_per_step, s_q, LANES), lambda b, j, pt: (b, j, 0, 0)),
                  pl.BlockSpec((1, 1, s_q, LANES), lambda b, j, pt: (b, npages, 0, 0)),
                  new_spec, new_spec]
                 + [page_spec(r) for r in range(pages_per_step)] * 2,
        out_specs=pl.BlockSpec((1, rows, ATT_KV), lambda b, j, pt: (b, 0, 0)),
        scratch_shapes=[
            pltpu.VMEM((rows, 1), F32),
            pltpu.VMEM((rows, 1), F32),
            pltpu.VMEM((rows, ATT_KV), F32),
        ],
    )
    return pl.pallas_call(
        functools.partial(_sample_attend_kernel, pages_per_step=pages_per_step, s_q=s_q),
        grid_spec=grid_spec,
        out_shape=jax.ShapeDtypeStruct((ns, rows, ATT_KV), BF16),
        compiler_params=_cparams(("parallel", "arbitrary")),
        name="sample_attend",
    )(page_table, q_s, bias, bias, kn, vn,
      *([cache_kt] * pages_per_step), *([cache_vt] * pages_per_step))


def _merge_kernel(x_ref, a_ref, d_ref, u_ref, gvn_ref, wc_ref, gb_ref, wg_ref, wba_ref, pbd_ref, psc_ref,
                  wbp_ref, wbg_ref, wo_ref, lng_ref, lnb_ref, rwt_ref, rb_ref,
                  x1_ref, gate_ref, *, alpha):
    tm = x_ref.shape[0]
    x = x_ref[...]
    xb = x.astype(BF16)

    def gate(idx):
        return _sigmoid(_dot(xb, wg_ref[:, idx * D_MODEL:(idx + 1) * D_MODEL]))

    m = gate(0) * _dot(a_ref[...], wba_ref[...])
    y = _dot(d_ref[...].astype(BF16), pbd_ref[...]) * psc_ref[...]
    m = m + gate(1) * _dot(y.astype(BF16), wbp_ref[...])
    gv = gvn_ref[...].astype(BF16)
    grp = lax.broadcasted_iota(I32, (tm, GMLP_WIDTH), 1) // GMLP_GC
    mix = gb_ref[...]
    for g in range(GMLP_GROUPS):
        mix = mix + jnp.where(grp == g, _dot(wc_ref[g], gv), 0.0)
    c = u_ref[...] * mix
    m = m + gate(2) * _dot(c.astype(BF16), wbg_ref[...])
    y = _dot(m.astype(BF16), wo_ref[...])
    x1 = _layer_norm(alpha * x + y, lng_ref[...], lnb_ref[...])
    x1_ref[...] = x1

    scores = _sigmoid(_dot_nt(rwt_ref[...], x1.astype(BF16)))
    sel = scores + rb_ref[...]
    per = N_EXPERTS // N_EXPERT_GROUPS
    g3 = sel.reshape(N_EXPERT_GROUPS, per, tm)
    sub = lax.broadcasted_iota(I32, (N_EXPERT_GROUPS, per, tm), 1)
    m1 = jnp.max(g3, axis=1, keepdims=True)
    first = jnp.min(jnp.where(g3 == m1, sub, per), axis=1, keepdims=True)
    m2 = jnp.max(jnp.where(sub == first, -jnp.inf, g3), axis=1, keepdims=True)
    gs = (m1 + m2).reshape(N_EXPERT_GROUPS, tm)

    def rank_of(vals, count):
        idx = lax.broadcasted_iota(I32, vals.shape, 0)
        rank = jnp.zeros(vals.shape, I32)
        for o in range(count):
            other = vals[o:o + 1]
            beats = (other > vals) | ((other == vals) & (o < idx))
            rank = rank + jnp.where(beats, 1, 0)
        return rank

    gkeep = rank_of(gs, N_EXPERT_GROUPS) < TOPK_GROUPS
    ekeep = jnp.broadcast_to(gkeep.reshape(N_EXPERT_GROUPS, 1, tm), (N_EXPERT_GROUPS, per, tm)).reshape(N_EXPERTS, tm)
    sel = jnp.where(ekeep, sel, -jnp.inf)
    chosen = rank_of(sel, N_EXPERTS) < TOPK_EXPERTS
    wsel = jnp.where(chosen, scores, 0.0)
    gate_t = wsel / jnp.sum(wsel, axis=0, keepdims=True) * ROUTED_SCALE
    gate_ref[...] = gate_t.T


def _merge(x, a_bd, d, u, gvn, wl, tm, alpha):
    m = x.shape[0]
    row = lambda w: pl.BlockSpec((tm, w), lambda i: (i, 0))
    full = lambda a: pl.BlockSpec(a.shape, lambda i: (0,) * a.ndim)
    weights = [wl["wc"], wl["gb"], wl["wg"], wl["wba"], wl["pbd"], wl["psc"], wl["wbp"], wl["wbg"], wl["wo"],
               wl["ln1_g"], wl["ln1_b"], wl["rwt"], wl["rb"]]
    return pl.pallas_call(
        functools.partial(_merge_kernel, alpha=alpha),
        grid=(m // tm,),
        in_specs=[row(D_MODEL), row(ATT_Q), row(POOL_WIDTH), row(GMLP_WIDTH), row(GMLP_WIDTH)]
                 + [full(w) for w in weights],
        out_specs=(row(D_MODEL), row(N_EXPERTS)),
        out_shape=(jax.ShapeDtypeStruct((m, D_MODEL), F32), jax.ShapeDtypeStruct((m, N_EXPERTS), F32)),
        compiler_params=_cparams(("parallel",)),
        name="merge",
    )(x, a_bd, d, u, gvn, *weights)


def _swiglu_act(xb, w_gu):
    h = _dot(xb, w_gu)
    g = h[:, :EXPERT_FF]
    return g * _sigmoid(g) * h[:, EXPERT_FF:]


def _moe_kernel(x_ref, gate_ref, wgu_ref, wd_ref, sgu_ref, sd_ref, lng_ref, lnb_ref, o_ref, xb_ref, acc_ref,
                *, alpha):
    step = pl.program_id(1)

    @pl.when(step == 0)
    def _():
        xb_ref[...] = x_ref[...].astype(BF16)
        acc_ref[...] = _dot(_swiglu_act(xb_ref[...], sgu_ref[...]).astype(BF16), sd_ref[...])

    xb = xb_ref[...]
    gate = gate_ref[...]
    lane = lax.broadcasted_iota(I32, gate.shape, 1)
    acts = []
    for j in range(MOE_EXPERTS_PER_STEP):
        e = step * MOE_EXPERTS_PER_STEP + j
        gcol = jnp.sum(jnp.where(lane == e, gate, 0.0), axis=1, keepdims=True)
        acts.append((gcol * _swiglu_act(xb, wgu_ref[j].astype(BF16))).astype(BF16))
    w_down = wd_ref[...].astype(BF16).reshape(MOE_EXPERTS_PER_STEP * EXPERT_FF, D_MODEL)
    acc_ref[...] += _dot(jnp.concatenate(acts, axis=1), w_down)

    @pl.when(step == pl.num_programs(1) - 1)
    def _():
        o_ref[...] = _layer_norm(alpha * x_ref[...] + acc_ref[...], lng_ref[...], lnb_ref[...])


def _moe(x1, gate, w_gu, w_down, layer, sh_gu, sh_down, ln_g, ln_b, tm, alpha):
    m = x1.shape[0]
    full = lambda a: pl.BlockSpec(a.shape, lambda i, e: (0,) * a.ndim)
    per = MOE_EXPERTS_PER_STEP
    steps = N_EXPERTS // per
    return pl.pallas_call(
        functools.partial(_moe_kernel, alpha=alpha),
        grid=(m // tm, steps),
        in_specs=[
            pl.BlockSpec((tm, D_MODEL), lambda i, e: (i, 0)),
            pl.BlockSpec((tm, N_EXPERTS), lambda i, e: (i, 0)),
            pl.BlockSpec((per, D_MODEL, 2 * EXPERT_FF), lambda i, e: (layer * steps + e, 0, 0)),
            pl.BlockSpec((per, EXPERT_FF, D_MODEL), lambda i, e: (layer * steps + e, 0, 0)),
            full(sh_gu), full(sh_down), full(ln_g), full(ln_b),
        ],
        out_specs=pl.BlockSpec((tm, D_MODEL), lambda i, e: (i, 0)),
        out_shape=jax.ShapeDtypeStruct((m, D_MODEL), F32),
        scratch_shapes=[pltpu.VMEM((tm, D_MODEL), BF16), pltpu.VMEM((tm, D_MODEL), F32)],
        compiler_params=_cparams(("parallel", "arbitrary")),
        name="moe",
    )(x1, gate, w_gu, w_down, sh_gu, sh_down, ln_g, ln_b)


def _rope_tables(pos):
    half = HEAD_DIM // 2
    inv = ROPE_THETA ** (-jnp.arange(half, dtype=F32) / half)
    ang = pos.astype(F32)[:, None] * inv[None, :]
    cos, sin = jnp.cos(ang), jnp.sin(ang)
    cos_t = jnp.tile(cos, (1, LANES // half))
    sin_t = jnp.tile(jnp.concatenate([-sin, sin], axis=1), (1, LANES // HEAD_DIM))
    return cos_t, sin_t


def _block_diag(blocks):
    g, r, c = blocks.shape
    eye = jnp.eye(g, dtype=blocks.dtype)
    return jnp.einsum("grc,gh->grhc", blocks, eye).reshape(g * r, g * c)


def _layer_weights(l, w_in, w_ba, w_bp, w_bg, w_out, pool_w, pool_scale, g_ln_g, g_ln_b, g_ws, g_b,
                   ln1_g, ln1_b, router_w, router_bias, sh_gu, sh_down, ln2_g, ln2_b):
    w = w_in[l]
    sizes = (ATT_Q, ATT_KV, ATT_KV, N_IDX_HEADS * IDX_DIM, IDX_DIM, N_IDX_HEADS,
             POOL_WIDTH, GMLP_WIDTH, GMLP_WIDTH, N_BRANCH * D_MODEL)
    offs = np.concatenate([[0], np.cumsum(sizes)]).tolist()
    wq, wk, wv, wiq, wik, wiw, wp, wu, wgv, wg = [w[:, offs[j]:offs[j + 1]] for j in range(len(sizes))]
    wq = wq * (HEAD_DIM ** -0.5 * LOG2E)
    wikw = jnp.pad(jnp.concatenate([wik, wiw], axis=1), ((0, 0), (0, LANES - IDX_DIM - N_IDX_HEADS)))
    w_cat = jnp.concatenate([wq, wk, wv, wiq, wikw, wp, wu, wgv], axis=1).astype(BF16)
    return dict(
        w_cat=w_cat, wvt=wv.T.astype(BF16), wiwt=wiw.T.astype(BF16), wg=wg.astype(BF16),
        wba=w_ba[l].astype(BF16),
        pbd=_block_diag(pool_w[l]).astype(BF16), psc=pool_scale[l].reshape(1, POOL_WIDTH),
        wbp=w_bp[l].astype(BF16), wbg=w_bg[l].astype(BF16), wo=w_out[l].astype(BF16),
        g_ln_g=g_ln_g[l].reshape(1, GMLP_WIDTH), g_ln_b=g_ln_b[l].reshape(1, GMLP_WIDTH),
        ws=g_ws[l], gbias=g_b[l],
        ln1_g=ln1_g[l].reshape(1, D_MODEL), ln1_b=ln1_b[l].reshape(1, D_MODEL),
        rwt=router_w[l].T.astype(BF16), rbias=router_bias[l],
        sh_gu=sh_gu[l].astype(BF16), sh_down=sh_down[l].astype(BF16),
        ln2_g=ln2_g[l].reshape(1, D_MODEL), ln2_b=ln2_b[l].reshape(1, D_MODEL),
    )


def _chunk_mix(wl, cl, tm):
    tril = jnp.tril(jnp.ones((cl, cl), F32))
    wm = wl["ws"][:, :cl, :cl] * tril
    reps = tm // cl
    wc = jnp.stack([_block_diag(jnp.broadcast_to(wm[g], (reps, cl, cl))) for g in range(GMLP_GROUPS)])
    gb = jnp.tile(jnp.repeat(wl["gbias"][:, :cl].T, GMLP_GC, axis=1), (reps, 1))
    return wc.astype(BF16), gb


def _to_rows_by_seq(a, ns, s_q):
    nblk, heads, _, width = a.shape
    per = BLK // s_q
    a = a.reshape(nblk, heads, per, s_q, width).transpose(0, 2, 1, 3, 4)
    return a.reshape(ns, heads * s_q, width)


def kernel(x_prompt, x_sample, cache_k, cache_v, cache_idx_k, state_pool, page_table, w_in, w_branch_attn,
           w_branch_pool, w_branch_gmlp, w_out, pool_w, pool_scale, gmlp_ln_g, gmlp_ln_b, gmlp_ws, gmlp_b,
           ln1_g, ln1_b, router_w, router_bias, expert_w_gu, expert_w_down, shared_w_gu, shared_w_down,
           ln2_g, ln2_b):
    n_p, t_p, _ = x_prompt.shape
    n_s, t_s, _ = x_sample.shape
    depth = w_in.shape[0]
    n_pool = cache_k.shape[1]
    npages = page_table.shape[1]
    past = npages * PAGE_SIZE
    m_p, m_s = n_p * t_p, n_s * t_s
    alpha = (2 * depth) ** 0.25
    assert t_p % BLK == 0 and m_s % BLK == 0 and BLK % t_s == 0 and t_s <= 16

    tm_p = BLK
    tm_s = BLK
    tm_moe = 1024 if m_p % 1024 == 0 else tm_p
    tp_pool = 512 if t_p % 512 == 0 else BLK
    pages_per_step = 16 if npages % 16 == 0 else 1
    score_pages_per_step = 32 if npages % 32 == 0 else pages_per_step
    sel_batch = 8 if n_s % 8 == 0 else 1
    topk_p = min(TOPK_MAX, t_p // 4)
    topk_s = min(TOPK_MAX, (past + t_s) // 4)

    cos_p, sin_p = _rope_tables(jnp.arange(t_p, dtype=I32))
    cos_s, sin_s = _rope_tables(past + jnp.arange(t_s, dtype=I32))
    cos_s, sin_s = jnp.tile(cos_s, (m_s // t_s, 1)), jnp.tile(sin_s, (m_s // t_s, 1))
    tri_p = _tri_matrix(BLK, keys_on_lanes=False)
    tri_s = _tri_matrix(LANES, keys_on_lanes=True)
    cache_kt = cache_k.transpose(0, 1, 3, 4, 2).reshape(depth * n_pool, ATT_KV, PAGE_SIZE)
    cache_vt = cache_v.transpose(0, 1, 3, 4, 2).reshape(depth * n_pool, ATT_KV, PAGE_SIZE)
    cache_ikt = cache_idx_k.transpose(0, 1, 3, 2).reshape(depth * n_pool, IDX_DIM, PAGE_SIZE)

    w_gu_all = expert_w_gu.reshape(depth * N_EXPERTS, D_MODEL, 2 * EXPERT_FF)
    w_down_all = expert_w_down.reshape(depth * N_EXPERTS, EXPERT_FF, D_MODEL)

    hp = x_prompt.reshape(m_p, D_MODEL)
    hs = x_sample.reshape(m_s, D_MODEL)
    outs = {name: [] for name in ("kp", "vp", "ikp", "pp", "ks", "vs", "iks", "ps", "gs")}
    for l in range(depth):
        wl = _layer_weights(l, w_in, w_branch_attn, w_branch_pool, w_branch_gmlp, w_out, pool_w, pool_scale,
                            gmlp_ln_g, gmlp_ln_b, gmlp_ws, gmlp_b, ln1_g, ln1_b, router_w, router_bias,
                            shared_w_gu, shared_w_down, ln2_g, ln2_b)

        def finish(x, a_bd, d, u, gvn, cl, tm, tm_e):
            wc, gb = _chunk_mix(wl, cl, tm)
            wm = dict(wl, wc=wc, gb=gb, rb=jnp.broadcast_to(wl["rbias"][:, None], (N_EXPERTS, tm)))
            x1, gate = _merge(x, a_bd, d, u, gvn, wm, tm, alpha)
            return _moe(x1, gate, w_gu_all, w_down_all, l, wl["sh_gu"], wl["sh_down"],
                        wl["ln2_g"], wl["ln2_b"], tm_e, alpha)

        qbd, k, v, kb, vt, iq, ikw, ikb, iwt, p, u, gvn = _proj(hp, wl, cos_p, sin_p, tm_p)
        p3 = p.reshape(n_p, t_p, POOL_WIDTH)
        d = _pool(p3, None, 0, tp_pool).reshape(m_p, POOL_WIDTH)
        a = _dsa_prompt(iq, iwt, qbd, ikb, kb, vt, tri_p, n_p, t_p, topk_p).reshape(m_p, ATT_Q)
        hp = finish(hp, a, d, u, gvn, CHUNK, tm_p, tm_moe)
        outs["kp"].append(k.reshape(n_p, t_p, N_KV_HEADS, HEAD_DIM))
        outs["vp"].append(v.reshape(n_p, t_p, N_KV_HEADS, HEAD_DIM))
        outs["ikp"].append(ikw[:, :IDX_DIM].reshape(n_p, t_p, IDX_DIM))
        outs["pp"].append(p3[:, t_p - POOL_STATE:])

        qbd, k, v, kb, vt, iq, ikw, ikb, iwt, p, u, gvn = _proj(hs, wl, cos_s, sin_s, tm_s)
        p3 = p.reshape(n_s, t_s, POOL_WIDTH)
        prefix16 = jnp.pad(state_pool[l], ((0, 0), (16 - POOL_STATE, 0), (0, 0)))
        d = _pool(p3, prefix16, past, t_s).reshape(m_s, POOL_WIDTH)
        iq_s = _to_rows_by_seq(iq, n_s, t_s)[:, :, :IDX_DIM]
        q_s = _to_rows_by_seq(qbd, n_s, t_s)
        wb_s = jnp.broadcast_to(iwt.reshape(N_IDX_HEADS, n_s, t_s).transpose(1, 0, 2).reshape(n_s, -1, 1),
                                (n_s, N_IDX_HEADS * t_s, LANES))
        pad_rows = lambda a: jnp.pad(a.reshape(n_s, t_s, -1), ((0, 0), (0, PAGE_SIZE - t_s), (0, 0)))
        ikn = pad_rows(ikb[:, :IDX_DIM])
        scores = _sample_scores(page_table, iq_s, wb_s, cache_ikt, l * n_pool, score_pages_per_step)
        bias = _sample_select(scores, iq_s, wb_s, ikn, tri_s, topk_s, sel_batch)
        a_s = _sample_attend(page_table, q_s, bias, pad_rows(k), pad_rows(v), cache_kt, cache_vt,
                             l * n_pool, pages_per_step)
        a_s = a_s.reshape(n_s, N_HEADS, t_s, N_KV_HEADS, HEAD_DIM)
        a_s = jnp.stack([a_s[:, h, :, h // (N_HEADS // N_KV_HEADS)] for h in range(N_HEADS)], axis=2)
        hs = finish(hs, a_s.reshape(m_s, ATT_Q), d, u, gvn, t_s, tm_s, tm_s)
        outs["ks"].append(k.reshape(n_s, t_s, N_KV_HEADS, HEAD_DIM))
        outs["vs"].append(v.reshape(n_s, t_s, N_KV_HEADS, HEAD_DIM))
        outs["iks"].append(ikw[:, :IDX_DIM].reshape(n_s, t_s, IDX_DIM))
        outs["ps"].append(jnp.concatenate([state_pool[l], p3], axis=1)[:, -POOL_STATE:])
        outs["gs"].append(gvn.reshape(n_s, t_s, GMLP_WIDTH))

    st = lambda name: jnp.stack(outs[name])
    return (hp.reshape(n_p, t_p, D_MODEL), hs.reshape(n_s, t_s, D_MODEL),
            st("kp"), st("vp"), st("ikp"), st("ks"), st("vs"), st("iks"), st("pp"), st("ps"), st("gs"))
```

```python
import functools

import jax
import jax.numpy as jnp
import numpy as np
from jax import lax
from jax.experimental import pallas as pl
from jax.experimental.pallas import tpu as pltpu

F32 = jnp.float32
BF16 = jnp.bfloat16
I32 = jnp.int32
I16 = jnp.int16

D_MODEL = 1024
N_HEADS = 8
N_KV_HEADS = 4
HEAD_DIM = 64
N_IDX_HEADS = 8
IDX_DIM = 64
TOPK_MAX = 256
PAGE_SIZE = 128
ROPE_THETA = 10000.0
POOL_WINDOWS = (2, 4, 8, 16)
POOL_WIDTH = 256
POOL_GC = 64
POOL_STATE = 15
GMLP_WIDTH = 256
GMLP_GROUPS = 4
GMLP_GC = 64
CHUNK = 128
N_BRANCH = 3
ATT_Q = N_HEADS * HEAD_DIM
ATT_KV = N_KV_HEADS * HEAD_DIM
N_EXPERTS = 64
TOPK_EXPERTS = 8
N_EXPERT_GROUPS = 8
TOPK_GROUPS = 4
EXPERT_FF = 256
ROUTED_SCALE = 2.5
LN_EPS = 1e-5

LANES = 128
SUBLANES = 8
PACK16 = 16
HALF16 = 2 ** 15
BLK = 256
MOE_EXPERTS_PER_STEP = 2
MOE_ROW_CHUNK = 1024
INT_MIN = -2 ** 31
MASKED = -1e30
LOG2E = 1.4426950408889634
VMEM_LIMIT = 56 * 1024 * 1024

C_Q = 0
C_K = C_Q + ATT_Q
C_V = C_K + ATT_KV
C_IQ = C_V + ATT_KV
C_IKW = C_IQ + N_IDX_HEADS * IDX_DIM
C_P = C_IKW + LANES
C_U = C_P + POOL_WIDTH
C_GV = C_U + GMLP_WIDTH
C_END = C_GV + GMLP_WIDTH


def _cparams(sem):
    return pltpu.CompilerParams(dimension_semantics=sem, vmem_limit_bytes=VMEM_LIMIT)


def _layer_norm(x, g, b):
    mu = jnp.mean(x, axis=-1, keepdims=True)
    xc = x - mu
    var = jnp.mean(xc * xc, axis=-1, keepdims=True)
    return xc * lax.rsqrt(var + LN_EPS) * g + b


def _sigmoid(x):
    return 1.0 / (1.0 + jnp.exp(-x))


def _dot(a, b):
    return jnp.dot(a, b, preferred_element_type=F32)


def _dot_nt(a, b):
    return lax.dot_general(a, b, (((1,), (1,)), ((), ())), preferred_element_type=F32)


def _sort_key(score):
    score = jnp.where(score == 0.0, 0.0, score)
    bits = lax.bitcast_convert_type(score, I32)
    return bits ^ ((bits >> 31) & jnp.int32(0x7FFFFFFF))


def _proj_kernel(x_ref, w_ref, wvt_ref, wiwt_ref, cos_ref, sin_ref, lng_ref, lnb_ref,
                 q_ref, k_ref, v_ref, kb_ref, vt_ref, iq_ref, ikw_ref, ikb_ref, iwt_ref,
                 p_ref, u_ref, gvn_ref):
    tm = x_ref.shape[0]
    xb = x_ref[...].astype(BF16)
    cos = cos_ref[...]
    sin = sin_ref[...]
    lane = lax.broadcasted_iota(I32, (tm, LANES), 1)
    first_half = (lane % HEAD_DIM) < (HEAD_DIM // 2)

    def mm(c0, width):
        return _dot(xb, w_ref[:, c0:c0 + width])

    def rope(z):
        partner = jnp.where(first_half, pltpu.roll(z, LANES - 32, 1), pltpu.roll(z, 32, 1))
        return z * cos + partner * sin

    def rope_wide(z):
        return jnp.concatenate([rope(z[:, s * LANES:(s + 1) * LANES]) for s in range(z.shape[1] // LANES)], axis=1)

    nblk = tm // BLK
    q_ref[...] = rope_wide(mm(C_Q, ATT_Q)).astype(BF16)
    k = rope_wide(mm(C_K, ATT_KV))
    k_ref[...] = k
    kb_ref[...] = k.astype(BF16)
    v_ref[...] = mm(C_V, ATT_KV)
    vt = _dot_nt(wvt_ref[...], xb).astype(BF16)
    for b in range(nblk):
        vt_ref[b] = vt[:, b * BLK:(b + 1) * BLK]
    iq_ref[...] = rope_wide(mm(C_IQ, N_IDX_HEADS * IDX_DIM)).astype(BF16)
    z = mm(C_IKW, LANES)
    is_key = lane < IDX_DIM
    ikw = jnp.where(is_key, rope(z), z)
    ikw_ref[...] = ikw
    ikb_ref[...] = jnp.where(is_key, ikw, 0.0).astype(BF16)
    iwt_ref[...] = _dot_nt(wiwt_ref[...], xb)
    p_ref[...] = mm(C_P, POOL_WIDTH)
    u_ref[...] = mm(C_U, GMLP_WIDTH)
    gvn_ref[...] = _layer_norm(mm(C_GV, GMLP_WIDTH), lng_ref[...], lnb_ref[...])


def _proj(x, wl, cos_t, sin_t, tm):
    m = x.shape[0]
    nt = cos_t.shape[0] // tm
    row = lambda w: pl.BlockSpec((tm, w), lambda i: (i, 0))
    full = lambda a: pl.BlockSpec(a.shape, lambda i: (0,) * a.ndim)
    tab = pl.BlockSpec((tm, LANES), lambda i: (i % nt, 0))
    nb = tm // BLK
    out_shape = (
        jax.ShapeDtypeStruct((m, ATT_Q), BF16),
        jax.ShapeDtypeStruct((m, ATT_KV), F32),
        jax.ShapeDtypeStruct((m, ATT_KV), F32),
        jax.ShapeDtypeStruct((m, ATT_KV), BF16),
        jax.ShapeDtypeStruct((m // BLK, ATT_KV, BLK), BF16),
        jax.ShapeDtypeStruct((m, N_IDX_HEADS * IDX_DIM), BF16),
        jax.ShapeDtypeStruct((m, LANES), F32),
        jax.ShapeDtypeStruct((m, LANES), BF16),
        jax.ShapeDtypeStruct((N_IDX_HEADS, m), F32),
        jax.ShapeDtypeStruct((m, POOL_WIDTH), F32),
        jax.ShapeDtypeStruct((m, GMLP_WIDTH), F32),
        jax.ShapeDtypeStruct((m, GMLP_WIDTH), F32),
    )
    out_specs = (
        row(ATT_Q),
        row(ATT_KV), row(ATT_KV), row(ATT_KV),
        pl.BlockSpec((nb, ATT_KV, BLK), lambda i: (i, 0, 0)),
        row(N_IDX_HEADS * IDX_DIM),
        row(LANES), row(LANES),
        pl.BlockSpec((N_IDX_HEADS, tm), lambda i: (0, i)),
        row(POOL_WIDTH), row(GMLP_WIDTH), row(GMLP_WIDTH),
    )
    weights = [wl["w_cat"], wl["wvt"], wl["wiwt"]]
    return pl.pallas_call(
        _proj_kernel,
        grid=(m // tm,),
        in_specs=[row(D_MODEL)] + [full(w) for w in weights] + [tab, tab, full(wl["g_ln_g"]), full(wl["g_ln_b"])],
        out_specs=out_specs,
        out_shape=out_shape,
        compiler_params=_cparams(("parallel",)),
        name="proj",
    )(x, *weights, cos_t, sin_t, wl["g_ln_g"], wl["g_ln_b"])


def _pool_kernel(p_ref, halo_ref, d_ref, ext_ref, s2_ref, s4_ref, s8_ref, *, pos_base, halo_is_prefix):
    tp = p_ref.shape[1]
    r_end = tp + 32
    i = pl.program_id(1)
    p = p_ref[0]
    halo = halo_ref[0]
    if not halo_is_prefix:
        halo = jnp.where(i == 0, 0.0, halo)
    ext_ref[0:16, :] = jnp.zeros((16, POOL_WIDTH), F32)
    ext_ref[16:32, :] = halo
    ext_ref[32:r_end, :] = p
    s2_ref[8:r_end, :] = ext_ref[8:r_end, :] + ext_ref[7:r_end - 1, :]
    s4_ref[16:r_end, :] = s2_ref[16:r_end, :] + s2_ref[14:r_end - 2, :]
    s8_ref[24:r_end, :] = s4_ref[24:r_end, :] + s4_ref[20:r_end - 4, :]
    s16 = s8_ref[32:r_end, :] + s8_ref[24:r_end - 8, :]
    lane = lax.broadcasted_iota(I32, (tp, POOL_WIDTH), 1)
    grp = lane // POOL_GC
    win = jnp.where(grp == 0, s2_ref[32:r_end, :],
                    jnp.where(grp == 1, s4_ref[32:r_end, :],
                              jnp.where(grp == 2, s8_ref[32:r_end, :], s16)))
    width = jnp.where(grp == 0, POOL_WINDOWS[0],
                      jnp.where(grp == 1, POOL_WINDOWS[1],
                                jnp.where(grp == 2, POOL_WINDOWS[2], POOL_WINDOWS[3])))
    pos = pos_base + i * tp + lax.broadcasted_iota(I32, (tp, POOL_WIDTH), 0)
    cnt = jnp.minimum(width, pos + 1).astype(F32)
    d_ref[0] = win / cnt - p


def _pool(p3, prefix16, pos_base, tp):
    n, t, _ = p3.shape
    halo_is_prefix = prefix16 is not None
    if halo_is_prefix:
        halo = prefix16
        halo_spec = pl.BlockSpec((1, 16, POOL_WIDTH), lambda b, i: (b, 0, 0))
    else:
        halo = p3
        step = tp // 16
        halo_spec = pl.BlockSpec((1, 16, POOL_WIDTH), lambda b, i: (b, jnp.maximum(i * step - 1, 0), 0))
    rows = tp + 32
    return pl.pallas_call(
        functools.partial(_pool_kernel, pos_base=pos_base, halo_is_prefix=halo_is_prefix),
        grid=(n, t // tp),
        in_specs=[pl.BlockSpec((1, tp, POOL_WIDTH), lambda b, i: (b, i, 0)), halo_spec],
        out_specs=pl.BlockSpec((1, tp, POOL_WIDTH), lambda b, i: (b, i, 0)),
        out_shape=jax.ShapeDtypeStruct((n, t, POOL_WIDTH), F32),
        scratch_shapes=[pltpu.VMEM((rows, POOL_WIDTH), F32) for _ in range(4)],
        compiler_params=_cparams(("parallel", "parallel")),
        name="pool",
    )(p3, halo)


def _kth_largest(sk_ref, hi_ref, lo_ref, nch, topk, keys_on_lanes, unroll=1):
    rows, cols = sk_ref.shape[1:]
    if keys_on_lanes:
        vec = (rows, 1)
        fold, acc_shape = (lambda x: x), (rows, cols)
        fold16, acc16_shape = fold, acc_shape
        total = lambda cnt: jnp.sum(cnt.astype(F32), axis=1, keepdims=True)
    else:
        vec = (1, cols)
        fold, acc_shape = (lambda x: jnp.sum(x.reshape(rows // SUBLANES, SUBLANES, cols), axis=0)), (SUBLANES, cols)
        fold16 = lambda x: functools.reduce(jnp.add, [x[g * PACK16:(g + 1) * PACK16] for g in range(rows // PACK16)])
        acc16_shape = (PACK16, cols)
        total = lambda cnt: jnp.sum(cnt.astype(F32), axis=0, keepdims=True)

    def over_chunks(body, init):
        if isinstance(nch, int):
            return lax.fori_loop(0, nch, body, init, unroll=unroll)
        carry = lax.fori_loop(0, nch // 2, lambda j, carry: body(2 * j + 1, body(2 * j, carry)), init)
        return lax.cond(nch % 2 == 1, lambda carry: body(nch - 1, carry), lambda carry: carry, carry)

    def count(pred_fn):
        def body(c, cnt):
            return cnt + fold(jnp.where(pred_fn(sk_ref[c]), 1, 0))
        return total(over_chunks(body, jnp.zeros(acc_shape, I32)))

    def count16(ref, pred_fn):
        def body(c, cnt):
            return cnt + fold16(jnp.where(pred_fn(ref[c]), jnp.int16(1), jnp.int16(0)))
        return total(over_chunks(body, jnp.zeros(acc16_shape, I16)))

    def broadcast16(v):
        return jnp.broadcast_to(v.astype(I16), (rows, cols))

    def search16(ref, wanted):
        def bit_body(it, ubits):
            cand = ubits | lax.shift_left(jnp.int32(1), 15 - it)
            cand_b = broadcast16(cand - HALF16)
            tot = count16(ref, lambda s: s >= cand_b)
            return jnp.where(tot >= wanted, cand, ubits)
        return lax.fori_loop(0, 16, bit_body, jnp.zeros(vec, I32))

    hi = search16(hi_ref, topk) - HALF16
    hi_b = broadcast16(hi)
    wanted_lo = topk - count16(hi_ref, lambda s: s > hi_b)

    def low_halves(c, carry):
        low = ((sk_ref[c] & jnp.int32(0xFFFF)) - HALF16).astype(I16)
        lo_ref[c] = jnp.where(hi_ref[c] == hi_b, low, jnp.int16(-HALF16))
        return carry

    lax.fori_loop(0, nch, low_halves, 0)
    thr = hi * (2 * HALF16) + search16(lo_ref, wanted_lo)
    thr_b = jnp.broadcast_to(thr, (rows, cols))
    need_b = jnp.broadcast_to(topk - count(lambda s: s > thr_b), (rows, cols))
    surplus = jnp.where(thr == jnp.int32(INT_MIN), 0.0, count(lambda s: s >= thr_b) - topk)
    return thr_b, need_b, jnp.max(surplus) > 0.0


def _threshold_bias(skc, thr_b):
    return jnp.where((skc >= thr_b) & (skc > jnp.int32(INT_MIN)), 0.0, MASKED)


def _select_bias(skc, thr_b, need_b, eq_before, tri, keys_on_lanes):
    rows, cols = skc.shape
    eq = skc == thr_b
    eqf = jnp.where(eq, 1.0, 0.0).astype(BF16)
    if keys_on_lanes:
        res = _dot(eqf, tri)
        prefix, chunk_total = res[:, :cols], res[:, cols:]
    else:
        res = _dot(tri, eqf)
        prefix, chunk_total = res[:rows], res[rows:]
    keep = ((skc > thr_b) | (eq & (prefix + eq_before <= need_b))) & (skc > jnp.int32(INT_MIN))
    return jnp.where(keep, 0.0, MASKED), eq_before + chunk_total


def _tri_matrix(n, keys_on_lanes):
    r = np.arange(n)
    ones = np.ones((n, n), np.float32)
    if keys_on_lanes:
        return jnp.asarray(np.concatenate([(r[:, None] <= r[None, :]).astype(np.float32), ones], axis=1), BF16)
    return jnp.asarray(np.concatenate([(r[:, None] >= r[None, :]).astype(np.float32), ones], axis=0), BF16)


def _pad_heads(q, iq, qpad_ref, iqpad_ref):
    rows = q.shape[0]
    low_half = lax.broadcasted_iota(I32, (rows, LANES), 1) < HEAD_DIM
    zero_slab = jnp.zeros((rows, LANES), F32)

    def head_slabs(z):
        z = z.astype(F32)
        for s in range(z.shape[1] // LANES):
            slab = z[:, s * LANES:(s + 1) * LANES]
            swapped = pltpu.roll(slab, HEAD_DIM, 1)
            for r in range(2):
                yield 2 * s + r, (slab, swapped) if r == 0 else (swapped, slab)

    for h, (head_low, head_high) in head_slabs(q):
        g = h // (N_HEADS // N_KV_HEADS)
        own = jnp.where(low_half, head_low, 0.0) if g % 2 == 0 else jnp.where(low_half, 0.0, head_high)
        qpad_ref[h] = jnp.concatenate([own, zero_slab] if g // 2 == 0 else [zero_slab, own], axis=1).astype(BF16)
    for h, (head_low, _) in head_slabs(iq):
        iqpad_ref[h * rows:(h + 1) * rows, :] = jnp.where(low_half, head_low, 0.0).astype(BF16)


def _dsa_prompt_kernel(iq_ref, iwt_ref, q_ref, ikb_ref, kb_ref, vt_ref, tri_ref, a_ref,
                       sk_ref, hi_ref, lo_ref, m_ref, l_ref, acc_ref, qpad_ref, iqpad_ref, *, topk):
    i = pl.program_id(1)
    nch = i + 1
    _pad_heads(q_ref[0], iq_ref[0], qpad_ref, iqpad_ref)
    iq = iqpad_ref[...]
    iwt = iwt_ref[...]
    key_id = lax.broadcasted_iota(I32, (BLK, BLK), 0)
    q_id = lax.broadcasted_iota(I32, (BLK, BLK), 1)

    def score_chunk(c, carry):
        s = _dot_nt(ikb_ref[0, c], iq)
        score = jnp.zeros((BLK, BLK), F32)
        for h in range(N_IDX_HEADS):
            score = score + jnp.maximum(s[:, h * BLK:(h + 1) * BLK], 0.0) * iwt[h:h + 1, :]
        key = jnp.where(key_id > q_id + jnp.where(c == i, 0, BLK), jnp.int32(INT_MIN), _sort_key(score))
        sk_ref[c] = key
        hi_ref[c] = (key >> 16).astype(I16)
        return carry

    lax.fori_loop(0, nch, score_chunk, 0)
    thr_b, need_b, any_tie = _kth_largest(sk_ref, hi_ref, lo_ref, nch, topk, keys_on_lanes=False)

    m_ref[...] = jnp.full(m_ref.shape, MASKED, F32)
    l_ref[...] = jnp.zeros(l_ref.shape, F32)
    acc_ref[...] = jnp.zeros(acc_ref.shape, F32)
    heads_per_group = N_HEADS // N_KV_HEADS

    def attend_chunk(c, bias):
        kc = kb_ref[0, c]
        heads = range(N_HEADS)
        cols = [slice(h * BLK, (h + 1) * BLK) for h in heads]
        lgs = [_dot_nt(kc, qpad_ref[h]) + bias for h in heads]
        m_prev = [m_ref[:, cs] for cs in cols]
        m_new = [jnp.maximum(m_prev[h], jnp.max(lgs[h], axis=0, keepdims=True)) for h in heads]
        alpha = [jnp.exp2(m_prev[h] - m_new[h]) for h in heads]
        ps = [jnp.exp2(lgs[h] - m_new[h]) for h in heads]
        for h in heads:
            l_ref[:, cols[h]] = alpha[h] * l_ref[:, cols[h]] + jnp.sum(ps[h], axis=0, keepdims=True)
            m_ref[:, cols[h]] = m_new[h]
        for h in heads:
            g = h // heads_per_group
            vg = vt_ref[0, c, g * HEAD_DIM:(g + 1) * HEAD_DIM, :]
            acc_ref[:, cols[h]] = alpha[h] * acc_ref[:, cols[h]] + _dot(vg, ps[h].astype(BF16))

    @pl.when(any_tie)
    def _():
        tri = tri_ref[...]

        def body(c, eq_before):
            bias, eq_after = _select_bias(sk_ref[c], thr_b, need_b, eq_before, tri, keys_on_lanes=False)
            attend_chunk(c, bias)
            return eq_after

        lax.fori_loop(0, nch, body, jnp.zeros((BLK, BLK), F32))

    @pl.when(jnp.logical_not(any_tie))
    def _():
        def body(c, carry):
            attend_chunk(c, _threshold_bias(sk_ref[c], thr_b))
            return carry

        lax.fori_loop(0, nch, body, 0)

    a_t = acc_ref[...] / l_ref[...]
    a_t = jnp.concatenate([a_t[:, h * BLK:(h + 1) * BLK] for h in range(N_HEADS)], axis=0)
    a_ref[0] = a_t.T.astype(BF16)


def _dsa_prompt(iq, iwt, q, ikb, kb, vt, tri, n, t, topk):
    nb = t // BLK
    iq = iq.reshape(n * nb, BLK, N_IDX_HEADS * IDX_DIM)
    q = q.reshape(n * nb, BLK, ATT_Q)
    ikb4 = ikb.reshape(n, nb, BLK, LANES)
    kb4 = kb.reshape(n, nb, BLK, ATT_KV)
    vt4 = vt.reshape(n, nb, ATT_KV, BLK)
    cols = N_HEADS * BLK
    return pl.pallas_call(
        functools.partial(_dsa_prompt_kernel, topk=topk),
        grid=(n, nb),
        in_specs=[
            pl.BlockSpec((1, BLK, N_IDX_HEADS * IDX_DIM), lambda b, i: (b * nb + i, 0, 0)),
            pl.BlockSpec((N_IDX_HEADS, BLK), lambda b, i: (0, b * nb + i)),
            pl.BlockSpec((1, BLK, ATT_Q), lambda b, i: (b * nb + i, 0, 0)),
            pl.BlockSpec((1, nb, BLK, LANES), lambda b, i: (b, 0, 0, 0)),
            pl.BlockSpec((1, nb, BLK, ATT_KV), lambda b, i: (b, 0, 0, 0)),
            pl.BlockSpec((1, nb, ATT_KV, BLK), lambda b, i: (b, 0, 0, 0)),
            pl.BlockSpec(tri.shape, lambda b, i: (0, 0)),
        ],
        out_specs=pl.BlockSpec((1, BLK, ATT_Q), lambda b, i: (b * nb + i, 0, 0)),
        out_shape=jax.ShapeDtypeStruct((n * nb, BLK, ATT_Q), BF16),
        scratch_shapes=[
            pltpu.VMEM((nb, BLK, BLK), I32),
            pltpu.VMEM((nb, BLK, BLK), I16),
            pltpu.VMEM((nb, BLK, BLK), I16),
            pltpu.VMEM((1, cols), F32),
            pltpu.VMEM((1, cols), F32),
            pltpu.VMEM((HEAD_DIM, cols), F32),
            pltpu.VMEM((N_HEADS, BLK, ATT_KV), BF16),
            pltpu.VMEM((N_IDX_HEADS * BLK, LANES), BF16),
        ],
        compiler_params=_cparams(("parallel", "arbitrary")),
        name="dsa_prompt",
    )(iq, iwt, q, ikb4, kb4, vt4, tri)


def _sample_scores_kernel(pt_ref, iq_ref, wb_ref, *refs, pages_per_step):
    del pt_ref
    page_refs = refs[:pages_per_step]
    sc_ref = refs[pages_per_step]
    iq = iq_ref[0]
    wb = wb_ref[0]
    s_q = iq.shape[0] // N_IDX_HEADS
    for r in range(pages_per_step):
        s = _dot(iq, page_refs[r][0].astype(BF16))
        t = jnp.maximum(s, 0.0) * wb
        sc_ref[0, r] = jnp.sum(t.reshape(N_IDX_HEADS, s_q, LANES), axis=0)


def _sample_scores(page_table, iq_s, wb_s, cache_ikt, layer_off, pages_per_step):
    ns, npages = page_table.shape
    rows = iq_s.shape[1]
    s_q = rows // N_IDX_HEADS
    steps = npages // pages_per_step

    def page_spec(r):
        return pl.BlockSpec((1, IDX_DIM, PAGE_SIZE),
                            lambda b, j, pt: (layer_off + pt[b, j * pages_per_step + r], 0, 0))

    grid_spec = pltpu.PrefetchScalarGridSpec(
        num_scalar_prefetch=1,
        grid=(ns, steps),
        in_specs=[pl.BlockSpec((1, rows, IDX_DIM), lambda b, j, pt: (b, 0, 0)),
                  pl.BlockSpec((1, rows, LANES), lambda b, j, pt: (b, 0, 0))]
                 + [page_spec(r) for r in range(pages_per_step)],
        out_specs=pl.BlockSpec((1, pages_per_step, s_q, LANES), lambda b, j, pt: (b, j, 0, 0)),
    )
    return pl.pallas_call(
        functools.partial(_sample_scores_kernel, pages_per_step=pages_per_step),
        grid_spec=grid_spec,
        out_shape=jax.ShapeDtypeStruct((ns, npages, s_q, LANES), F32),
        compiler_params=_cparams(("parallel", "arbitrary")),
        name="sample_scores",
    )(page_table, iq_s, wb_s, *([cache_ikt] * pages_per_step))


def _sample_select_kernel(sc_ref, iq_ref, wb_ref, ikn_ref, tri_ref, bias_ref, sk_ref, hi_ref, lo_ref,
                          *, topk, s_q):
    sb = sc_ref.shape[0]
    npages = sc_ref.shape[1]
    rows = sb * s_q

    def key_chunk(c, carry):
        key = _sort_key(sc_ref[:, c].reshape(rows, LANES))
        sk_ref[c] = key
        hi_ref[c] = (key >> 16).astype(I16)
        return carry

    lax.fori_loop(0, npages, key_chunk, 0)
    row_id = lax.broadcasted_iota(I32, (s_q, LANES), 0)
    col_id = lax.broadcasted_iota(I32, (s_q, LANES), 1)
    new_keys = []
    for b in range(sb):
        s = _dot_nt(iq_ref[b], ikn_ref[b])
        t = jnp.maximum(s, 0.0) * wb_ref[b]
        score = jnp.sum(t.reshape(N_IDX_HEADS, s_q, LANES), axis=0)
        new_keys.append(jnp.where(col_id <= row_id, _sort_key(score), jnp.int32(INT_MIN)))
    key = jnp.concatenate(new_keys, axis=0)
    sk_ref[npages] = key
    hi_ref[npages] = (key >> 16).astype(I16)
    nch = npages + 1
    thr_b, need_b, any_tie = _kth_largest(sk_ref, hi_ref, lo_ref, nch, topk, keys_on_lanes=True, unroll=4)

    @pl.when(any_tie)
    def _():
        tri = tri_ref[...]

        def body(c, eq_before):
            bias, eq_after = _select_bias(sk_ref[c], thr_b, need_b, eq_before, tri, keys_on_lanes=True)
            bias_ref[:, c] = bias.reshape(sb, s_q, LANES)
            return eq_after

        lax.fori_loop(0, nch, body, jnp.zeros((rows, LANES), F32))

    @pl.when(jnp.logical_not(any_tie))
    def _():
        def body(c, carry):
            bias_ref[:, c] = _threshold_bias(sk_ref[c], thr_b).reshape(sb, s_q, LANES)
            return carry

        lax.fori_loop(0, nch, body, 0, unroll=4)


def _sample_select(scores, iq_s, wb_s, ikn, tri, topk, sb):
    ns, npages, s_q, _ = scores.shape
    rows = iq_s.shape[1]
    return pl.pallas_call(
        functools.partial(_sample_select_kernel, topk=topk, s_q=s_q),
        grid=(ns // sb,),
        in_specs=[
            pl.BlockSpec((sb, npages, s_q, LANES), lambda g: (g, 0, 0, 0)),
            pl.BlockSpec((sb, rows, IDX_DIM), lambda g: (g, 0, 0)),
            pl.BlockSpec((sb, rows, LANES), lambda g: (g, 0, 0)),
            pl.BlockSpec((sb, PAGE_SIZE, IDX_DIM), lambda g: (g, 0, 0)),
            pl.BlockSpec(tri.shape, lambda g: (0, 0)),
        ],
        out_specs=pl.BlockSpec((sb, npages + 1, s_q, LANES), lambda g: (g, 0, 0, 0)),
        out_shape=jax.ShapeDtypeStruct((ns, npages + 1, s_q, LANES), F32),
        scratch_shapes=[pltpu.VMEM((npages + 1, sb * s_q, LANES), dt) for dt in (I32, I16, I16)],
        compiler_params=_cparams(("parallel",)),
        name="sample_select",
    )(scores, iq_s, wb_s, ikn, tri)


def _sample_attend_kernel(pt_ref, q_ref, bias_ref, bias_new_ref, kn_ref, vn_ref, *refs, pages_per_step, s_q):
    del pt_ref
    k_refs = refs[:pages_per_step]
    v_refs = refs[pages_per_step:2 * pages_per_step]
    a_ref, m_ref, l_ref, acc_ref = refs[2 * pages_per_step:]
    j = pl.program_id(1)
    rows = N_HEADS * s_q

    @pl.when(j == 0)
    def _():
        m_ref[...] = jnp.full(m_ref.shape, MASKED, F32)
        l_ref[...] = jnp.zeros(l_ref.shape, F32)
        acc_ref[...] = jnp.zeros(acc_ref.shape, F32)

    q = q_ref[0]

    def masked(lg, bias):
        return (lg.reshape(N_HEADS, s_q, LANES) + bias[None]).reshape(rows, LANES)

    def update(lgs, weighted_values):
        top = functools.reduce(jnp.maximum, lgs)
        m_prev = m_ref[...]
        m_new = jnp.maximum(m_prev, jnp.max(top, axis=1, keepdims=True))
        alpha = jnp.exp2(m_prev - m_new)
        ps = [jnp.exp2(lg - m_new) for lg in lgs]
        l_ref[...] = alpha * l_ref[...] + jnp.sum(functools.reduce(jnp.add, ps), axis=1, keepdims=True)
        m_ref[...] = m_new
        acc_ref[...] = alpha * acc_ref[...] + weighted_values([p.astype(BF16) for p in ps])

    lgs = [masked(_dot(q, k_refs[r][0].astype(BF16)), bias_ref[0, r]) for r in range(pages_per_step)]
    update(lgs, lambda ps: functools.reduce(
        jnp.add, [_dot_nt(p, v_refs[r][0].astype(BF16)) for r, p in enumerate(ps)]))

    @pl.when(j == pl.num_programs(1) - 1)
    def _():
        lg = masked(_dot_nt(q, kn_ref[0].astype(BF16)), bias_new_ref[0, 0])
        update([lg], lambda ps: _dot(ps[0], vn_ref[0].astype(BF16)))
        a_ref[0] = (acc_ref[...] / l_ref[...]).astype(BF16)


def _sample_attend(page_table, q_s, bias, kn, vn, cache_kt, cache_vt, layer_off, pages_per_step):
    ns, npages = page_table.shape
    rows = q_s.shape[1]
    s_q = rows // N_HEADS
    steps = npages // pages_per_step

    def page_spec(r):
        return pl.BlockSpec((1, ATT_KV, PAGE_SIZE),
                            lambda b, j, pt: (layer_off + pt[b, j * pages_per_step + r], 0, 0))

    new_spec = pl.BlockSpec((1, PAGE_SIZE, ATT_KV), lambda b, j, pt: (b, 0, 0))
    grid_spec = pltpu.PrefetchScalarGridSpec(
        num_scalar_prefetch=1,
        grid=(ns, steps),
        in_specs=[pl.BlockSpec((1, rows, ATT_KV), lambda b, j, pt: (b, 0, 0)),
                  pl.BlockSpec((1, pages_per_step, s_q, LANES), lambda b, j, pt: (b, j, 0, 0)),
                  pl.BlockSpec((1, 1, s_q, LANES), lambda b, j, pt: (b, npages, 0, 0)),
                  new_spec, new_spec]
                 + [page_spec(r) for r in range(pages_per_step)] * 2,
        out_specs=pl.BlockSpec((1, rows, ATT_KV), lambda b, j, pt: (b, 0, 0)),
        scratch_shapes=[
            pltpu.VMEM((rows, 1), F32),
            pltpu.VMEM((rows, 1), F32),
            pltpu.VMEM((rows, ATT_KV), F32),
        ],
    )
    return pl.pallas_call(
        functools.partial(_sample_attend_kernel, pages_per_step=pages_per_step, s_q=s_q),
        grid_spec=grid_spec,
        out_shape=jax.ShapeDtypeStruct((ns, rows, ATT_KV), BF16),
        compiler_params=_cparams(("parallel", "arbitrary")),
        name="sample_attend",
    )(page_table, q_s, bias, bias, kn, vn,
      *([cache_kt] * pages_per_step), *([cache_vt] * pages_per_step))


def _merge_kernel(x_ref, a_ref, d_ref, u_ref, gvn_ref, wc_ref, gb_ref, wg_ref, wba_ref, pbd_ref, psc_ref,
                  wbp_ref, wbg_ref, wo_ref, lng_ref, lnb_ref, rwt_ref, rb_ref,
                  x1_ref, gate_ref, *, alpha):
    tm = x_ref.shape[0]
    x = x_ref[...]
    xb = x.astype(BF16)

    def gate(idx):
        return _sigmoid(_dot(xb, wg_ref[:, idx * D_MODEL:(idx + 1) * D_MODEL]))

    m = gate(0) * _dot(a_ref[...], wba_ref[...])
    y = _dot(d_ref[...].astype(BF16), pbd_ref[...]) * psc_ref[...]
    m = m + gate(1) * _dot(y.astype(BF16), wbp_ref[...])
    gv = gvn_ref[...].astype(BF16)
    grp = lax.broadcasted_iota(I32, (tm, GMLP_WIDTH), 1) // GMLP_GC
    mix = gb_ref[...]
    for g in range(GMLP_GROUPS):
        mix = mix + jnp.where(grp == g, _dot(wc_ref[g], gv), 0.0)
    c = u_ref[...] * mix
    m = m + gate(2) * _dot(c.astype(BF16), wbg_ref[...])
    y = _dot(m.astype(BF16), wo_ref[...])
    x1 = _layer_norm(alpha * x + y, lng_ref[...], lnb_ref[...])
    x1_ref[...] = x1

    scores = _sigmoid(_dot_nt(rwt_ref[...], x1.astype(BF16)))
    sel = scores + rb_ref[...]
    per = N_EXPERTS // N_EXPERT_GROUPS
    g3 = sel.reshape(N_EXPERT_GROUPS, per, tm)
    sub = lax.broadcasted_iota(I32, (N_EXPERT_GROUPS, per, tm), 1)
    m1 = jnp.max(g3, axis=1, keepdims=True)
    first = jnp.min(jnp.where(g3 == m1, sub, per), axis=1, keepdims=True)
    m2 = jnp.max(jnp.where(sub == first, -jnp.inf, g3), axis=1, keepdims=True)
    gs = (m1 + m2).reshape(N_EXPERT_GROUPS, tm)

    def rank_of(vals, count):
        idx = lax.broadcasted_iota(I32, vals.shape, 0)
        rank = jnp.zeros(vals.shape, I32)
        for o in range(count):
            other = vals[o:o + 1]
            beats = (other > vals) | ((other == vals) & (o < idx))
            rank = rank + jnp.where(beats, 1, 0)
        return rank

    gkeep = rank_of(gs, N_EXPERT_GROUPS) < TOPK_GROUPS
    ekeep = jnp.broadcast_to(gkeep.reshape(N_EXPERT_GROUPS, 1, tm), (N_EXPERT_GROUPS, per, tm)).reshape(N_EXPERTS, tm)
    sel = jnp.where(ekeep, sel, -jnp.inf)
    chosen = rank_of(sel, N_EXPERTS) < TOPK_EXPERTS
    wsel = jnp.where(chosen, scores, 0.0)
    gate_t = wsel / jnp.sum(wsel, axis=0, keepdims=True) * ROUTED_SCALE
    gate_ref[...] = gate_t.T


def _merge(x, a_bd, d, u, gvn, wl, tm, alpha):
    m = x.shape[0]
    row = lambda w: pl.BlockSpec((tm, w), lambda i: (i, 0))
    full = lambda a: pl.BlockSpec(a.shape, lambda i: (0,) * a.ndim)
    weights = [wl["wc"], wl["gb"], wl["wg"], wl["wba"], wl["pbd"], wl["psc"], wl["wbp"], wl["wbg"], wl["wo"],
               wl["ln1_g"], wl["ln1_b"], wl["rwt"], wl["rb"]]
    return pl.pallas_call(
        functools.partial(_merge_kernel, alpha=alpha),
        grid=(m // tm,),
        in_specs=[row(D_MODEL), row(ATT_Q), row(POOL_WIDTH), row(GMLP_WIDTH), row(GMLP_WIDTH)]
                 + [full(w) for w in weights],
        out_specs=(row(D_MODEL), row(N_EXPERTS)),
        out_shape=(jax.ShapeDtypeStruct((m, D_MODEL), F32), jax.ShapeDtypeStruct((m, N_EXPERTS), F32)),
        compiler_params=_cparams(("parallel",)),
        name="merge",
    )(x, a_bd, d, u, gvn, *weights)


def _swiglu_act(xb, w_gu):
    h = _dot(xb, w_gu)
    g = h[:, :EXPERT_FF]
    return g * _sigmoid(g) * h[:, EXPERT_FF:]


def _moe_kernel(x_ref, gate_ref, wgu_ref, wd_ref, sgu_ref, sd_ref, lng_ref, lnb_ref, o_ref, xb_ref, *, alpha):
    step = pl.program_id(1)
    tm = x_ref.shape[0]
    row_chunks = [slice(r, r + MOE_ROW_CHUNK) for r in range(0, tm, MOE_ROW_CHUNK)] if tm > MOE_ROW_CHUNK \
        else [slice(0, tm)]

    @pl.when(step == 0)
    def _():
        for rows in row_chunks:
            xb_ref[rows, :] = x_ref[rows, :].astype(BF16)
            o_ref[rows, :] = _dot(_swiglu_act(xb_ref[rows, :], sgu_ref[...]).astype(BF16), sd_ref[...])

    w_gu = [wgu_ref[j].astype(BF16) for j in range(MOE_EXPERTS_PER_STEP)]
    w_down = wd_ref[...].astype(BF16).reshape(MOE_EXPERTS_PER_STEP * EXPERT_FF, D_MODEL)
    for rows in row_chunks:
        xb = xb_ref[rows, :]
        gate = gate_ref[rows, :]
        lane = lax.broadcasted_iota(I32, gate.shape, 1)
        acts = []
        for j in range(MOE_EXPERTS_PER_STEP):
            e = step * MOE_EXPERTS_PER_STEP + j
            gcol = jnp.sum(jnp.where(lane == e, gate, 0.0), axis=1, keepdims=True)
            acts.append((gcol * _swiglu_act(xb, w_gu[j])).astype(BF16))
        o_ref[rows, :] += _dot(jnp.concatenate(acts, axis=1), w_down)

    @pl.when(step == pl.num_programs(1) - 1)
    def _():
        for rows in row_chunks:
            o_ref[rows, :] = _layer_norm(alpha * x_ref[rows, :] + o_ref[rows, :], lng_ref[...], lnb_ref[...])


def _moe(x1, gate, w_gu, w_down, layer, sh_gu, sh_down, ln_g, ln_b, tm, alpha):
    m = x1.shape[0]
    full = lambda a: pl.BlockSpec(a.shape, lambda i, e: (0,) * a.ndim)
    per = MOE_EXPERTS_PER_STEP
    steps = N_EXPERTS // per
    once = pl.Buffered(1)
    return pl.pallas_call(
        functools.partial(_moe_kernel, alpha=alpha),
        grid=(m // tm, steps),
        in_specs=[
            pl.BlockSpec((tm, D_MODEL), lambda i, e: (i, 0), pipeline_mode=once),
            pl.BlockSpec((tm, N_EXPERTS), lambda i, e: (i, 0), pipeline_mode=once),
            pl.BlockSpec((per, D_MODEL, 2 * EXPERT_FF), lambda i, e: (layer * steps + e, 0, 0)),
            pl.BlockSpec((per, EXPERT_FF, D_MODEL), lambda i, e: (layer * steps + e, 0, 0)),
            full(sh_gu), full(sh_down), full(ln_g), full(ln_b),
        ],
        out_specs=pl.BlockSpec((tm, D_MODEL), lambda i, e: (i, 0), pipeline_mode=once),
        out_shape=jax.ShapeDtypeStruct((m, D_MODEL), F32),
        scratch_shapes=[pltpu.VMEM((tm, D_MODEL), BF16)],
        compiler_params=_cparams(("parallel", "arbitrary")),
        name="moe",
    )(x1, gate, w_gu, w_down, sh_gu, sh_down, ln_g, ln_b)


def _rope_tables(pos):
    half = HEAD_DIM // 2
    inv = ROPE_THETA ** (-jnp.arange(half, dtype=F32) / half)
    ang = pos.astype(F32)[:, None] * inv[None, :]
    cos, sin = jnp.cos(ang), jnp.sin(ang)
    cos_t = jnp.tile(cos, (1, LANES // half))
    sin_t = jnp.tile(jnp.concatenate([-sin, sin], axis=1), (1, LANES // HEAD_DIM))
    return cos_t, sin_t


def _block_diag(blocks):
    g, r, c = blocks.shape
    eye = jnp.eye(g, dtype=blocks.dtype)
    return jnp.einsum("grc,gh->grhc", blocks, eye).reshape(g * r, g * c)


def _layer_weights(l, w_in, w_ba, w_bp, w_bg, w_out, pool_w, pool_scale, g_ln_g, g_ln_b, g_ws, g_b,
                   ln1_g, ln1_b, router_w, router_bias, sh_gu, sh_down, ln2_g, ln2_b):
    w = w_in[l]
    sizes = (ATT_Q, ATT_KV, ATT_KV, N_IDX_HEADS * IDX_DIM, IDX_DIM, N_IDX_HEADS,
             POOL_WIDTH, GMLP_WIDTH, GMLP_WIDTH, N_BRANCH * D_MODEL)
    offs = np.concatenate([[0], np.cumsum(sizes)]).tolist()
    wq, wk, wv, wiq, wik, wiw, wp, wu, wgv, wg = [w[:, offs[j]:offs[j + 1]] for j in range(len(sizes))]
    wq = wq * (HEAD_DIM ** -0.5 * LOG2E)
    wikw = jnp.pad(jnp.concatenate([wik, wiw], axis=1), ((0, 0), (0, LANES - IDX_DIM - N_IDX_HEADS)))
    w_cat = jnp.concatenate([wq, wk, wv, wiq, wikw, wp, wu, wgv], axis=1).astype(BF16)
    return dict(
        w_cat=w_cat, wvt=wv.T.astype(BF16), wiwt=wiw.T.astype(BF16), wg=wg.astype(BF16),
        wba=w_ba[l].astype(BF16),
        pbd=_block_diag(pool_w[l]).astype(BF16), psc=pool_scale[l].reshape(1, POOL_WIDTH),
        wbp=w_bp[l].astype(BF16), wbg=w_bg[l].astype(BF16), wo=w_out[l].astype(BF16),
        g_ln_g=g_ln_g[l].reshape(1, GMLP_WIDTH), g_ln_b=g_ln_b[l].reshape(1, GMLP_WIDTH),
        ws=g_ws[l], gbias=g_b[l],
        ln1_g=ln1_g[l].reshape(1, D_MODEL), ln1_b=ln1_b[l].reshape(1, D_MODEL),
        rwt=router_w[l].T.astype(BF16), rbias=router_bias[l],
        sh_gu=sh_gu[l].astype(BF16), sh_down=sh_down[l].astype(BF16),
        ln2_g=ln2_g[l].reshape(1, D_MODEL), ln2_b=ln2_b[l].reshape(1, D_MODEL),
    )


def _chunk_mix(wl, cl, tm):
    tril = jnp.tril(jnp.ones((cl, cl), F32))
    wm = wl["ws"][:, :cl, :cl] * tril
    reps = tm // cl
    wc = jnp.stack([_block_diag(jnp.broadcast_to(wm[g], (reps, cl, cl))) for g in range(GMLP_GROUPS)])
    gb = jnp.tile(jnp.repeat(wl["gbias"][:, :cl].T, GMLP_GC, axis=1), (reps, 1))
    return wc.astype(BF16), gb


def _heads_by_seq(a, ns, s_q):
    heads = a.shape[1] // HEAD_DIM
    return a.reshape(ns, s_q, heads, HEAD_DIM).transpose(0, 2, 1, 3).reshape(ns, heads * s_q, HEAD_DIM)


def kernel(x_prompt, x_sample, cache_k, cache_v, cache_idx_k, state_pool, page_table, w_in, w_branch_attn,
           w_branch_pool, w_branch_gmlp, w_out, pool_w, pool_scale, gmlp_ln_g, gmlp_ln_b, gmlp_ws, gmlp_b,
           ln1_g, ln1_b, router_w, router_bias, expert_w_gu, expert_w_down, shared_w_gu, shared_w_down,
           ln2_g, ln2_b):
    n_p, t_p, _ = x_prompt.shape
    n_s, t_s, _ = x_sample.shape
    depth = w_in.shape[0]
    n_pool = cache_k.shape[1]
    npages = page_table.shape[1]
    past = npages * PAGE_SIZE
    m_p, m_s = n_p * t_p, n_s * t_s
    alpha = (2 * depth) ** 0.25
    assert t_p % BLK == 0 and m_s % BLK == 0 and BLK % t_s == 0 and t_s <= 16

    tm_p = BLK
    tm_s = BLK
    tm_moe = 2048 if m_p % 2048 == 0 else tm_p
    tp_pool = 512 if t_p % 512 == 0 else BLK
    pages_per_step = 16 if npages % 16 == 0 else 1
    score_pages_per_step = 32 if npages % 32 == 0 else pages_per_step
    sel_batch = 8 if n_s % 8 == 0 else 1
    topk_p = min(TOPK_MAX, t_p // 4)
    topk_s = min(TOPK_MAX, (past + t_s) // 4)

    cos_p, sin_p = _rope_tables(jnp.arange(t_p, dtype=I32))
    cos_s, sin_s = _rope_tables(past + jnp.arange(t_s, dtype=I32))
    cos_s, sin_s = jnp.tile(cos_s, (m_s // t_s, 1)), jnp.tile(sin_s, (m_s // t_s, 1))
    tri_p = _tri_matrix(BLK, keys_on_lanes=False)
    group_of_row = jnp.asarray(np.eye(N_KV_HEADS, dtype=np.float32)[
        np.repeat(np.arange(N_HEADS) // (N_HEADS // N_KV_HEADS), t_s)])
    tri_s = _tri_matrix(LANES, keys_on_lanes=True)
    cache_kt = cache_k.transpose(0, 1, 3, 4, 2).reshape(depth * n_pool, ATT_KV, PAGE_SIZE)
    cache_vt = cache_v.transpose(0, 1, 3, 4, 2).reshape(depth * n_pool, ATT_KV, PAGE_SIZE)
    cache_ikt = cache_idx_k.transpose(0, 1, 3, 2).reshape(depth * n_pool, IDX_DIM, PAGE_SIZE)

    w_gu_all = expert_w_gu.reshape(depth * N_EXPERTS, D_MODEL, 2 * EXPERT_FF)
    w_down_all = expert_w_down.reshape(depth * N_EXPERTS, EXPERT_FF, D_MODEL)

    hp = x_prompt.reshape(m_p, D_MODEL)
    hs = x_sample.reshape(m_s, D_MODEL)
    outs = {name: [] for name in ("kp", "vp", "ikp", "pp", "ks", "vs", "iks", "ps", "gs")}
    for l in range(depth):
        wl = _layer_weights(l, w_in, w_branch_attn, w_branch_pool, w_branch_gmlp, w_out, pool_w, pool_scale,
                            gmlp_ln_g, gmlp_ln_b, gmlp_ws, gmlp_b, ln1_g, ln1_b, router_w, router_bias,
                            shared_w_gu, shared_w_down, ln2_g, ln2_b)

        def finish(x, a_bd, d, u, gvn, cl, tm, tm_e):
            wc, gb = _chunk_mix(wl, cl, tm)
            wm = dict(wl, wc=wc, gb=gb, rb=jnp.broadcast_to(wl["rbias"][:, None], (N_EXPERTS, tm)))
            x1, gate = _merge(x, a_bd, d, u, gvn, wm, tm, alpha)
            return _moe(x1, gate, w_gu_all, w_down_all, l, wl["sh_gu"], wl["sh_down"],
                        wl["ln2_g"], wl["ln2_b"], tm_e, alpha)

        q, k, v, kb, vt, iq, ikw, ikb, iwt, p, u, gvn = _proj(hp, wl, cos_p, sin_p, tm_p)
        p3 = p.reshape(n_p, t_p, POOL_WIDTH)
        d = _pool(p3, None, 0, tp_pool).reshape(m_p, POOL_WIDTH)
        a = _dsa_prompt(iq, iwt, q, ikb, kb, vt, tri_p, n_p, t_p, topk_p).reshape(m_p, ATT_Q)
        hp = finish(hp, a, d, u, gvn, CHUNK, tm_p, tm_moe)
        outs["kp"].append(k.reshape(n_p, t_p, N_KV_HEADS, HEAD_DIM))
        outs["vp"].append(v.reshape(n_p, t_p, N_KV_HEADS, HEAD_DIM))
        outs["ikp"].append(ikw[:, :IDX_DIM].reshape(n_p, t_p, IDX_DIM))
        outs["pp"].append(p3[:, t_p - POOL_STATE:])

        q, k, v, kb, vt, iq, ikw, ikb, iwt, p, u, gvn = _proj(hs, wl, cos_s, sin_s, tm_s)
        p3 = p.reshape(n_s, t_s, POOL_WIDTH)
        prefix16 = jnp.pad(state_pool[l], ((0, 0), (16 - POOL_STATE, 0), (0, 0)))
        d = _pool(p3, prefix16, past, t_s).reshape(m_s, POOL_WIDTH)
        iq_s = _heads_by_seq(iq, n_s, t_s)
        q_s = _heads_by_seq(q, n_s, t_s)
        q_s = jnp.einsum("nrc,rg->nrgc", q_s, group_of_row.astype(q_s.dtype)).reshape(n_s, -1, ATT_KV)
        wb_s = jnp.broadcast_to(iwt.reshape(N_IDX_HEADS, n_s, t_s).transpose(1, 0, 2).reshape(n_s, -1, 1),
                                (n_s, N_IDX_HEADS * t_s, LANES))
        pad_rows = lambda a: jnp.pad(a.reshape(n_s, t_s, -1), ((0, 0), (0, PAGE_SIZE - t_s), (0, 0)))
        ikn = pad_rows(ikb[:, :IDX_DIM])
        scores = _sample_scores(page_table, iq_s, wb_s, cache_ikt, l * n_pool, score_pages_per_step)
        bias = _sample_select(scores, iq_s, wb_s, ikn, tri_s, topk_s, sel_batch)
        a_s = _sample_attend(page_table, q_s, bias, pad_rows(k), pad_rows(v), cache_kt, cache_vt,
                             l * n_pool, pages_per_step)
        a_s = a_s.reshape(n_s, N_HEADS, t_s, N_KV_HEADS, HEAD_DIM)
        a_s = jnp.stack([a_s[:, h, :, h // (N_HEADS // N_KV_HEADS)] for h in range(N_HEADS)], axis=2)
        hs = finish(hs, a_s.reshape(m_s, ATT_Q), d, u, gvn, t_s, tm_s, tm_s)
        outs["ks"].append(k.reshape(n_s, t_s, N_KV_HEADS, HEAD_DIM))
        outs["vs"].append(v.reshape(n_s, t_s, N_KV_HEADS, HEAD_DIM))
        outs["iks"].append(ikw[:, :IDX_DIM].reshape(n_s, t_s, IDX_DIM))
        outs["ps"].append(jnp.concatenate([state_pool[l], p3], axis=1)[:, -POOL_STATE:])
        outs["gs"].append(gvn.reshape(n_s, t_s, GMLP_WIDTH))

    st = lambda name: jnp.stack(outs[name])
    return (hp.reshape(n_p, t_p, D_MODEL), hs.reshape(n_s, t_s, D_MODEL),
            st("kp"), st("vp"), st("ikp"), st("ks"), st("vs"), st("iks"), st("pp"), st("ps"), st("gs"))
```

```python
import functools

import jax
import jax.numpy as jnp
import numpy as np
from jax import lax
from jax.experimental import pallas as pl
from jax.experimental.pallas import tpu as pltpu

F32 = jnp.float32
BF16 = jnp.bfloat16
I32 = jnp.int32
I16 = jnp.int16

D_MODEL = 1024
N_HEADS = 8
N_KV_HEADS = 4
HEAD_DIM = 64
N_IDX_HEADS = 8
IDX_DIM = 64
TOPK_MAX = 256
PAGE_SIZE = 128
ROPE_THETA = 10000.0
POOL_WINDOWS = (2, 4, 8, 16)
POOL_WIDTH = 256
POOL_GC = 64
POOL_STATE = 15
GMLP_WIDTH = 256
GMLP_GROUPS = 4
GMLP_GC = 64
CHUNK = 128
N_BRANCH = 3
ATT_Q = N_HEADS * HEAD_DIM
ATT_KV = N_KV_HEADS * HEAD_DIM
N_EXPERTS = 64
TOPK_EXPERTS = 8
N_EXPERT_GROUPS = 8
TOPK_GROUPS = 4
EXPERT_FF = 256
ROUTED_SCALE = 2.5
LN_EPS = 1e-5

LANES = 128
SUBLANES = 8
PACK16 = 16
HALF16 = 2 ** 15
VT_ROWS = HEAD_DIM + PACK16
BLK = 256
MOE_EXPERTS_PER_STEP = 2
MOE_ROW_CHUNK = 1024
INT_MIN = -2 ** 31
MASKED = -1e30
LOG2E = 1.4426950408889634
VMEM_LIMIT = 56 * 1024 * 1024

C_Q = 0
C_K = C_Q + ATT_Q
C_V = C_K + ATT_KV
C_IQ = C_V + ATT_KV
C_IKW = C_IQ + N_IDX_HEADS * IDX_DIM
C_P = C_IKW + LANES
C_U = C_P + POOL_WIDTH
C_GV = C_U + GMLP_WIDTH
C_END = C_GV + GMLP_WIDTH


def _cparams(sem):
    return pltpu.CompilerParams(dimension_semantics=sem, vmem_limit_bytes=VMEM_LIMIT)


def _layer_norm(x, g, b):
    mu = jnp.mean(x, axis=-1, keepdims=True)
    xc = x - mu
    var = jnp.mean(xc * xc, axis=-1, keepdims=True)
    return xc * lax.rsqrt(var + LN_EPS) * g + b


def _sigmoid(x):
    return 1.0 / (1.0 + jnp.exp(-x))


def _dot(a, b):
    return jnp.dot(a, b, preferred_element_type=F32)


def _dot_nt(a, b):
    return lax.dot_general(a, b, (((1,), (1,)), ((), ())), preferred_element_type=F32)


def _sort_key(score):
    score = jnp.where(score == 0.0, 0.0, score)
    bits = lax.bitcast_convert_type(score, I32)
    return bits ^ ((bits >> 31) & jnp.int32(0x7FFFFFFF))


def _proj_kernel(x_ref, w_ref, wvt_ref, wiwt_ref, cos_ref, sin_ref, lng_ref, lnb_ref,
                 q_ref, k_ref, v_ref, kb_ref, vt_ref, iq_ref, ikw_ref, ikb_ref, iwt_ref,
                 p_ref, u_ref, gvn_ref):
    tm = x_ref.shape[0]
    xb = x_ref[...].astype(BF16)
    cos = cos_ref[...]
    sin = sin_ref[...]
    lane = lax.broadcasted_iota(I32, (tm, LANES), 1)
    first_half = (lane % HEAD_DIM) < (HEAD_DIM // 2)

    def mm(c0, width):
        return _dot(xb, w_ref[:, c0:c0 + width])

    def rope(z):
        partner = jnp.where(first_half, pltpu.roll(z, LANES - 32, 1), pltpu.roll(z, 32, 1))
        return z * cos + partner * sin

    def rope_wide(z):
        return jnp.concatenate([rope(z[:, s * LANES:(s + 1) * LANES]) for s in range(z.shape[1] // LANES)], axis=1)

    nblk = tm // BLK
    q_ref[...] = rope_wide(mm(C_Q, ATT_Q)).astype(BF16)
    k = rope_wide(mm(C_K, ATT_KV))
    k_ref[...] = k
    kb_ref[...] = k.astype(BF16)
    v_ref[...] = mm(C_V, ATT_KV)
    vt = _dot_nt(wvt_ref[...], xb)
    ones_rows = jnp.where(lax.broadcasted_iota(I32, (PACK16, tm), 0) == 0, 1.0, 0.0)
    vt = jnp.concatenate([piece for g in range(N_KV_HEADS)
                          for piece in (vt[g * HEAD_DIM:(g + 1) * HEAD_DIM], ones_rows)], axis=0).astype(BF16)
    for b in range(nblk):
        vt_ref[b] = vt[:, b * BLK:(b + 1) * BLK]
    iq_ref[...] = rope_wide(mm(C_IQ, N_IDX_HEADS * IDX_DIM)).astype(BF16)
    z = mm(C_IKW, LANES)
    is_key = lane < IDX_DIM
    ikw = jnp.where(is_key, rope(z), z)
    ikw_ref[...] = ikw
    ikb_ref[...] = jnp.where(is_key, ikw, 0.0).astype(BF16)
    iwt_ref[...] = _dot_nt(wiwt_ref[...], xb)
    p_ref[...] = mm(C_P, POOL_WIDTH)
    u_ref[...] = mm(C_U, GMLP_WIDTH)
    gvn_ref[...] = _layer_norm(mm(C_GV, GMLP_WIDTH), lng_ref[...], lnb_ref[...])


def _proj(x, wl, cos_t, sin_t, tm):
    m = x.shape[0]
    nt = cos_t.shape[0] // tm
    row = lambda w: pl.BlockSpec((tm, w), lambda i: (i, 0))
    full = lambda a: pl.BlockSpec(a.shape, lambda i: (0,) * a.ndim)
    tab = pl.BlockSpec((tm, LANES), lambda i: (i % nt, 0))
    nb = tm // BLK
    out_shape = (
        jax.ShapeDtypeStruct((m, ATT_Q), BF16),
        jax.ShapeDtypeStruct((m, ATT_KV), F32),
        jax.ShapeDtypeStruct((m, ATT_KV), F32),
        jax.ShapeDtypeStruct((m, ATT_KV), BF16),
        jax.ShapeDtypeStruct((m // BLK, N_KV_HEADS * VT_ROWS, BLK), BF16),
        jax.ShapeDtypeStruct((m, N_IDX_HEADS * IDX_DIM), BF16),
        jax.ShapeDtypeStruct((m, LANES), F32),
        jax.ShapeDtypeStruct((m, LANES), BF16),
        jax.ShapeDtypeStruct((N_IDX_HEADS, m), F32),
        jax.ShapeDtypeStruct((m, POOL_WIDTH), F32),
        jax.ShapeDtypeStruct((m, GMLP_WIDTH), F32),
        jax.ShapeDtypeStruct((m, GMLP_WIDTH), F32),
    )
    out_specs = (
        row(ATT_Q),
        row(ATT_KV), row(ATT_KV), row(ATT_KV),
        pl.BlockSpec((nb, N_KV_HEADS * VT_ROWS, BLK), lambda i: (i, 0, 0)),
        row(N_IDX_HEADS * IDX_DIM),
        row(LANES), row(LANES),
        pl.BlockSpec((N_IDX_HEADS, tm), lambda i: (0, i)),
        row(POOL_WIDTH), row(GMLP_WIDTH), row(GMLP_WIDTH),
    )
    weights = [wl["w_cat"], wl["wvt"], wl["wiwt"]]
    return pl.pallas_call(
        _proj_kernel,
        grid=(m // tm,),
        in_specs=[row(D_MODEL)] + [full(w) for w in weights] + [tab, tab, full(wl["g_ln_g"]), full(wl["g_ln_b"])],
        out_specs=out_specs,
        out_shape=out_shape,
        compiler_params=_cparams(("parallel",)),
        name="proj",
    )(x, *weights, cos_t, sin_t, wl["g_ln_g"], wl["g_ln_b"])


def _pool_kernel(p_ref, halo_ref, d_ref, ext_ref, s2_ref, s4_ref, s8_ref, *, pos_base, halo_is_prefix):
    tp = p_ref.shape[1]
    r_end = tp + 32
    i = pl.program_id(1)
    p = p_ref[0]
    halo = halo_ref[0]
    if not halo_is_prefix:
        halo = jnp.where(i == 0, 0.0, halo)
    ext_ref[0:16, :] = jnp.zeros((16, POOL_WIDTH), F32)
    ext_ref[16:32, :] = halo
    ext_ref[32:r_end, :] = p
    s2_ref[8:r_end, :] = ext_ref[8:r_end, :] + ext_ref[7:r_end - 1, :]
    s4_ref[16:r_end, :] = s2_ref[16:r_end, :] + s2_ref[14:r_end - 2, :]
    s8_ref[24:r_end, :] = s4_ref[24:r_end, :] + s4_ref[20:r_end - 4, :]
    s16 = s8_ref[32:r_end, :] + s8_ref[24:r_end - 8, :]
    lane = lax.broadcasted_iota(I32, (tp, POOL_WIDTH), 1)
    grp = lane // POOL_GC
    win = jnp.where(grp == 0, s2_ref[32:r_end, :],
                    jnp.where(grp == 1, s4_ref[32:r_end, :],
                              jnp.where(grp == 2, s8_ref[32:r_end, :], s16)))
    width = jnp.where(grp == 0, POOL_WINDOWS[0],
                      jnp.where(grp == 1, POOL_WINDOWS[1],
                                jnp.where(grp == 2, POOL_WINDOWS[2], POOL_WINDOWS[3])))
    pos = pos_base + i * tp + lax.broadcasted_iota(I32, (tp, POOL_WIDTH), 0)
    cnt = jnp.minimum(width, pos + 1).astype(F32)
    d_ref[0] = win / cnt - p


def _pool(p3, prefix16, pos_base, tp):
    n, t, _ = p3.shape
    halo_is_prefix = prefix16 is not None
    if halo_is_prefix:
        halo = prefix16
        halo_spec = pl.BlockSpec((1, 16, POOL_WIDTH), lambda b, i: (b, 0, 0))
    else:
        halo = p3
        step = tp // 16
        halo_spec = pl.BlockSpec((1, 16, POOL_WIDTH), lambda b, i: (b, jnp.maximum(i * step - 1, 0), 0))
    rows = tp + 32
    return pl.pallas_call(
        functools.partial(_pool_kernel, pos_base=pos_base, halo_is_prefix=halo_is_prefix),
        grid=(n, t // tp),
        in_specs=[pl.BlockSpec((1, tp, POOL_WIDTH), lambda b, i: (b, i, 0)), halo_spec],
        out_specs=pl.BlockSpec((1, tp, POOL_WIDTH), lambda b, i: (b, i, 0)),
        out_shape=jax.ShapeDtypeStruct((n, t, POOL_WIDTH), F32),
        scratch_shapes=[pltpu.VMEM((rows, POOL_WIDTH), F32) for _ in range(4)],
        compiler_params=_cparams(("parallel", "parallel")),
        name="pool",
    )(p3, halo)


def _kth_largest(sk_ref, hi_ref, lo_ref, nch, topk, keys_on_lanes, unroll=1):
    rows, cols = sk_ref.shape[1:]
    if keys_on_lanes:
        vec = (rows, 1)
        fold, acc_shape = (lambda x: x), (rows, cols)
        fold16, acc16_shape = fold, acc_shape
        total = lambda cnt: jnp.sum(cnt.astype(F32), axis=1, keepdims=True)
    else:
        vec = (1, cols)
        fold, acc_shape = (lambda x: jnp.sum(x.reshape(rows // SUBLANES, SUBLANES, cols), axis=0)), (SUBLANES, cols)
        fold16 = lambda x: functools.reduce(jnp.add, [x[g * PACK16:(g + 1) * PACK16] for g in range(rows // PACK16)])
        acc16_shape = (PACK16, cols)
        total = lambda cnt: jnp.sum(cnt.astype(F32), axis=0, keepdims=True)

    if not isinstance(nch, int):
        sk_ref[nch] = jnp.full((rows, cols), INT_MIN, I32)
        hi_ref[nch] = jnp.full((rows, cols), -HALF16, I16)
        lo_ref[nch] = jnp.full((rows, cols), -HALF16, I16)

    def over_chunks(body, init):
        if isinstance(nch, int):
            return lax.fori_loop(0, nch, body, init, unroll=unroll)
        return lax.fori_loop(0, (nch + 1) // 2, lambda j, carry: body(2 * j + 1, body(2 * j, carry)), init)

    def count(pred_fn):
        def body(c, cnt):
            return cnt + fold(jnp.where(pred_fn(sk_ref[c]), 1, 0))
        return total(over_chunks(body, jnp.zeros(acc_shape, I32)))

    def count16(ref, pred_fn):
        def body(c, cnt):
            return cnt + fold16(jnp.where(pred_fn(ref[c]), jnp.int16(1), jnp.int16(0)))
        return total(over_chunks(body, jnp.zeros(acc16_shape, I16)))

    def broadcast16(v):
        return jnp.broadcast_to(v.astype(I16), (rows, cols))

    def search16(ref, wanted):
        def bit_body(it, ubits):
            cand = ubits | lax.shift_left(jnp.int32(1), 15 - it)
            cand_b = broadcast16(cand - HALF16)
            tot = count16(ref, lambda s: s >= cand_b)
            return jnp.where(tot >= wanted, cand, ubits)
        return lax.fori_loop(0, 16, bit_body, jnp.zeros(vec, I32))

    hi = search16(hi_ref, topk) - HALF16
    hi_b = broadcast16(hi)
    wanted_lo = topk - count16(hi_ref, lambda s: s > hi_b)

    def low_halves(c, carry):
        low = ((sk_ref[c] & jnp.int32(0xFFFF)) - HALF16).astype(I16)
        lo_ref[c] = jnp.where(hi_ref[c] == hi_b, low, jnp.int16(-HALF16))
        return carry

    lax.fori_loop(0, nch, low_halves, 0)
    thr = hi * (2 * HALF16) + search16(lo_ref, wanted_lo)
    thr_b = jnp.broadcast_to(thr, (rows, cols))
    need_b = jnp.broadcast_to(topk - count(lambda s: s > thr_b), (rows, cols))
    surplus = jnp.where(thr == jnp.int32(INT_MIN), 0.0, count(lambda s: s >= thr_b) - topk)
    return thr_b, need_b, jnp.max(surplus) > 0.0


def _threshold_bias(skc, thr_b):
    return jnp.where((skc >= thr_b) & (skc > jnp.int32(INT_MIN)), 0.0, MASKED)


def _select_bias(skc, thr_b, need_b, eq_before, tri, keys_on_lanes):
    rows, cols = skc.shape
    eq = skc == thr_b
    eqf = jnp.where(eq, 1.0, 0.0).astype(BF16)
    if keys_on_lanes:
        res = _dot(eqf, tri)
        prefix, chunk_total = res[:, :cols], res[:, cols:]
    else:
        res = _dot(tri, eqf)
        prefix, chunk_total = res[:rows], res[rows:]
    keep = ((skc > thr_b) | (eq & (prefix + eq_before <= need_b))) & (skc > jnp.int32(INT_MIN))
    return jnp.where(keep, 0.0, MASKED), eq_before + chunk_total


def _tri_matrix(n, keys_on_lanes):
    r = np.arange(n)
    ones = np.ones((n, n), np.float32)
    if keys_on_lanes:
        return jnp.asarray(np.concatenate([(r[:, None] <= r[None, :]).astype(np.float32), ones], axis=1), BF16)
    return jnp.asarray(np.concatenate([(r[:, None] >= r[None, :]).astype(np.float32), ones], axis=0), BF16)


def _pad_heads(q, iq, qpad_ref, iqpad_ref):
    rows = q.shape[0]
    low_half = lax.broadcasted_iota(I32, (rows, LANES), 1) < HEAD_DIM
    zero_slab = jnp.zeros((rows, LANES), F32)

    def head_slabs(z):
        z = z.astype(F32)
        for s in range(z.shape[1] // LANES):
            slab = z[:, s * LANES:(s + 1) * LANES]
            swapped = pltpu.roll(slab, HEAD_DIM, 1)
            for r in range(2):
                yield 2 * s + r, (slab, swapped) if r == 0 else (swapped, slab)

    for h, (head_low, head_high) in head_slabs(q):
        g = h // (N_HEADS // N_KV_HEADS)
        own = jnp.where(low_half, head_low, 0.0) if g % 2 == 0 else jnp.where(low_half, 0.0, head_high)
        qpad_ref[h] = jnp.concatenate([own, zero_slab] if g // 2 == 0 else [zero_slab, own], axis=1).astype(BF16)
    for h, (head_low, _) in head_slabs(iq):
        iqpad_ref[h * rows:(h + 1) * rows, :] = jnp.where(low_half, head_low, 0.0).astype(BF16)


def _dsa_prompt_kernel(iq_ref, iwt_ref, q_ref, ikb_ref, kb_ref, vt_ref, tri_ref, a_ref,
                       sk_ref, hi_ref, lo_ref, m_ref, acc_ref, qpad_ref, iqpad_ref, *, topk):
    i = pl.program_id(1)
    nch = i + 1
    _pad_heads(q_ref[0], iq_ref[0], qpad_ref, iqpad_ref)
    iq = iqpad_ref[...]
    iwt = iwt_ref[...]
    key_id = lax.broadcasted_iota(I32, (BLK, BLK), 0)
    q_id = lax.broadcasted_iota(I32, (BLK, BLK), 1)

    def score_chunk(c, carry):
        s = _dot_nt(ikb_ref[0, c], iq)
        score = jnp.zeros((BLK, BLK), F32)
        for h in range(N_IDX_HEADS):
            score = score + jnp.maximum(s[:, h * BLK:(h + 1) * BLK], 0.0) * iwt[h:h + 1, :]
        key = jnp.where(key_id > q_id + jnp.where(c == i, 0, BLK), jnp.int32(INT_MIN), _sort_key(score))
        sk_ref[c] = key
        hi_ref[c] = (key >> 16).astype(I16)
        return carry

    lax.fori_loop(0, nch, score_chunk, 0)
    thr_b, need_b, any_tie = _kth_largest(sk_ref, hi_ref, lo_ref, nch, topk, keys_on_lanes=False)

    m_ref[...] = jnp.full(m_ref.shape, MASKED, F32)
    acc_ref[...] = jnp.zeros(acc_ref.shape, F32)
    heads_per_group = N_HEADS // N_KV_HEADS

    def attend_chunk(c, bias):
        kc = kb_ref[0, c]
        heads = range(N_HEADS)
        cols = [slice(h * BLK, (h + 1) * BLK) for h in heads]
        lgs = [_dot_nt(kc, qpad_ref[h]) + bias for h in heads]
        m_prev = [m_ref[:, cs] for cs in cols]
        m_new = [jnp.maximum(m_prev[h], jnp.max(lgs[h], axis=0, keepdims=True)) for h in heads]
        alpha = [jnp.exp2(m_prev[h] - m_new[h]) for h in heads]
        ps = [jnp.exp2(lgs[h] - m_new[h]) for h in heads]
        for h in heads:
            m_ref[:, cols[h]] = m_new[h]
        for h in heads:
            g = h // heads_per_group
            vg = vt_ref[0, c, g * VT_ROWS:(g + 1) * VT_ROWS, :]
            acc_ref[:, cols[h]] = alpha[h] * acc_ref[:, cols[h]] + _dot(vg, ps[h].astype(BF16))

    @pl.when(any_tie)
    def _():
        tri = tri_ref[...]

        def body(c, eq_before):
            bias, eq_after = _select_bias(sk_ref[c], thr_b, need_b, eq_before, tri, keys_on_lanes=False)
            attend_chunk(c, bias)
            return eq_after

        lax.fori_loop(0, nch, body, jnp.zeros((BLK, BLK), F32))

    @pl.when(jnp.logical_not(any_tie))
    def _():
        def body(c, carry):
            attend_chunk(c, _threshold_bias(sk_ref[c], thr_b))
            return carry

        lax.fori_loop(0, nch, body, 0)

    a_t = acc_ref[0:HEAD_DIM, :] / acc_ref[HEAD_DIM:HEAD_DIM + 1, :]
    a_t = jnp.concatenate([a_t[:, h * BLK:(h + 1) * BLK] for h in range(N_HEADS)], axis=0)
    a_ref[0] = a_t.T.astype(BF16)


def _dsa_prompt(iq, iwt, q, ikb, kb, vt, tri, n, t, topk):
    nb = t // BLK
    iq = iq.reshape(n * nb, BLK, N_IDX_HEADS * IDX_DIM)
    q = q.reshape(n * nb, BLK, ATT_Q)
    ikb4 = ikb.reshape(n, nb, BLK, LANES)
    kb4 = kb.reshape(n, nb, BLK, ATT_KV)
    vt4 = vt.reshape(n, nb, N_KV_HEADS * VT_ROWS, BLK)
    cols = N_HEADS * BLK
    return pl.pallas_call(
        functools.partial(_dsa_prompt_kernel, topk=topk),
        grid=(n, nb),
        in_specs=[
            pl.BlockSpec((1, BLK, N_IDX_HEADS * IDX_DIM), lambda b, i: (b * nb + i, 0, 0)),
            pl.BlockSpec((N_IDX_HEADS, BLK), lambda b, i: (0, b * nb + i)),
            pl.BlockSpec((1, BLK, ATT_Q), lambda b, i: (b * nb + i, 0, 0)),
            pl.BlockSpec((1, nb, BLK, LANES), lambda b, i: (b, 0, 0, 0)),
            pl.BlockSpec((1, nb, BLK, ATT_KV), lambda b, i: (b, 0, 0, 0)),
            pl.BlockSpec((1, nb, N_KV_HEADS * VT_ROWS, BLK), lambda b, i: (b, 0, 0, 0)),
            pl.BlockSpec(tri.shape, lambda b, i: (0, 0)),
        ],
        out_specs=pl.BlockSpec((1, BLK, ATT_Q), lambda b, i: (b * nb + i, 0, 0)),
        out_shape=jax.ShapeDtypeStruct((n * nb, BLK, ATT_Q), BF16),
        scratch_shapes=[
            pltpu.VMEM((nb + 1, BLK, BLK), I32),
            pltpu.VMEM((nb + 1, BLK, BLK), I16),
            pltpu.VMEM((nb + 1, BLK, BLK), I16),
            pltpu.VMEM((1, cols), F32),
            pltpu.VMEM((VT_ROWS, cols), F32),
            pltpu.VMEM((N_HEADS, BLK, ATT_KV), BF16),
            pltpu.VMEM((N_IDX_HEADS * BLK, LANES), BF16),
        ],
        compiler_params=_cparams(("parallel", "arbitrary")),
        name="dsa_prompt",
    )(iq, iwt, q, ikb4, kb4, vt4, tri)


def _sample_scores_kernel(pt_ref, iq_ref, wb_ref, *refs, pages_per_step):
    del pt_ref
    page_refs = refs[:pages_per_step]
    sc_ref = refs[pages_per_step]
    iq = iq_ref[0]
    wb = wb_ref[0]
    s_q = iq.shape[0] // N_IDX_HEADS
    for r in range(pages_per_step):
        s = _dot(iq, page_refs[r][0].astype(BF16))
        t = jnp.maximum(s, 0.0) * wb
        sc_ref[0, r] = jnp.sum(t.reshape(N_IDX_HEADS, s_q, LANES), axis=0)


def _sample_scores(page_table, iq_s, wb_s, cache_ikt, layer_off, pages_per_step):
    ns, npages = page_table.shape
    rows = iq_s.shape[1]
    s_q = rows // N_IDX_HEADS
    steps = npages // pages_per_step

    def page_spec(r):
        return pl.BlockSpec((1, IDX_DIM, PAGE_SIZE),
                            lambda b, j, pt: (layer_off + pt[b, j * pages_per_step + r], 0, 0))

    grid_spec = pltpu.PrefetchScalarGridSpec(
        num_scalar_prefetch=1,
        grid=(ns, steps),
        in_specs=[pl.BlockSpec((1, rows, IDX_DIM), lambda b, j, pt: (b, 0, 0)),
                  pl.BlockSpec((1, rows, LANES), lambda b, j, pt: (b, 0, 0))]
                 + [page_spec(r) for r in range(pages_per_step)],
        out_specs=pl.BlockSpec((1, pages_per_step, s_q, LANES), lambda b, j, pt: (b, j, 0, 0)),
    )
    return pl.pallas_call(
        functools.partial(_sample_scores_kernel, pages_per_step=pages_per_step),
        grid_spec=grid_spec,
        out_shape=jax.ShapeDtypeStruct((ns, npages, s_q, LANES), F32),
        compiler_params=_cparams(("parallel", "arbitrary")),
        name="sample_scores",
    )(page_table, iq_s, wb_s, *([cache_ikt] * pages_per_step))


def _sample_select_kernel(sc_ref, iq_ref, wb_ref, ikn_ref, tri_ref, bias_ref, sk_ref, hi_ref, lo_ref,
                          *, topk, s_q):
    sb = sc_ref.shape[0]
    npages = sc_ref.shape[1]
    rows = sb * s_q

    def key_chunk(c, carry):
        key = _sort_key(sc_ref[:, c].reshape(rows, LANES))
        sk_ref[c] = key
        hi_ref[c] = (key >> 16).astype(I16)
        return carry

    lax.fori_loop(0, npages, key_chunk, 0)
    row_id = lax.broadcasted_iota(I32, (s_q, LANES), 0)
    col_id = lax.broadcasted_iota(I32, (s_q, LANES), 1)
    new_keys = []
    for b in range(sb):
        s = _dot_nt(iq_ref[b], ikn_ref[b])
        t = jnp.maximum(s, 0.0) * wb_ref[b]
        score = jnp.sum(t.reshape(N_IDX_HEADS, s_q, LANES), axis=0)
        new_keys.append(jnp.where(col_id <= row_id, _sort_key(score), jnp.int32(INT_MIN)))
    key = jnp.concatenate(new_keys, axis=0)
    sk_ref[npages] = key
    hi_ref[npages] = (key >> 16).astype(I16)
    nch = npages + 1
    thr_b, need_b, any_tie = _kth_largest(sk_ref, hi_ref, lo_ref, nch, topk, keys_on_lanes=True, unroll=4)

    @pl.when(any_tie)
    def _():
        tri = tri_ref[...]

        def body(c, eq_before):
            bias, eq_after = _select_bias(sk_ref[c], thr_b, need_b, eq_before, tri, keys_on_lanes=True)
            bias_ref[:, c] = bias.reshape(sb, s_q, LANES)
            return eq_after

        lax.fori_loop(0, nch, body, jnp.zeros((rows, LANES), F32))

    @pl.when(jnp.logical_not(any_tie))
    def _():
        def body(c, carry):
            bias_ref[:, c] = _threshold_bias(sk_ref[c], thr_b).reshape(sb, s_q, LANES)
            return carry

        lax.fori_loop(0, nch, body, 0, unroll=4)


def _sample_select(scores, iq_s, wb_s, ikn, tri, topk, sb):
    ns, npages, s_q, _ = scores.shape
    rows = iq_s.shape[1]
    return pl.pallas_call(
        functools.partial(_sample_select_kernel, topk=topk, s_q=s_q),
        grid=(ns // sb,),
        in_specs=[
            pl.BlockSpec((sb, npages, s_q, LANES), lambda g: (g, 0, 0, 0)),
            pl.BlockSpec((sb, rows, IDX_DIM), lambda g: (g, 0, 0)),
            pl.BlockSpec((sb, rows, LANES), lambda g: (g, 0, 0)),
            pl.BlockSpec((sb, PAGE_SIZE, IDX_DIM), lambda g: (g, 0, 0)),
            pl.BlockSpec(tri.shape, lambda g: (0, 0)),
        ],
        out_specs=pl.BlockSpec((sb, npages + 1, s_q, LANES), lambda g: (g, 0, 0, 0)),
        out_shape=jax.ShapeDtypeStruct((ns, npages + 1, s_q, LANES), F32),
        scratch_shapes=[pltpu.VMEM((npages + 1, sb * s_q, LANES), dt) for dt in (I32, I16, I16)],
        compiler_params=_cparams(("parallel",)),
        name="sample_select",
    )(scores, iq_s, wb_s, ikn, tri)


def _sample_attend_kernel(pt_ref, q_ref, bias_ref, bias_new_ref, kn_ref, vn_ref, *refs, pages_per_step, s_q):
    del pt_ref
    k_refs = refs[:pages_per_step]
    v_refs = refs[pages_per_step:2 * pages_per_step]
    a_ref, m_ref, l_ref, acc_ref = refs[2 * pages_per_step:]
    j = pl.program_id(1)
    rows = N_HEADS * s_q

    @pl.when(j == 0)
    def _():
        m_ref[...] = jnp.full(m_ref.shape, MASKED, F32)
        l_ref[...] = jnp.zeros(l_ref.shape, F32)
        acc_ref[...] = jnp.zeros(acc_ref.shape, F32)

    q = q_ref[0]

    def masked(lg, bias):
        return (lg.reshape(N_HEADS, s_q, LANES) + bias[None]).reshape(rows, LANES)

    def update(lgs, weighted_values):
        top = functools.reduce(jnp.maximum, lgs)
        m_prev = m_ref[...]
        m_new = jnp.maximum(m_prev, jnp.max(top, axis=1, keepdims=True))
        alpha = jnp.exp2(m_prev - m_new)
        ps = [jnp.exp2(lg - m_new) for lg in lgs]
        l_ref[...] = alpha * l_ref[...] + jnp.sum(functools.reduce(jnp.add, ps), axis=1, keepdims=True)
        m_ref[...] = m_new
        acc_ref[...] = alpha * acc_ref[...] + weighted_values([p.astype(BF16) for p in ps])

    lgs = [masked(_dot(q, k_refs[r][0].astype(BF16)), bias_ref[0, r]) for r in range(pages_per_step)]
    update(lgs, lambda ps: functools.reduce(
        jnp.add, [_dot_nt(p, v_refs[r][0].astype(BF16)) for r, p in enumerate(ps)]))

    @pl.when(j == pl.num_programs(1) - 1)
    def _():
        lg = masked(_dot_nt(q, kn_ref[0].astype(BF16)), bias_new_ref[0, 0])
        update([lg], lambda ps: _dot(ps[0], vn_ref[0].astype(BF16)))
        a_ref[0] = (acc_ref[...] / l_ref[...]).astype(BF16)


def _sample_attend(page_table, q_s, bias, kn, vn, cache_kt, cache_vt, layer_off, pages_per_step):
    ns, npages = page_table.shape
    rows = q_s.shape[1]
    s_q = rows // N_HEADS
    steps = npages // pages_per_step

    def page_spec(r):
        return pl.BlockSpec((1, ATT_KV, PAGE_SIZE),
                            lambda b, j, pt: (layer_off + pt[b, j * pages_per_step + r], 0, 0))

    new_spec = pl.BlockSpec((1, PAGE_SIZE, ATT_KV), lambda b, j, pt: (b, 0, 0))
    grid_spec = pltpu.PrefetchScalarGridSpec(
        num_scalar_prefetch=1,
        grid=(ns, steps),
        in_specs=[pl.BlockSpec((1, rows, ATT_KV), lambda b, j, pt: (b, 0, 0)),
                  pl.BlockSpec((1, pages_per_step, s_q, LANES), lambda b, j, pt: (b, j, 0, 0)),
                  pl.BlockSpec((1, 1, s_q, LANES), lambda b, j, pt: (b, npages, 0, 0)),
                  new_spec, new_spec]
                 + [page_spec(r) for r in range(pages_per_step)] * 2,
        out_specs=pl.BlockSpec((1, rows, ATT_KV), lambda b, j, pt: (b, 0, 0)),
        scratch_shapes=[
            pltpu.VMEM((rows, 1), F32),
            pltpu.VMEM((rows, 1), F32),
            pltpu.VMEM((rows, ATT_KV), F32),
        ],
    )
    return pl.pallas_call(
        functools.partial(_sample_attend_kernel, pages_per_step=pages_per_step, s_q=s_q),
        grid_spec=grid_spec,
        out_shape=jax.ShapeDtypeStruct((ns, rows, ATT_KV), BF16),
        compiler_params=_cparams(("parallel", "arbitrary")),
        name="sample_attend",
    )(page_table, q_s, bias, bias, kn, vn,
      *([cache_kt] * pages_per_step), *([cache_vt] * pages_per_step))


def _merge_kernel(x_ref, a_ref, d_ref, u_ref, gvn_ref, wc_ref, gb_ref, wg_ref, wba_ref, pbd_ref, psc_ref,
                  wbp_ref, wbg_ref, wo_ref, lng_ref, lnb_ref, rwt_ref, rb_ref,
                  x1_ref, gate_ref, *, alpha):
    tm = x_ref.shape[0]
    x = x_ref[...]
    xb = x.astype(BF16)

    def gate(idx):
        return _sigmoid(_dot(xb, wg_ref[:, idx * D_MODEL:(idx + 1) * D_MODEL]))

    m = gate(0) * _dot(a_ref[...], wba_ref[...])
    y = _dot(d_ref[...].astype(BF16), pbd_ref[...]) * psc_ref[...]
    m = m + gate(1) * _dot(y.astype(BF16), wbp_ref[...])
    gv = gvn_ref[...].astype(BF16)
    grp = lax.broadcasted_iota(I32, (tm, GMLP_WIDTH), 1) // GMLP_GC
    mix = gb_ref[...]
    for g in range(GMLP_GROUPS):
        mix = mix + jnp.where(grp == g, _dot(wc_ref[g], gv), 0.0)
    c = u_ref[...] * mix
    m = m + gate(2) * _dot(c.astype(BF16), wbg_ref[...])
    y = _dot(m.astype(BF16), wo_ref[...])
    x1 = _layer_norm(alpha * x + y, lng_ref[...], lnb_ref[...])
    x1_ref[...] = x1

    scores = _sigmoid(_dot_nt(rwt_ref[...], x1.astype(BF16)))
    sel = scores + rb_ref[...]
    per = N_EXPERTS // N_EXPERT_GROUPS
    g3 = sel.reshape(N_EXPERT_GROUPS, per, tm)
    sub = lax.broadcasted_iota(I32, (N_EXPERT_GROUPS, per, tm), 1)
    m1 = jnp.max(g3, axis=1, keepdims=True)
    first = jnp.min(jnp.where(g3 == m1, sub, per), axis=1, keepdims=True)
    m2 = jnp.max(jnp.where(sub == first, -jnp.inf, g3), axis=1, keepdims=True)
    gs = (m1 + m2).reshape(N_EXPERT_GROUPS, tm)

    def rank_of(vals, count):
        idx = lax.broadcasted_iota(I32, vals.shape, 0)
        rank = jnp.zeros(vals.shape, I32)
        for o in range(count):
            other = vals[o:o + 1]
            beats = (other > vals) | ((other == vals) & (o < idx))
            rank = rank + jnp.where(beats, 1, 0)
        return rank

    gkeep = rank_of(gs, N_EXPERT_GROUPS) < TOPK_GROUPS
    ekeep = jnp.broadcast_to(gkeep.reshape(N_EXPERT_GROUPS, 1, tm), (N_EXPERT_GROUPS, per, tm)).reshape(N_EXPERTS, tm)
    sel = jnp.where(ekeep, sel, -jnp.inf)
    chosen = rank_of(sel, N_EXPERTS) < TOPK_EXPERTS
    wsel = jnp.where(chosen, scores, 0.0)
    gate_t = wsel / jnp.sum(wsel, axis=0, keepdims=True) * ROUTED_SCALE
    gate_ref[...] = gate_t.T


def _merge(x, a_bd, d, u, gvn, wl, tm, alpha):
    m = x.shape[0]
    row = lambda w: pl.BlockSpec((tm, w), lambda i: (i, 0))
    full = lambda a: pl.BlockSpec(a.shape, lambda i: (0,) * a.ndim)
    weights = [wl["wc"], wl["gb"], wl["wg"], wl["wba"], wl["pbd"], wl["psc"], wl["wbp"], wl["wbg"], wl["wo"],
               wl["ln1_g"], wl["ln1_b"], wl["rwt"], wl["rb"]]
    return pl.pallas_call(
        functools.partial(_merge_kernel, alpha=alpha),
        grid=(m // tm,),
        in_specs=[row(D_MODEL), row(ATT_Q), row(POOL_WIDTH), row(GMLP_WIDTH), row(GMLP_WIDTH)]
                 + [full(w) for w in weights],
        out_specs=(row(D_MODEL), row(N_EXPERTS)),
        out_shape=(jax.ShapeDtypeStruct((m, D_MODEL), F32), jax.ShapeDtypeStruct((m, N_EXPERTS), F32)),
        compiler_params=_cparams(("parallel",)),
        name="merge",
    )(x, a_bd, d, u, gvn, *weights)


def _swiglu_act(xb, w_gu):
    h = _dot(xb, w_gu)
    g = h[:, :EXPERT_FF]
    return g * _sigmoid(g) * h[:, EXPERT_FF:]


def _moe_kernel(x_ref, gate_ref, wgu_ref, wd_ref, sgu_ref, sd_ref, lng_ref, lnb_ref, o_ref, xb_ref, *, alpha):
    step = pl.program_id(1)
    tm = x_ref.shape[0]
    row_chunks = [slice(r, r + MOE_ROW_CHUNK) for r in range(0, tm, MOE_ROW_CHUNK)] if tm > MOE_ROW_CHUNK \
        else [slice(0, tm)]

    @pl.when(step == 0)
    def _():
        for rows in row_chunks:
            xb_ref[rows, :] = x_ref[rows, :].astype(BF16)
            o_ref[rows, :] = _dot(_swiglu_act(xb_ref[rows, :], sgu_ref[...]).astype(BF16), sd_ref[...])

    w_gu = [wgu_ref[j].astype(BF16) for j in range(MOE_EXPERTS_PER_STEP)]
    w_down = wd_ref[...].astype(BF16).reshape(MOE_EXPERTS_PER_STEP * EXPERT_FF, D_MODEL)
    for rows in row_chunks:
        xb = xb_ref[rows, :]
        gate = gate_ref[rows, :]
        lane = lax.broadcasted_iota(I32, gate.shape, 1)
        acts = []
        for j in range(MOE_EXPERTS_PER_STEP):
            e = step * MOE_EXPERTS_PER_STEP + j
            gcol = jnp.sum(jnp.where(lane == e, gate, 0.0), axis=1, keepdims=True)
            acts.append((gcol * _swiglu_act(xb, w_gu[j])).astype(BF16))
        o_ref[rows, :] += _dot(jnp.concatenate(acts, axis=1), w_down)

    @pl.when(step == pl.num_programs(1) - 1)
    def _():
        for rows in row_chunks:
            o_ref[rows, :] = _layer_norm(alpha * x_ref[rows, :] + o_ref[rows, :], lng_ref[...], lnb_ref[...])


def _moe(x1, gate, w_gu, w_down, layer, sh_gu, sh_down, ln_g, ln_b, tm, alpha):
    m = x1.shape[0]
    full = lambda a: pl.BlockSpec(a.shape, lambda i, e: (0,) * a.ndim)
    per = MOE_EXPERTS_PER_STEP
    steps = N_EXPERTS // per
    once = pl.Buffered(1)
    return pl.pallas_call(
        functools.partial(_moe_kernel, alpha=alpha),
        grid=(m // tm, steps),
        in_specs=[
            pl.BlockSpec((tm, D_MODEL), lambda i, e: (i, 0), pipeline_mode=once),
            pl.BlockSpec((tm, N_EXPERTS), lambda i, e: (i, 0), pipeline_mode=once),
            pl.BlockSpec((per, D_MODEL, 2 * EXPERT_FF), lambda i, e: (layer * steps + e, 0, 0)),
            pl.BlockSpec((per, EXPERT_FF, D_MODEL), lambda i, e: (layer * steps + e, 0, 0)),
            full(sh_gu), full(sh_down), full(ln_g), full(ln_b),
        ],
        out_specs=pl.BlockSpec((tm, D_MODEL), lambda i, e: (i, 0), pipeline_mode=once),
        out_shape=jax.ShapeDtypeStruct((m, D_MODEL), F32),
        scratch_shapes=[pltpu.VMEM((tm, D_MODEL), BF16)],
        compiler_params=_cparams(("parallel", "arbitrary")),
        name="moe",
    )(x1, gate, w_gu, w_down, sh_gu, sh_down, ln_g, ln_b)


def _rope_tables(pos):
    half = HEAD_DIM // 2
    inv = ROPE_THETA ** (-jnp.arange(half, dtype=F32) / half)
    ang = pos.astype(F32)[:, None] * inv[None, :]
    cos, sin = jnp.cos(ang), jnp.sin(ang)
    cos_t = jnp.tile(cos, (1, LANES // half))
    sin_t = jnp.tile(jnp.concatenate([-sin, sin], axis=1), (1, LANES // HEAD_DIM))
    return cos_t, sin_t


def _block_diag(blocks):
    g, r, c = blocks.shape
    eye = jnp.eye(g, dtype=blocks.dtype)
    return jnp.einsum("grc,gh->grhc", blocks, eye).reshape(g * r, g * c)


def _layer_weights(l, w_in, w_ba, w_bp, w_bg, w_out, pool_w, pool_scale, g_ln_g, g_ln_b, g_ws, g_b,
                   ln1_g, ln1_b, router_w, router_bias, sh_gu, sh_down, ln2_g, ln2_b):
    w = w_in[l]
    sizes = (ATT_Q, ATT_KV, ATT_KV, N_IDX_HEADS * IDX_DIM, IDX_DIM, N_IDX_HEADS,
             POOL_WIDTH, GMLP_WIDTH, GMLP_WIDTH, N_BRANCH * D_MODEL)
    offs = np.concatenate([[0], np.cumsum(sizes)]).tolist()
    wq, wk, wv, wiq, wik, wiw, wp, wu, wgv, wg = [w[:, offs[j]:offs[j + 1]] for j in range(len(sizes))]
    wq = wq * (HEAD_DIM ** -0.5 * LOG2E)
    wikw = jnp.pad(jnp.concatenate([wik, wiw], axis=1), ((0, 0), (0, LANES - IDX_DIM - N_IDX_HEADS)))
    w_cat = jnp.concatenate([wq, wk, wv, wiq, wikw, wp, wu, wgv], axis=1).astype(BF16)
    return dict(
        w_cat=w_cat, wvt=wv.T.astype(BF16), wiwt=wiw.T.astype(BF16), wg=wg.astype(BF16),
        wba=w_ba[l].astype(BF16),
        pbd=_block_diag(pool_w[l]).astype(BF16), psc=pool_scale[l].reshape(1, POOL_WIDTH),
        wbp=w_bp[l].astype(BF16), wbg=w_bg[l].astype(BF16), wo=w_out[l].astype(BF16),
        g_ln_g=g_ln_g[l].reshape(1, GMLP_WIDTH), g_ln_b=g_ln_b[l].reshape(1, GMLP_WIDTH),
        ws=g_ws[l], gbias=g_b[l],
        ln1_g=ln1_g[l].reshape(1, D_MODEL), ln1_b=ln1_b[l].reshape(1, D_MODEL),
        rwt=router_w[l].T.astype(BF16), rbias=router_bias[l],
        sh_gu=sh_gu[l].astype(BF16), sh_down=sh_down[l].astype(BF16),
        ln2_g=ln2_g[l].reshape(1, D_MODEL), ln2_b=ln2_b[l].reshape(1, D_MODEL),
    )


def _chunk_mix(wl, cl, tm):
    tril = jnp.tril(jnp.ones((cl, cl), F32))
    wm = wl["ws"][:, :cl, :cl] * tril
    reps = tm // cl
    wc = jnp.stack([_block_diag(jnp.broadcast_to(wm[g], (reps, cl, cl))) for g in range(GMLP_GROUPS)])
    gb = jnp.tile(jnp.repeat(wl["gbias"][:, :cl].T, GMLP_GC, axis=1), (reps, 1))
    return wc.astype(BF16), gb


def _heads_by_seq(a, ns, s_q):
    heads = a.shape[1] // HEAD_DIM
    return a.reshape(ns, s_q, heads, HEAD_DIM).transpose(0, 2, 1, 3).reshape(ns, heads * s_q, HEAD_DIM)


def kernel(x_prompt, x_sample, cache_k, cache_v, cache_idx_k, state_pool, page_table, w_in, w_branch_attn,
           w_branch_pool, w_branch_gmlp, w_out, pool_w, pool_scale, gmlp_ln_g, gmlp_ln_b, gmlp_ws, gmlp_b,
           ln1_g, ln1_b, router_w, router_bias, expert_w_gu, expert_w_down, shared_w_gu, shared_w_down,
           ln2_g, ln2_b):
    n_p, t_p, _ = x_prompt.shape
    n_s, t_s, _ = x_sample.shape
    depth = w_in.shape[0]
    n_pool = cache_k.shape[1]
    npages = page_table.shape[1]
    past = npages * PAGE_SIZE
    m_p, m_s = n_p * t_p, n_s * t_s
    alpha = (2 * depth) ** 0.25
    assert t_p % BLK == 0 and m_s % BLK == 0 and BLK % t_s == 0 and t_s <= 16

    tm_p = BLK
    tm_s = BLK
    tm_moe = 2048 if m_p % 2048 == 0 else tm_p
    tp_pool = 512 if t_p % 512 == 0 else BLK
    pages_per_step = 16 if npages % 16 == 0 else 1
    score_pages_per_step = 32 if npages % 32 == 0 else pages_per_step
    sel_batch = 8 if n_s % 8 == 0 else 1
    topk_p = min(TOPK_MAX, t_p // 4)
    topk_s = min(TOPK_MAX, (past + t_s) // 4)

    cos_p, sin_p = _rope_tables(jnp.arange(t_p, dtype=I32))
    cos_s, sin_s = _rope_tables(past + jnp.arange(t_s, dtype=I32))
    cos_s, sin_s = jnp.tile(cos_s, (m_s // t_s, 1)), jnp.tile(sin_s, (m_s // t_s, 1))
    tri_p = _tri_matrix(BLK, keys_on_lanes=False)
    group_of_row = jnp.asarray(np.eye(N_KV_HEADS, dtype=np.float32)[
        np.repeat(np.arange(N_HEADS) // (N_HEADS // N_KV_HEADS), t_s)])
    tri_s = _tri_matrix(LANES, keys_on_lanes=True)
    cache_kt = cache_k.transpose(0, 1, 3, 4, 2).reshape(depth * n_pool, ATT_KV, PAGE_SIZE)
    cache_vt = cache_v.transpose(0, 1, 3, 4, 2).reshape(depth * n_pool, ATT_KV, PAGE_SIZE)
    cache_ikt = cache_idx_k.transpose(0, 1, 3, 2).reshape(depth * n_pool, IDX_DIM, PAGE_SIZE)

    w_gu_all = expert_w_gu.reshape(depth * N_EXPERTS, D_MODEL, 2 * EXPERT_FF)
    w_down_all = expert_w_down.reshape(depth * N_EXPERTS, EXPERT_FF, D_MODEL)

    hp = x_prompt.reshape(m_p, D_MODEL)
    hs = x_sample.reshape(m_s, D_MODEL)
    outs = {name: [] for name in ("kp", "vp", "ikp", "pp", "ks", "vs", "iks", "ps", "gs")}
    for l in range(depth):
        wl = _layer_weights(l, w_in, w_branch_attn, w_branch_pool, w_branch_gmlp, w_out, pool_w, pool_scale,
                            gmlp_ln_g, gmlp_ln_b, gmlp_ws, gmlp_b, ln1_g, ln1_b, router_w, router_bias,
                            shared_w_gu, shared_w_down, ln2_g, ln2_b)

        def finish(x, a_bd, d, u, gvn, cl, tm, tm_e):
            wc, gb = _chunk_mix(wl, cl, tm)
            wm = dict(wl, wc=wc, gb=gb, rb=jnp.broadcast_to(wl["rbias"][:, None], (N_EXPERTS, tm)))
            x1, gate = _merge(x, a_bd, d, u, gvn, wm, tm, alpha)
            return _moe(x1, gate, w_gu_all, w_down_all, l, wl["sh_gu"], wl["sh_down"],
                        wl["ln2_g"], wl["ln2_b"], tm_e, alpha)

        q, k, v, kb, vt, iq, ikw, ikb, iwt, p, u, gvn = _proj(hp, wl, cos_p, sin_p, tm_p)
        p3 = p.reshape(n_p, t_p, POOL_WIDTH)
        d = _pool(p3, None, 0, tp_pool).reshape(m_p, POOL_WIDTH)
        a = _dsa_prompt(iq, iwt, q, ikb, kb, vt, tri_p, n_p, t_p, topk_p).reshape(m_p, ATT_Q)
        hp = finish(hp, a, d, u, gvn, CHUNK, tm_p, tm_moe)
        outs["kp"].append(k.reshape(n_p, t_p, N_KV_HEADS, HEAD_DIM))
        outs["vp"].append(v.reshape(n_p, t_p, N_KV_HEADS, HEAD_DIM))
        outs["ikp"].append(ikw[:, :IDX_DIM].reshape(n_p, t_p, IDX_DIM))
        outs["pp"].append(p3[:, t_p - POOL_STATE:])

        q, k, v, kb, vt, iq, ikw, ikb, iwt, p, u, gvn = _proj(hs, wl, cos_s, sin_s, tm_s)
        p3 = p.reshape(n_s, t_s, POOL_WIDTH)
        prefix16 = jnp.pad(state_pool[l], ((0, 0), (16 - POOL_STATE, 0), (0, 0)))
        d = _pool(p3, prefix16, past, t_s).reshape(m_s, POOL_WIDTH)
        iq_s = _heads_by_seq(iq, n_s, t_s)
        q_s = _heads_by_seq(q, n_s, t_s)
        q_s = jnp.einsum("nrc,rg->nrgc", q_s, group_of_row.astype(q_s.dtype)).reshape(n_s, -1, ATT_KV)
        wb_s = jnp.broadcast_to(iwt.reshape(N_IDX_HEADS, n_s, t_s).transpose(1, 0, 2).reshape(n_s, -1, 1),
                                (n_s, N_IDX_HEADS * t_s, LANES))
        pad_rows = lambda a: jnp.pad(a.reshape(n_s, t_s, -1), ((0, 0), (0, PAGE_SIZE - t_s), (0, 0)))
        ikn = pad_rows(ikb[:, :IDX_DIM])
        scores = _sample_scores(page_table, iq_s, wb_s, cache_ikt, l * n_pool, score_pages_per_step)
        bias = _sample_select(scores, iq_s, wb_s, ikn, tri_s, topk_s, sel_batch)
        a_s = _sample_attend(page_table, q_s, bias, pad_rows(k), pad_rows(v), cache_kt, cache_vt,
                             l * n_pool, pages_per_step)
        a_s = a_s.reshape(n_s, N_HEADS, t_s, N_KV_HEADS, HEAD_DIM)
        a_s = jnp.stack([a_s[:, h, :, h // (N_HEADS // N_KV_HEADS)] for h in range(N_HEADS)], axis=2)
        hs = finish(hs, a_s.reshape(m_s, ATT_Q), d, u, gvn, t_s, tm_s, tm_s)
        outs["ks"].append(k.reshape(n_s, t_s, N_KV_HEADS, HEAD_DIM))
        outs["vs"].append(v.reshape(n_s, t_s, N_KV_HEADS, HEAD_DIM))
        outs["iks"].append(ikw[:, :IDX_DIM].reshape(n_s, t_s, IDX_DIM))
        outs["ps"].append(jnp.concatenate([state_pool[l], p3], axis=1)[:, -POOL_STATE:])
        outs["gs"].append(gvn.reshape(n_s, t_s, GMLP_WIDTH))

    st = lambda name: jnp.stack(outs[name])
    return (hp.reshape(n_p, t_p, D_MODEL), hs.reshape(n_s, t_s, D_MODEL),
            st("kp"), st("vp"), st("ikp"), st("ks"), st("vs"), st("iks"), st("pp"), st("ps"), st("gs"))
```

```python
import functools

import jax
import jax.numpy as jnp
import numpy as np
from jax import lax
from jax.experimental import pallas as pl
from jax.experimental.pallas import tpu as pltpu

F32 = jnp.float32
BF16 = jnp.bfloat16
I32 = jnp.int32
I16 = jnp.int16

D_MODEL = 1024
N_HEADS = 8
N_KV_HEADS = 4
HEAD_DIM = 64
N_IDX_HEADS = 8
IDX_DIM = 64
TOPK_MAX = 256
PAGE_SIZE = 128
ROPE_THETA = 10000.0
POOL_WINDOWS = (2, 4, 8, 16)
POOL_WIDTH = 256
POOL_GC = 64
POOL_STATE = 15
GMLP_WIDTH = 256
GMLP_GROUPS = 4
GMLP_GC = 64
CHUNK = 128
N_BRANCH = 3
ATT_Q = N_HEADS * HEAD_DIM
ATT_KV = N_KV_HEADS * HEAD_DIM
N_EXPERTS = 64
TOPK_EXPERTS = 8
N_EXPERT_GROUPS = 8
TOPK_GROUPS = 4
EXPERT_FF = 256
ROUTED_SCALE = 2.5
LN_EPS = 1e-5

LANES = 128
SUBLANES = 8
PACK16 = 16
HALF16 = 2 ** 15
VT_ROWS = HEAD_DIM + PACK16
BLK = 256
ATTEND_HEADS_PER_STAGE = 8
MOE_EXPERTS_PER_STEP = 2
MOE_ROW_CHUNK = 1024
INT_MIN = -2 ** 31
MASKED = -1e30
LOG2E = 1.4426950408889634
VMEM_LIMIT = 56 * 1024 * 1024

C_Q = 0
C_K = C_Q + ATT_Q
C_V = C_K + ATT_KV
C_IQ = C_V + ATT_KV
C_IKW = C_IQ + N_IDX_HEADS * IDX_DIM
C_P = C_IKW + LANES
C_U = C_P + POOL_WIDTH
C_GV = C_U + GMLP_WIDTH
C_END = C_GV + GMLP_WIDTH


def _cparams(sem):
    return pltpu.CompilerParams(dimension_semantics=sem, vmem_limit_bytes=VMEM_LIMIT)


def _layer_norm(x, g, b):
    mu = jnp.mean(x, axis=-1, keepdims=True)
    xc = x - mu
    var = jnp.mean(xc * xc, axis=-1, keepdims=True)
    return xc * lax.rsqrt(var + LN_EPS) * g + b


def _sigmoid(x):
    return 1.0 / (1.0 + jnp.exp(-x))


def _dot(a, b):
    return jnp.dot(a, b, preferred_element_type=F32)


def _dot_nt(a, b):
    return lax.dot_general(a, b, (((1,), (1,)), ((), ())), preferred_element_type=F32)


def _sort_key(score):
    score = jnp.where(score == 0.0, 0.0, score)
    bits = lax.bitcast_convert_type(score, I32)
    return bits ^ ((bits >> 31) & jnp.int32(0x7FFFFFFF))


def _proj_kernel(x_ref, w_ref, wvt_ref, wiwt_ref, cos_ref, sin_ref, lng_ref, lnb_ref,
                 q_ref, k_ref, v_ref, kb_ref, vt_ref, iq_ref, ikw_ref, ikb_ref, iwt_ref,
                 p_ref, u_ref, gvn_ref):
    tm = x_ref.shape[0]
    xb = x_ref[...].astype(BF16)
    cos = cos_ref[...]
    sin = sin_ref[...]
    lane = lax.broadcasted_iota(I32, (tm, LANES), 1)
    first_half = (lane % HEAD_DIM) < (HEAD_DIM // 2)

    def mm(c0, width):
        return _dot(xb, w_ref[:, c0:c0 + width])

    def rope(z):
        partner = jnp.where(first_half, pltpu.roll(z, LANES - 32, 1), pltpu.roll(z, 32, 1))
        return z * cos + partner * sin

    def rope_wide(z):
        return jnp.concatenate([rope(z[:, s * LANES:(s + 1) * LANES]) for s in range(z.shape[1] // LANES)], axis=1)

    nblk = tm // BLK
    q_ref[...] = rope_wide(mm(C_Q, ATT_Q)).astype(BF16)
    k = rope_wide(mm(C_K, ATT_KV))
    k_ref[...] = k
    kb_ref[...] = k.astype(BF16)
    v_ref[...] = mm(C_V, ATT_KV)
    vt = _dot_nt(wvt_ref[...], xb)
    ones_rows = jnp.where(lax.broadcasted_iota(I32, (PACK16, tm), 0) == 0, 1.0, 0.0)
    vt = jnp.concatenate([piece for g in range(N_KV_HEADS)
                          for piece in (vt[g * HEAD_DIM:(g + 1) * HEAD_DIM], ones_rows)], axis=0).astype(BF16)
    for b in range(nblk):
        vt_ref[b] = vt[:, b * BLK:(b + 1) * BLK]
    iq_ref[...] = rope_wide(mm(C_IQ, N_IDX_HEADS * IDX_DIM)).astype(BF16)
    z = mm(C_IKW, LANES)
    is_key = lane < IDX_DIM
    ikw = jnp.where(is_key, rope(z), z)
    ikw_ref[...] = ikw
    ikb_ref[...] = jnp.where(is_key, ikw, 0.0).astype(BF16)
    iwt_ref[...] = _dot_nt(wiwt_ref[...], xb)
    p_ref[...] = mm(C_P, POOL_WIDTH)
    u_ref[...] = mm(C_U, GMLP_WIDTH)
    gvn_ref[...] = _layer_norm(mm(C_GV, GMLP_WIDTH), lng_ref[...], lnb_ref[...])


def _proj(x, wl, cos_t, sin_t, tm):
    m = x.shape[0]
    nt = cos_t.shape[0] // tm
    row = lambda w: pl.BlockSpec((tm, w), lambda i: (i, 0))
    full = lambda a: pl.BlockSpec(a.shape, lambda i: (0,) * a.ndim)
    tab = pl.BlockSpec((tm, LANES), lambda i: (i % nt, 0))
    nb = tm // BLK
    out_shape = (
        jax.ShapeDtypeStruct((m, ATT_Q), BF16),
        jax.ShapeDtypeStruct((m, ATT_KV), F32),
        jax.ShapeDtypeStruct((m, ATT_KV), F32),
        jax.ShapeDtypeStruct((m, ATT_KV), BF16),
        jax.ShapeDtypeStruct((m // BLK, N_KV_HEADS * VT_ROWS, BLK), BF16),
        jax.ShapeDtypeStruct((m, N_IDX_HEADS * IDX_DIM), BF16),
        jax.ShapeDtypeStruct((m, LANES), F32),
        jax.ShapeDtypeStruct((m, LANES), BF16),
        jax.ShapeDtypeStruct((N_IDX_HEADS, m), F32),
        jax.ShapeDtypeStruct((m, POOL_WIDTH), F32),
        jax.ShapeDtypeStruct((m, GMLP_WIDTH), F32),
        jax.ShapeDtypeStruct((m, GMLP_WIDTH), F32),
    )
    out_specs = (
        row(ATT_Q),
        row(ATT_KV), row(ATT_KV), row(ATT_KV),
        pl.BlockSpec((nb, N_KV_HEADS * VT_ROWS, BLK), lambda i: (i, 0, 0)),
        row(N_IDX_HEADS * IDX_DIM),
        row(LANES), row(LANES),
        pl.BlockSpec((N_IDX_HEADS, tm), lambda i: (0, i)),
        row(POOL_WIDTH), row(GMLP_WIDTH), row(GMLP_WIDTH),
    )
    weights = [wl["w_cat"], wl["wvt"], wl["wiwt"]]
    return pl.pallas_call(
        _proj_kernel,
        grid=(m // tm,),
        in_specs=[row(D_MODEL)] + [full(w) for w in weights] + [tab, tab, full(wl["g_ln_g"]), full(wl["g_ln_b"])],
        out_specs=out_specs,
        out_shape=out_shape,
        compiler_params=_cparams(("parallel",)),
        name="proj",
    )(x, *weights, cos_t, sin_t, wl["g_ln_g"], wl["g_ln_b"])


def _pool_kernel(p_ref, halo_ref, d_ref, ext_ref, s2_ref, s4_ref, s8_ref, *, pos_base, halo_is_prefix):
    tp = p_ref.shape[1]
    r_end = tp + 32
    i = pl.program_id(1)
    p = p_ref[0]
    halo = halo_ref[0]
    if not halo_is_prefix:
        halo = jnp.where(i == 0, 0.0, halo)
    ext_ref[0:16, :] = jnp.zeros((16, POOL_WIDTH), F32)
    ext_ref[16:32, :] = halo
    ext_ref[32:r_end, :] = p
    s2_ref[8:r_end, :] = ext_ref[8:r_end, :] + ext_ref[7:r_end - 1, :]
    s4_ref[16:r_end, :] = s2_ref[16:r_end, :] + s2_ref[14:r_end - 2, :]
    s8_ref[24:r_end, :] = s4_ref[24:r_end, :] + s4_ref[20:r_end - 4, :]
    s16 = s8_ref[32:r_end, :] + s8_ref[24:r_end - 8, :]
    lane = lax.broadcasted_iota(I32, (tp, POOL_WIDTH), 1)
    grp = lane // POOL_GC
    win = jnp.where(grp == 0, s2_ref[32:r_end, :],
                    jnp.where(grp == 1, s4_ref[32:r_end, :],
                              jnp.where(grp == 2, s8_ref[32:r_end, :], s16)))
    width = jnp.where(grp == 0, POOL_WINDOWS[0],
                      jnp.where(grp == 1, POOL_WINDOWS[1],
                                jnp.where(grp == 2, POOL_WINDOWS[2], POOL_WINDOWS[3])))
    pos = pos_base + i * tp + lax.broadcasted_iota(I32, (tp, POOL_WIDTH), 0)
    cnt = jnp.minimum(width, pos + 1).astype(F32)
    d_ref[0] = win / cnt - p


def _pool(p3, prefix16, pos_base, tp):
    n, t, _ = p3.shape
    halo_is_prefix = prefix16 is not None
    if halo_is_prefix:
        halo = prefix16
        halo_spec = pl.BlockSpec((1, 16, POOL_WIDTH), lambda b, i: (b, 0, 0))
    else:
        halo = p3
        step = tp // 16
        halo_spec = pl.BlockSpec((1, 16, POOL_WIDTH), lambda b, i: (b, jnp.maximum(i * step - 1, 0), 0))
    rows = tp + 32
    return pl.pallas_call(
        functools.partial(_pool_kernel, pos_base=pos_base, halo_is_prefix=halo_is_prefix),
        grid=(n, t // tp),
        in_specs=[pl.BlockSpec((1, tp, POOL_WIDTH), lambda b, i: (b, i, 0)), halo_spec],
        out_specs=pl.BlockSpec((1, tp, POOL_WIDTH), lambda b, i: (b, i, 0)),
        out_shape=jax.ShapeDtypeStruct((n, t, POOL_WIDTH), F32),
        scratch_shapes=[pltpu.VMEM((rows, POOL_WIDTH), F32) for _ in range(4)],
        compiler_params=_cparams(("parallel", "parallel")),
        name="pool",
    )(p3, halo)


def _kth_largest(sk_ref, hi_ref, lo_ref, nch, topk, keys_on_lanes, unroll=1):
    rows, cols = sk_ref.shape[1:]
    if keys_on_lanes:
        vec = (rows, 1)
        fold, acc_shape = (lambda x: x), (rows, cols)
        fold16, acc16_shape = fold, acc_shape
        total = lambda cnt: jnp.sum(cnt.astype(F32), axis=1, keepdims=True)
    else:
        vec = (1, cols)
        fold, acc_shape = (lambda x: jnp.sum(x.reshape(rows // SUBLANES, SUBLANES, cols), axis=0)), (SUBLANES, cols)
        fold16 = lambda x: functools.reduce(jnp.add, [x[g * PACK16:(g + 1) * PACK16] for g in range(rows // PACK16)])
        acc16_shape = (PACK16, cols)
        total = lambda cnt: jnp.sum(cnt.astype(F32), axis=0, keepdims=True)

    if not isinstance(nch, int):
        sk_ref[nch] = jnp.full((rows, cols), INT_MIN, I32)
        hi_ref[nch] = jnp.full((rows, cols), -HALF16, I16)
        lo_ref[nch] = jnp.full((rows, cols), -HALF16, I16)

    def over_chunks(body, init):
        if isinstance(nch, int):
            return lax.fori_loop(0, nch, body, init, unroll=unroll)
        return lax.fori_loop(0, (nch + 1) // 2, lambda j, carry: body(2 * j + 1, body(2 * j, carry)), init)

    def count(pred_fn):
        def body(c, cnt):
            return cnt + fold(jnp.where(pred_fn(sk_ref[c]), 1, 0))
        return total(over_chunks(body, jnp.zeros(acc_shape, I32)))

    def count16(ref, pred_fn):
        def body(c, cnt):
            return cnt + fold16(jnp.where(pred_fn(ref[c]), jnp.int16(1), jnp.int16(0)))
        return total(over_chunks(body, jnp.zeros(acc16_shape, I16)))

    def broadcast16(v):
        return jnp.broadcast_to(v.astype(I16), (rows, cols))

    def search16(ref, wanted):
        def bit_body(it, ubits):
            cand = ubits | lax.shift_left(jnp.int32(1), 15 - it)
            cand_b = broadcast16(cand - HALF16)
            tot = count16(ref, lambda s: s >= cand_b)
            return jnp.where(tot >= wanted, cand, ubits)
        return lax.fori_loop(0, 16, bit_body, jnp.zeros(vec, I32))

    hi = search16(hi_ref, topk) - HALF16
    hi_b = broadcast16(hi)
    wanted_lo = topk - count16(hi_ref, lambda s: s > hi_b)

    def low_halves(c, carry):
        low = ((sk_ref[c] & jnp.int32(0xFFFF)) - HALF16).astype(I16)
        lo_ref[c] = jnp.where(hi_ref[c] == hi_b, low, jnp.int16(-HALF16))
        return carry

    lax.fori_loop(0, nch, low_halves, 0)
    thr = hi * (2 * HALF16) + search16(lo_ref, wanted_lo)
    thr_b = jnp.broadcast_to(thr, (rows, cols))
    need_b = jnp.broadcast_to(topk - count(lambda s: s > thr_b), (rows, cols))
    surplus = jnp.where(thr == jnp.int32(INT_MIN), 0.0, count(lambda s: s >= thr_b) - topk)
    return jnp.maximum(thr_b, jnp.int32(INT_MIN + 1)), need_b, jnp.max(surplus) > 0.0


def _threshold_bias(skc, thr_b):
    return jnp.where(skc >= thr_b, 0.0, MASKED)


def _select_bias(skc, thr_b, need_b, eq_before, tri, keys_on_lanes):
    rows, cols = skc.shape
    eq = skc == thr_b
    eqf = jnp.where(eq, 1.0, 0.0).astype(BF16)
    if keys_on_lanes:
        res = _dot(eqf, tri)
        prefix, chunk_total = res[:, :cols], res[:, cols:]
    else:
        res = _dot(tri, eqf)
        prefix, chunk_total = res[:rows], res[rows:]
    keep = (skc > thr_b) | (eq & (prefix + eq_before <= need_b))
    return jnp.where(keep, 0.0, MASKED), eq_before + chunk_total


def _tri_matrix(n, keys_on_lanes):
    r = np.arange(n)
    ones = np.ones((n, n), np.float32)
    if keys_on_lanes:
        return jnp.asarray(np.concatenate([(r[:, None] <= r[None, :]).astype(np.float32), ones], axis=1), BF16)
    return jnp.asarray(np.concatenate([(r[:, None] >= r[None, :]).astype(np.float32), ones], axis=0), BF16)


def _pad_heads(q, iq, qpad_ref, iqpad_ref):
    rows = q.shape[0]
    low_half = lax.broadcasted_iota(I32, (rows, LANES), 1) < HEAD_DIM
    zero_slab = jnp.zeros((rows, LANES), F32)

    def head_slabs(z):
        z = z.astype(F32)
        for s in range(z.shape[1] // LANES):
            slab = z[:, s * LANES:(s + 1) * LANES]
            swapped = pltpu.roll(slab, HEAD_DIM, 1)
            for r in range(2):
                yield 2 * s + r, (slab, swapped) if r == 0 else (swapped, slab)

    for h, (head_low, head_high) in head_slabs(q):
        g = h // (N_HEADS // N_KV_HEADS)
        own = jnp.where(low_half, head_low, 0.0) if g % 2 == 0 else jnp.where(low_half, 0.0, head_high)
        qpad_ref[h] = jnp.concatenate([own, zero_slab] if g // 2 == 0 else [zero_slab, own], axis=1).astype(BF16)
    for h, (head_low, _) in head_slabs(iq):
        iqpad_ref[h * rows:(h + 1) * rows, :] = jnp.where(low_half, head_low, 0.0).astype(BF16)


def _dsa_prompt_kernel(iq_ref, iwt_ref, q_ref, ikb_ref, kb_ref, vt_ref, tri_ref, a_ref,
                       sk_ref, hi_ref, lo_ref, m_ref, acc_ref, qpad_ref, iqpad_ref, *, topk):
    i = pl.program_id(1)
    nch = i + 1
    _pad_heads(q_ref[0], iq_ref[0], qpad_ref, iqpad_ref)
    iwt = iwt_ref[...]
    key_id = lax.broadcasted_iota(I32, (BLK, BLK), 0)
    q_id = lax.broadcasted_iota(I32, (BLK, BLK), 1)

    def score_chunk(c, carry):
        ikc = ikb_ref[0, c]
        score = jnp.zeros((BLK, BLK), F32)
        for h in range(N_IDX_HEADS):
            s = _dot_nt(ikc, iqpad_ref[h * BLK:(h + 1) * BLK, :])
            score = score + jnp.maximum(s, 0.0) * iwt[h:h + 1, :]
        key = jnp.where(key_id > q_id + (i - c) * BLK, jnp.int32(INT_MIN), _sort_key(score))
        sk_ref[c] = key
        hi_ref[c] = (key >> 16).astype(I16)
        return carry

    lax.fori_loop(0, (nch + 1) // 2, lambda j, carry: score_chunk(2 * j + 1, score_chunk(2 * j, carry)), 0)
    thr_b, need_b, any_tie = _kth_largest(sk_ref, hi_ref, lo_ref, nch, topk, keys_on_lanes=False)

    m_ref[...] = jnp.full(m_ref.shape, MASKED, F32)
    acc_ref[...] = jnp.zeros(acc_ref.shape, F32)
    heads_per_group = N_HEADS // N_KV_HEADS

    def attend_chunk(c, bias):
        kc = kb_ref[0, c]
        cols = [slice(h * BLK, (h + 1) * BLK) for h in range(N_HEADS)]
        for h0 in range(0, N_HEADS, ATTEND_HEADS_PER_STAGE):
            heads = range(h0, h0 + ATTEND_HEADS_PER_STAGE)
            lgs = {h: _dot_nt(kc, qpad_ref[h]) + bias for h in heads}
            m_prev = {h: m_ref[:, cols[h]] for h in heads}
            m_new = {h: jnp.maximum(m_prev[h], jnp.max(lgs[h], axis=0, keepdims=True)) for h in heads}
            alpha = {h: jnp.exp2(m_prev[h] - m_new[h]) for h in heads}
            ps = {h: jnp.exp2(lgs[h] - m_new[h]) for h in heads}
            for h in heads:
                m_ref[:, cols[h]] = m_new[h]
            for h in heads:
                g = h // heads_per_group
                vg = vt_ref[0, c, g * VT_ROWS:(g + 1) * VT_ROWS, :]
                acc_ref[:, cols[h]] = alpha[h] * acc_ref[:, cols[h]] + _dot(vg, ps[h].astype(BF16))

    @pl.when(any_tie)
    def _():
        tri = tri_ref[...]

        def body(c, eq_before):
            bias, eq_after = _select_bias(sk_ref[c], thr_b, need_b, eq_before, tri, keys_on_lanes=False)
            attend_chunk(c, bias)
            return eq_after

        lax.fori_loop(0, nch, body, jnp.zeros((BLK, BLK), F32))

    @pl.when(jnp.logical_not(any_tie))
    def _():
        def body(c, carry):
            attend_chunk(c, _threshold_bias(sk_ref[c], thr_b))
            return carry

        lax.fori_loop(0, nch, body, 0)

    a_t = acc_ref[0:HEAD_DIM, :] / acc_ref[HEAD_DIM:HEAD_DIM + 1, :]
    a_t = jnp.concatenate([a_t[:, h * BLK:(h + 1) * BLK] for h in range(N_HEADS)], axis=0)
    a_ref[0] = a_t.T.astype(BF16)


def _dsa_prompt(iq, iwt, q, ikb, kb, vt, tri, n, t, topk):
    nb = t // BLK
    assert nb % 2 == 0, "the score loop reads key chunks in pairs"
    iq = iq.reshape(n * nb, BLK, N_IDX_HEADS * IDX_DIM)
    q = q.reshape(n * nb, BLK, ATT_Q)
    ikb4 = ikb.reshape(n, nb, BLK, LANES)
    kb4 = kb.reshape(n, nb, BLK, ATT_KV)
    vt4 = vt.reshape(n, nb, N_KV_HEADS * VT_ROWS, BLK)
    cols = N_HEADS * BLK
    return pl.pallas_call(
        functools.partial(_dsa_prompt_kernel, topk=topk),
        grid=(n, nb),
        in_specs=[
            pl.BlockSpec((1, BLK, N_IDX_HEADS * IDX_DIM), lambda b, i: (b * nb + i, 0, 0)),
            pl.BlockSpec((N_IDX_HEADS, BLK), lambda b, i: (0, b * nb + i)),
            pl.BlockSpec((1, BLK, ATT_Q), lambda b, i: (b * nb + i, 0, 0)),
            pl.BlockSpec((1, nb, BLK, LANES), lambda b, i: (b, 0, 0, 0)),
            pl.BlockSpec((1, nb, BLK, ATT_KV), lambda b, i: (b, 0, 0, 0)),
            pl.BlockSpec((1, nb, N_KV_HEADS * VT_ROWS, BLK), lambda b, i: (b, 0, 0, 0)),
            pl.BlockSpec(tri.shape, lambda b, i: (0, 0)),
        ],
        out_specs=pl.BlockSpec((1, BLK, ATT_Q), lambda b, i: (b * nb + i, 0, 0)),
        out_shape=jax.ShapeDtypeStruct((n * nb, BLK, ATT_Q), BF16),
        scratch_shapes=[
            pltpu.VMEM((nb + 1, BLK, BLK), I32),
            pltpu.VMEM((nb + 1, BLK, BLK), I16),
            pltpu.VMEM((nb + 1, BLK, BLK), I16),
            pltpu.VMEM((1, cols), F32),
            pltpu.VMEM((VT_ROWS, cols), F32),
            pltpu.VMEM((N_HEADS, BLK, ATT_KV), BF16),
            pltpu.VMEM((N_IDX_HEADS * BLK, LANES), BF16),
        ],
        compiler_params=_cparams(("parallel", "arbitrary")),
        name="dsa_prompt",
    )(iq, iwt, q, ikb4, kb4, vt4, tri)


def _sample_scores_kernel(pt_ref, iq_ref, wb_ref, *refs, pages_per_step):
    del pt_ref
    page_refs = refs[:pages_per_step]
    sc_ref = refs[pages_per_step]
    iq = iq_ref[0]
    wb = wb_ref[0]
    s_q = iq.shape[0] // N_IDX_HEADS
    for r in range(pages_per_step):
        s = _dot(iq, page_refs[r][0].astype(BF16))
        t = jnp.maximum(s, 0.0) * wb
        sc_ref[0, r] = jnp.sum(t.reshape(N_IDX_HEADS, s_q, LANES), axis=0)


def _sample_scores(page_table, iq_s, wb_s, cache_ikt, layer_off, pages_per_step):
    ns, npages = page_table.shape
    rows = iq_s.shape[1]
    s_q = rows // N_IDX_HEADS
    steps = npages // pages_per_step

    def page_spec(r):
        return pl.BlockSpec((1, IDX_DIM, PAGE_SIZE),
                            lambda b, j, pt: (layer_off + pt[b, j * pages_per_step + r], 0, 0))

    grid_spec = pltpu.PrefetchScalarGridSpec(
        num_scalar_prefetch=1,
        grid=(ns, steps),
        in_specs=[pl.BlockSpec((1, rows, IDX_DIM), lambda b, j, pt: (b, 0, 0)),
                  pl.BlockSpec((1, rows, LANES), lambda b, j, pt: (b, 0, 0))]
                 + [page_spec(r) for r in range(pages_per_step)],
        out_specs=pl.BlockSpec((1, pages_per_step, s_q, LANES), lambda b, j, pt: (b, j, 0, 0)),
    )
    return pl.pallas_call(
        functools.partial(_sample_scores_kernel, pages_per_step=pages_per_step),
        grid_spec=grid_spec,
        out_shape=jax.ShapeDtypeStruct((ns, npages, s_q, LANES), F32),
        compiler_params=_cparams(("parallel", "arbitrary")),
        name="sample_scores",
    )(page_table, iq_s, wb_s, *([cache_ikt] * pages_per_step))


def _sample_select_kernel(sc_ref, iq_ref, wb_ref, ikn_ref, tri_ref, bias_ref, sk_ref, hi_ref, lo_ref,
                          *, topk, s_q):
    sb = sc_ref.shape[0]
    npages = sc_ref.shape[1]
    rows = sb * s_q

    def key_chunk(c, carry):
        key = _sort_key(sc_ref[:, c].reshape(rows, LANES))
        sk_ref[c] = key
        hi_ref[c] = (key >> 16).astype(I16)
        return carry

    lax.fori_loop(0, npages, key_chunk, 0)
    row_id = lax.broadcasted_iota(I32, (s_q, LANES), 0)
    col_id = lax.broadcasted_iota(I32, (s_q, LANES), 1)
    new_keys = []
    for b in range(sb):
        s = _dot_nt(iq_ref[b], ikn_ref[b])
        t = jnp.maximum(s, 0.0) * wb_ref[b]
        score = jnp.sum(t.reshape(N_IDX_HEADS, s_q, LANES), axis=0)
        new_keys.append(jnp.where(col_id <= row_id, _sort_key(score), jnp.int32(INT_MIN)))
    key = jnp.concatenate(new_keys, axis=0)
    sk_ref[npages] = key
    hi_ref[npages] = (key >> 16).astype(I16)
    nch = npages + 1
    thr_b, need_b, any_tie = _kth_largest(sk_ref, hi_ref, lo_ref, nch, topk, keys_on_lanes=True, unroll=4)

    @pl.when(any_tie)
    def _():
        tri = tri_ref[...]

        def body(c, eq_before):
            bias, eq_after = _select_bias(sk_ref[c], thr_b, need_b, eq_before, tri, keys_on_lanes=True)
            bias_ref[:, c] = bias.reshape(sb, s_q, LANES)
            return eq_after

        lax.fori_loop(0, nch, body, jnp.zeros((rows, LANES), F32))

    @pl.when(jnp.logical_not(any_tie))
    def _():
        def body(c, carry):
            bias_ref[:, c] = _threshold_bias(sk_ref[c], thr_b).reshape(sb, s_q, LANES)
            return carry

        lax.fori_loop(0, nch, body, 0, unroll=4)


def _sample_select(scores, iq_s, wb_s, ikn, tri, topk, sb):
    ns, npages, s_q, _ = scores.shape
    rows = iq_s.shape[1]
    return pl.pallas_call(
        functools.partial(_sample_select_kernel, topk=topk, s_q=s_q),
        grid=(ns // sb,),
        in_specs=[
            pl.BlockSpec((sb, npages, s_q, LANES), lambda g: (g, 0, 0, 0)),
            pl.BlockSpec((sb, rows, IDX_DIM), lambda g: (g, 0, 0)),
            pl.BlockSpec((sb, rows, LANES), lambda g: (g, 0, 0)),
            pl.BlockSpec((sb, PAGE_SIZE, IDX_DIM), lambda g: (g, 0, 0)),
            pl.BlockSpec(tri.shape, lambda g: (0, 0)),
        ],
        out_specs=pl.BlockSpec((sb, npages + 1, s_q, LANES), lambda g: (g, 0, 0, 0)),
        out_shape=jax.ShapeDtypeStruct((ns, npages + 1, s_q, LANES), F32),
        scratch_shapes=[pltpu.VMEM((npages + 1, sb * s_q, LANES), dt) for dt in (I32, I16, I16)],
        compiler_params=_cparams(("parallel",)),
        name="sample_select",
    )(scores, iq_s, wb_s, ikn, tri)


def _sample_attend_kernel(pt_ref, q_ref, bias_ref, bias_new_ref, kn_ref, vn_ref, *refs, pages_per_step, s_q):
    del pt_ref
    k_refs = refs[:pages_per_step]
    v_refs = refs[pages_per_step:2 * pages_per_step]
    a_ref, m_ref, l_ref, acc_ref = refs[2 * pages_per_step:]
    j = pl.program_id(1)
    rows = N_HEADS * s_q

    @pl.when(j == 0)
    def _():
        m_ref[...] = jnp.full(m_ref.shape, MASKED, F32)
        l_ref[...] = jnp.zeros(l_ref.shape, F32)
        acc_ref[...] = jnp.zeros(acc_ref.shape, F32)

    q = q_ref[0]

    def masked(lg, bias):
        return (lg.reshape(N_HEADS, s_q, LANES) + bias[None]).reshape(rows, LANES)

    def update(lgs, weighted_values):
        top = functools.reduce(jnp.maximum, lgs)
        m_prev = m_ref[...]
        m_new = jnp.maximum(m_prev, jnp.max(top, axis=1, keepdims=True))
        alpha = jnp.exp2(m_prev - m_new)
        ps = [jnp.exp2(lg - m_new) for lg in lgs]
        l_ref[...] = alpha * l_ref[...] + jnp.sum(functools.reduce(jnp.add, ps), axis=1, keepdims=True)
        m_ref[...] = m_new
        acc_ref[...] = alpha * acc_ref[...] + weighted_values([p.astype(BF16) for p in ps])

    lgs = [masked(_dot(q, k_refs[r][0].astype(BF16)), bias_ref[0, r]) for r in range(pages_per_step)]
    update(lgs, lambda ps: functools.reduce(
        jnp.add, [_dot_nt(p, v_refs[r][0].astype(BF16)) for r, p in enumerate(ps)]))

    @pl.when(j == pl.num_programs(1) - 1)
    def _():
        lg = masked(_dot_nt(q, kn_ref[0].astype(BF16)), bias_new_ref[0, 0])
        update([lg], lambda ps: _dot(ps[0], vn_ref[0].astype(BF16)))
        a_ref[0] = (acc_ref[...] / l_ref[...]).astype(BF16)


def _sample_attend(page_table, q_s, bias, kn, vn, cache_kt, cache_vt, layer_off, pages_per_step):
    ns, npages = page_table.shape
    rows = q_s.shape[1]
    s_q = rows // N_HEADS
    steps = npages // pages_per_step

    def page_spec(r):
        return pl.BlockSpec((1, ATT_KV, PAGE_SIZE),
                            lambda b, j, pt: (layer_off + pt[b, j * pages_per_step + r], 0, 0))

    new_spec = pl.BlockSpec((1, PAGE_SIZE, ATT_KV), lambda b, j, pt: (b, 0, 0))
    grid_spec = pltpu.PrefetchScalarGridSpec(
        num_scalar_prefetch=1,
        grid=(ns, steps),
        in_specs=[pl.BlockSpec((1, rows, ATT_KV), lambda b, j, pt: (b, 0, 0)),
                  pl.BlockSpec((1, pages_per_step, s_q, LANES), lambda b, j, pt: (b, j, 0, 0)),
                  pl.BlockSpec((1, 1, s_q, LANES), lambda b, j, pt: (b, npages, 0, 0)),
                  new_spec, new_spec]
                 + [page_spec(r) for r in range(pages_per_step)] * 2,
        out_specs=pl.BlockSpec((1, rows, ATT_KV), lambda b, j, pt: (b, 0, 0)),
        scratch_shapes=[
            pltpu.VMEM((rows, 1), F32),
            pltpu.VMEM((rows, 1), F32),
            pltpu.VMEM((rows, ATT_KV), F32),
        ],
    )
    return pl.pallas_call(
        functools.partial(_sample_attend_kernel, pages_per_step=pages_per_step, s_q=s_q),
        grid_spec=grid_spec,
        out_shape=jax.ShapeDtypeStruct((ns, rows, ATT_KV), BF16),
        compiler_params=_cparams(("parallel", "arbitrary")),
        name="sample_attend",
    )(page_table, q_s, bias, bias, kn, vn,
      *([cache_kt] * pages_per_step), *([cache_vt] * pages_per_step))


def _merge_kernel(x_ref, a_ref, d_ref, u_ref, gvn_ref, wc_ref, gb_ref, wg_ref, wba_ref, pbd_ref, psc_ref,
                  wbp_ref, wbg_ref, wo_ref, lng_ref, lnb_ref, rwt_ref, rb_ref,
                  x1_ref, gate_ref, *, alpha):
    tm = x_ref.shape[0]
    x = x_ref[...]
    xb = x.astype(BF16)

    def gate(idx):
        return _sigmoid(_dot(xb, wg_ref[:, idx * D_MODEL:(idx + 1) * D_MODEL]))

    m = gate(0) * _dot(a_ref[...], wba_ref[...])
    y = _dot(d_ref[...].astype(BF16), pbd_ref[...]) * psc_ref[...]
    m = m + gate(1) * _dot(y.astype(BF16), wbp_ref[...])
    gv = gvn_ref[...].astype(BF16)
    grp = lax.broadcasted_iota(I32, (tm, GMLP_WIDTH), 1) // GMLP_GC
    mix = gb_ref[...]
    for g in range(GMLP_GROUPS):
        mix = mix + jnp.where(grp == g, _dot(wc_ref[g], gv), 0.0)
    c = u_ref[...] * mix
    m = m + gate(2) * _dot(c.astype(BF16), wbg_ref[...])
    y = _dot(m.astype(BF16), wo_ref[...])
    x1 = _layer_norm(alpha * x + y, lng_ref[...], lnb_ref[...])
    x1_ref[...] = x1

    scores = _sigmoid(_dot_nt(rwt_ref[...], x1.astype(BF16)))
    sel = scores + rb_ref[...]
    per = N_EXPERTS // N_EXPERT_GROUPS
    g3 = sel.reshape(N_EXPERT_GROUPS, per, tm)
    sub = lax.broadcasted_iota(I32, (N_EXPERT_GROUPS, per, tm), 1)
    m1 = jnp.max(g3, axis=1, keepdims=True)
    first = jnp.min(jnp.where(g3 == m1, sub, per), axis=1, keepdims=True)
    m2 = jnp.max(jnp.where(sub == first, -jnp.inf, g3), axis=1, keepdims=True)
    gs = (m1 + m2).reshape(N_EXPERT_GROUPS, tm)

    def rank_of(vals, count):
        idx = lax.broadcasted_iota(I32, vals.shape, 0)
        rank = jnp.zeros(vals.shape, I32)
        for o in range(count):
            other = vals[o:o + 1]
            beats = (other > vals) | ((other == vals) & (o < idx))
            rank = rank + jnp.where(beats, 1, 0)
        return rank

    gkeep = rank_of(gs, N_EXPERT_GROUPS) < TOPK_GROUPS
    ekeep = jnp.broadcast_to(gkeep.reshape(N_EXPERT_GROUPS, 1, tm), (N_EXPERT_GROUPS, per, tm)).reshape(N_EXPERTS, tm)
    sel = jnp.where(ekeep, sel, -jnp.inf)
    chosen = rank_of(sel, N_EXPERTS) < TOPK_EXPERTS
    wsel = jnp.where(chosen, scores, 0.0)
    gate_t = wsel / jnp.sum(wsel, axis=0, keepdims=True) * ROUTED_SCALE
    gate_ref[...] = gate_t.T


def _merge(x, a_bd, d, u, gvn, wl, tm, alpha):
    m = x.shape[0]
    row = lambda w: pl.BlockSpec((tm, w), lambda i: (i, 0))
    full = lambda a: pl.BlockSpec(a.shape, lambda i: (0,) * a.ndim)
    weights = [wl["wc"], wl["gb"], wl["wg"], wl["wba"], wl["pbd"], wl["psc"], wl["wbp"], wl["wbg"], wl["wo"],
               wl["ln1_g"], wl["ln1_b"], wl["rwt"], wl["rb"]]
    return pl.pallas_call(
        functools.partial(_merge_kernel, alpha=alpha),
        grid=(m // tm,),
        in_specs=[row(D_MODEL), row(ATT_Q), row(POOL_WIDTH), row(GMLP_WIDTH), row(GMLP_WIDTH)]
                 + [full(w) for w in weights],
        out_specs=(row(D_MODEL), row(N_EXPERTS)),
        out_shape=(jax.ShapeDtypeStruct((m, D_MODEL), F32), jax.ShapeDtypeStruct((m, N_EXPERTS), F32)),
        compiler_params=_cparams(("parallel",)),
        name="merge",
    )(x, a_bd, d, u, gvn, *weights)


def _swiglu_act(xb, w_gu):
    h = _dot(xb, w_gu)
    g = h[:, :EXPERT_FF]
    return g * _sigmoid(g) * h[:, EXPERT_FF:]


def _moe_kernel(x_ref, gate_ref, wgu_ref, wd_ref, sgu_ref, sd_ref, lng_ref, lnb_ref, o_ref, xb_ref, *, alpha):
    step = pl.program_id(1)
    tm = x_ref.shape[0]
    row_chunks = [slice(r, r + MOE_ROW_CHUNK) for r in range(0, tm, MOE_ROW_CHUNK)] if tm > MOE_ROW_CHUNK \
        else [slice(0, tm)]

    @pl.when(step == 0)
    def _():
        for rows in row_chunks:
            xb_ref[rows, :] = x_ref[rows, :].astype(BF16)
            o_ref[rows, :] = _dot(_swiglu_act(xb_ref[rows, :], sgu_ref[...]).astype(BF16), sd_ref[...])

    w_gu = [wgu_ref[j].astype(BF16) for j in range(MOE_EXPERTS_PER_STEP)]
    w_down = wd_ref[...].astype(BF16).reshape(MOE_EXPERTS_PER_STEP * EXPERT_FF, D_MODEL)
    for rows in row_chunks:
        xb = xb_ref[rows, :]
        gate = gate_ref[rows, :]
        lane = lax.broadcasted_iota(I32, gate.shape, 1)
        acts = []
        for j in range(MOE_EXPERTS_PER_STEP):
            e = step * MOE_EXPERTS_PER_STEP + j
            gcol = jnp.sum(jnp.where(lane == e, gate, 0.0), axis=1, keepdims=True)
            acts.append((gcol * _swiglu_act(xb, w_gu[j])).astype(BF16))
        o_ref[rows, :] += _dot(jnp.concatenate(acts, axis=1), w_down)

    @pl.when(step == pl.num_programs(1) - 1)
    def _():
        for rows in row_chunks:
            o_ref[rows, :] = _layer_norm(alpha * x_ref[rows, :] + o_ref[rows, :], lng_ref[...], lnb_ref[...])


def _moe(x1, gate, w_gu, w_down, layer, sh_gu, sh_down, ln_g, ln_b, tm, alpha):
    m = x1.shape[0]
    full = lambda a: pl.BlockSpec(a.shape, lambda i, e: (0,) * a.ndim)
    per = MOE_EXPERTS_PER_STEP
    steps = N_EXPERTS // per
    once = pl.Buffered(1)
    return pl.pallas_call(
        functools.partial(_moe_kernel, alpha=alpha),
        grid=(m // tm, steps),
        in_specs=[
            pl.BlockSpec((tm, D_MODEL), lambda i, e: (i, 0), pipeline_mode=once),
            pl.BlockSpec((tm, N_EXPERTS), lambda i, e: (i, 0), pipeline_mode=once),
            pl.BlockSpec((per, D_MODEL, 2 * EXPERT_FF), lambda i, e: (layer * steps + e, 0, 0)),
            pl.BlockSpec((per, EXPERT_FF, D_MODEL), lambda i, e: (layer * steps + e, 0, 0)),
            full(sh_gu), full(sh_down), full(ln_g), full(ln_b),
        ],
        out_specs=pl.BlockSpec((tm, D_MODEL), lambda i, e: (i, 0), pipeline_mode=once),
        out_shape=jax.ShapeDtypeStruct((m, D_MODEL), F32),
        scratch_shapes=[pltpu.VMEM((tm, D_MODEL), BF16)],
        compiler_params=_cparams(("parallel", "arbitrary")),
        name="moe",
    )(x1, gate, w_gu, w_down, sh_gu, sh_down, ln_g, ln_b)


def _rope_tables(pos):
    half = HEAD_DIM // 2
    inv = ROPE_THETA ** (-jnp.arange(half, dtype=F32) / half)
    ang = pos.astype(F32)[:, None] * inv[None, :]
    cos, sin = jnp.cos(ang), jnp.sin(ang)
    cos_t = jnp.tile(cos, (1, LANES // half))
    sin_t = jnp.tile(jnp.concatenate([-sin, sin], axis=1), (1, LANES // HEAD_DIM))
    return cos_t, sin_t


def _block_diag(blocks):
    g, r, c = blocks.shape
    eye = jnp.eye(g, dtype=blocks.dtype)
    return jnp.einsum("grc,gh->grhc", blocks, eye).reshape(g * r, g * c)


def _layer_weights(l, w_in, w_ba, w_bp, w_bg, w_out, pool_w, pool_scale, g_ln_g, g_ln_b, g_ws, g_b,
                   ln1_g, ln1_b, router_w, router_bias, sh_gu, sh_down, ln2_g, ln2_b):
    w = w_in[l]
    sizes = (ATT_Q, ATT_KV, ATT_KV, N_IDX_HEADS * IDX_DIM, IDX_DIM, N_IDX_HEADS,
             POOL_WIDTH, GMLP_WIDTH, GMLP_WIDTH, N_BRANCH * D_MODEL)
    offs = np.concatenate([[0], np.cumsum(sizes)]).tolist()
    wq, wk, wv, wiq, wik, wiw, wp, wu, wgv, wg = [w[:, offs[j]:offs[j + 1]] for j in range(len(sizes))]
    wq = wq * (HEAD_DIM ** -0.5 * LOG2E)
    wikw = jnp.pad(jnp.concatenate([wik, wiw], axis=1), ((0, 0), (0, LANES - IDX_DIM - N_IDX_HEADS)))
    w_cat = jnp.concatenate([wq, wk, wv, wiq, wikw, wp, wu, wgv], axis=1).astype(BF16)
    return dict(
        w_cat=w_cat, wvt=wv.T.astype(BF16), wiwt=wiw.T.astype(BF16), wg=wg.astype(BF16),
        wba=w_ba[l].astype(BF16),
        pbd=_block_diag(pool_w[l]).astype(BF16), psc=pool_scale[l].reshape(1, POOL_WIDTH),
        wbp=w_bp[l].astype(BF16), wbg=w_bg[l].astype(BF16), wo=w_out[l].astype(BF16),
        g_ln_g=g_ln_g[l].reshape(1, GMLP_WIDTH), g_ln_b=g_ln_b[l].reshape(1, GMLP_WIDTH),
        ws=g_ws[l], gbias=g_b[l],
        ln1_g=ln1_g[l].reshape(1, D_MODEL), ln1_b=ln1_b[l].reshape(1, D_MODEL),
        rwt=router_w[l].T.astype(BF16), rbias=router_bias[l],
        sh_gu=sh_gu[l].astype(BF16), sh_down=sh_down[l].astype(BF16),
        ln2_g=ln2_g[l].reshape(1, D_MODEL), ln2_b=ln2_b[l].reshape(1, D_MODEL),
    )


def _chunk_mix(wl, cl, tm):
    tril = jnp.tril(jnp.ones((cl, cl), F32))
    wm = wl["ws"][:, :cl, :cl] * tril
    reps = tm // cl
    wc = jnp.stack([_block_diag(jnp.broadcast_to(wm[g], (reps, cl, cl))) for g in range(GMLP_GROUPS)])
    gb = jnp.tile(jnp.repeat(wl["gbias"][:, :cl].T, GMLP_GC, axis=1), (reps, 1))
    return wc.astype(BF16), gb


def _heads_by_seq(a, ns, s_q):
    heads = a.shape[1] // HEAD_DIM
    return a.reshape(ns, s_q, heads, HEAD_DIM).transpose(0, 2, 1, 3).reshape(ns, heads * s_q, HEAD_DIM)


def kernel(x_prompt, x_sample, cache_k, cache_v, cache_idx_k, state_pool, page_table, w_in, w_branch_attn,
           w_branch_pool, w_branch_gmlp, w_out, pool_w, pool_scale, gmlp_ln_g, gmlp_ln_b, gmlp_ws, gmlp_b,
           ln1_g, ln1_b, router_w, router_bias, expert_w_gu, expert_w_down, shared_w_gu, shared_w_down,
           ln2_g, ln2_b):
    n_p, t_p, _ = x_prompt.shape
    n_s, t_s, _ = x_sample.shape
    depth = w_in.shape[0]
    n_pool = cache_k.shape[1]
    npages = page_table.shape[1]
    past = npages * PAGE_SIZE
    m_p, m_s = n_p * t_p, n_s * t_s
    alpha = (2 * depth) ** 0.25
    assert t_p % BLK == 0 and m_s % BLK == 0 and BLK % t_s == 0 and t_s <= 16

    tm_p = BLK
    tm_s = BLK
    tm_moe = 2048 if m_p % 2048 == 0 else tm_p
    tp_pool = 512 if t_p % 512 == 0 else BLK
    pages_per_step = 16 if npages % 16 == 0 else 1
    score_pages_per_step = 32 if npages % 32 == 0 else pages_per_step
    sel_batch = 8 if n_s % 8 == 0 else 1
    topk_p = min(TOPK_MAX, t_p // 4)
    topk_s = min(TOPK_MAX, (past + t_s) // 4)

    cos_p, sin_p = _rope_tables(jnp.arange(t_p, dtype=I32))
    cos_s, sin_s = _rope_tables(past + jnp.arange(t_s, dtype=I32))
    cos_s, sin_s = jnp.tile(cos_s, (m_s // t_s, 1)), jnp.tile(sin_s, (m_s // t_s, 1))
    tri_p = _tri_matrix(BLK, keys_on_lanes=False)
    group_of_row = jnp.asarray(np.eye(N_KV_HEADS, dtype=np.float32)[
        np.repeat(np.arange(N_HEADS) // (N_HEADS // N_KV_HEADS), t_s)])
    tri_s = _tri_matrix(LANES, keys_on_lanes=True)
    cache_kt = cache_k.transpose(0, 1, 3, 4, 2).reshape(depth * n_pool, ATT_KV, PAGE_SIZE)
    cache_vt = cache_v.transpose(0, 1, 3, 4, 2).reshape(depth * n_pool, ATT_KV, PAGE_SIZE)
    cache_ikt = cache_idx_k.transpose(0, 1, 3, 2).reshape(depth * n_pool, IDX_DIM, PAGE_SIZE)

    w_gu_all = expert_w_gu.reshape(depth * N_EXPERTS, D_MODEL, 2 * EXPERT_FF)
    w_down_all = expert_w_down.reshape(depth * N_EXPERTS, EXPERT_FF, D_MODEL)

    hp = x_prompt.reshape(m_p, D_MODEL)
    hs = x_sample.reshape(m_s, D_MODEL)
    outs = {name: [] for name in ("kp", "vp", "ikp", "pp", "ks", "vs", "iks", "ps", "gs")}
    for l in range(depth):
        wl = _layer_weights(l, w_in, w_branch_attn, w_branch_pool, w_branch_gmlp, w_out, pool_w, pool_scale,
                            gmlp_ln_g, gmlp_ln_b, gmlp_ws, gmlp_b, ln1_g, ln1_b, router_w, router_bias,
                            shared_w_gu, shared_w_down, ln2_g, ln2_b)

        def finish(x, a_bd, d, u, gvn, cl, tm, tm_e):
            wc, gb = _chunk_mix(wl, cl, tm)
            wm = dict(wl, wc=wc, gb=gb, rb=jnp.broadcast_to(wl["rbias"][:, None], (N_EXPERTS, tm)))
            x1, gate = _merge(x, a_bd, d, u, gvn, wm, tm, alpha)
            return _moe(x1, gate, w_gu_all, w_down_all, l, wl["sh_gu"], wl["sh_down"],
                        wl["ln2_g"], wl["ln2_b"], tm_e, alpha)

        q, k, v, kb, vt, iq, ikw, ikb, iwt, p, u, gvn = _proj(hp, wl, cos_p, sin_p, tm_p)
        p3 = p.reshape(n_p, t_p, POOL_WIDTH)
        d = _pool(p3, None, 0, tp_pool).reshape(m_p, POOL_WIDTH)
        a = _dsa_prompt(iq, iwt, q, ikb, kb, vt, tri_p, n_p, t_p, topk_p).reshape(m_p, ATT_Q)
        hp = finish(hp, a, d, u, gvn, CHUNK, tm_p, tm_moe)
        outs["kp"].append(k.reshape(n_p, t_p, N_KV_HEADS, HEAD_DIM))
        outs["vp"].append(v.reshape(n_p, t_p, N_KV_HEADS, HEAD_DIM))
        outs["ikp"].append(ikw[:, :IDX_DIM].reshape(n_p, t_p, IDX_DIM))
        outs["pp"].append(p3[:, t_p - POOL_STATE:])

        q, k, v, kb, vt, iq, ikw, ikb, iwt, p, u, gvn = _proj(hs, wl, cos_s, sin_s, tm_s)
        p3 = p.reshape(n_s, t_s, POOL_WIDTH)
        prefix16 = jnp.pad(state_pool[l], ((0, 0), (16 - POOL_STATE, 0), (0, 0)))
        d = _pool(p3, prefix16, past, t_s).reshape(m_s, POOL_WIDTH)
        iq_s = _heads_by_seq(iq, n_s, t_s)
        q_s = _heads_by_seq(q, n_s, t_s)
        q_s = jnp.einsum("nrc,rg->nrgc", q_s, group_of_row.astype(q_s.dtype)).reshape(n_s, -1, ATT_KV)
        wb_s = jnp.broadcast_to(iwt.reshape(N_IDX_HEADS, n_s, t_s).transpose(1, 0, 2).reshape(n_s, -1, 1),
                                (n_s, N_IDX_HEADS * t_s, LANES))
        pad_rows = lambda a: jnp.pad(a.reshape(n_s, t_s, -1), ((0, 0), (0, PAGE_SIZE - t_s), (0, 0)))
        ikn = pad_rows(ikb[:, :IDX_DIM])
        scores = _sample_scores(page_table, iq_s, wb_s, cache_ikt, l * n_pool, score_pages_per_step)
        bias = _sample_select(scores, iq_s, wb_s, ikn, tri_s, topk_s, sel_batch)
        a_s = _sample_attend(page_table, q_s, bias, pad_rows(k), pad_rows(v), cache_kt, cache_vt,
                             l * n_pool, pages_per_step)
        a_s = a_s.reshape(n_s, N_HEADS, t_s, N_KV_HEADS, HEAD_DIM)
        a_s = jnp.stack([a_s[:, h, :, h // (N_HEADS // N_KV_HEADS)] for h in range(N_HEADS)], axis=2)
        hs = finish(hs, a_s.reshape(m_s, ATT_Q), d, u, gvn, t_s, tm_s, tm_s)
        outs["ks"].append(k.reshape(n_s, t_s, N_KV_HEADS, HEAD_DIM))
        outs["vs"].append(v.reshape(n_s, t_s, N_KV_HEADS, HEAD_DIM))
        outs["iks"].append(ikw[:, :IDX_DIM].reshape(n_s, t_s, IDX_DIM))
        outs["ps"].append(jnp.concatenate([state_pool[l], p3], axis=1)[:, -POOL_STATE:])
        outs["gs"].append(gvn.reshape(n_s, t_s, GMLP_WIDTH))

    st = lambda name: jnp.stack(outs[name])
    return (hp.reshape(n_p, t_p, D_MODEL), hs.reshape(n_s, t_s, D_MODEL),
            st("kp"), st("vp"), st("ikp"), st("ks"), st("vs"), st("iks"), st("pp"), st("ps"), st("gs"))
```

```python
import functools

import jax
import jax.numpy as jnp
import numpy as np
from jax import lax
from jax.experimental import pallas as pl
from jax.experimental.pallas import tpu as pltpu

F32 = jnp.float32
BF16 = jnp.bfloat16
I32 = jnp.int32
I16 = jnp.int16

D_MODEL = 1024
N_HEADS = 8
N_KV_HEADS = 4
HEAD_DIM = 64
N_IDX_HEADS = 8
IDX_DIM = 64
TOPK_MAX = 256
PAGE_SIZE = 128
ROPE_THETA = 10000.0
POOL_WINDOWS = (2, 4, 8, 16)
POOL_WIDTH = 256
POOL_GC = 64
POOL_STATE = 15
GMLP_WIDTH = 256
GMLP_GROUPS = 4
GMLP_GC = 64
CHUNK = 128
N_BRANCH = 3
ATT_Q = N_HEADS * HEAD_DIM
ATT_KV = N_KV_HEADS * HEAD_DIM
N_EXPERTS = 64
TOPK_EXPERTS = 8
N_EXPERT_GROUPS = 8
TOPK_GROUPS = 4
EXPERT_FF = 256
ROUTED_SCALE = 2.5
LN_EPS = 1e-5

LANES = 128
SUBLANES = 8
PACK16 = 16
HALF16 = 2 ** 15
VT_ROWS = HEAD_DIM + PACK16
BLK = 256
ATTEND_HEADS_PER_STAGE = 8
MOE_EXPERTS_PER_STEP = 2
MOE_ROW_CHUNK = 1024
INT_MIN = -2 ** 31
MASKED = -1e30
LOG2E = 1.4426950408889634
VMEM_LIMIT = 56 * 1024 * 1024

C_Q = 0
C_K = C_Q + ATT_Q
C_V = C_K + ATT_KV
C_IQ = C_V + ATT_KV
C_IKW = C_IQ + N_IDX_HEADS * IDX_DIM
C_P = C_IKW + LANES
C_U = C_P + POOL_WIDTH
C_GV = C_U + GMLP_WIDTH
C_END = C_GV + GMLP_WIDTH


def _cparams(sem):
    return pltpu.CompilerParams(dimension_semantics=sem, vmem_limit_bytes=VMEM_LIMIT)


def _layer_norm(x, g, b):
    mu = jnp.mean(x, axis=-1, keepdims=True)
    xc = x - mu
    var = jnp.mean(xc * xc, axis=-1, keepdims=True)
    return xc * lax.rsqrt(var + LN_EPS) * g + b


def _sigmoid(x):
    return 1.0 / (1.0 + jnp.exp(-x))


def _dot(a, b):
    return jnp.dot(a, b, preferred_element_type=F32)


def _dot_nt(a, b):
    return lax.dot_general(a, b, (((1,), (1,)), ((), ())), preferred_element_type=F32)


def _sort_key(score):
    score = jnp.where(score == 0.0, 0.0, score)
    bits = lax.bitcast_convert_type(score, I32)
    return bits ^ ((bits >> 31) & jnp.int32(0x7FFFFFFF))


def _proj_kernel(x_ref, w_ref, wvt_ref, wiwt_ref, cos_ref, sin_ref, lng_ref, lnb_ref,
                 q_ref, k_ref, v_ref, kb_ref, vt_ref, iq_ref, ikw_ref, ikb_ref, iwt_ref,
                 p_ref, u_ref, gvn_ref):
    tm = x_ref.shape[0]
    xb = x_ref[...].astype(BF16)
    cos = cos_ref[...]
    sin = sin_ref[...]
    lane = lax.broadcasted_iota(I32, (tm, LANES), 1)
    first_half = (lane % HEAD_DIM) < (HEAD_DIM // 2)

    def mm(c0, width):
        return _dot(xb, w_ref[:, c0:c0 + width])

    def rope(z):
        partner = jnp.where(first_half, pltpu.roll(z, LANES - 32, 1), pltpu.roll(z, 32, 1))
        return z * cos + partner * sin

    def rope_wide(z):
        return jnp.concatenate([rope(z[:, s * LANES:(s + 1) * LANES]) for s in range(z.shape[1] // LANES)], axis=1)

    nblk = tm // BLK
    q_ref[...] = rope_wide(mm(C_Q, ATT_Q)).astype(BF16)
    k = rope_wide(mm(C_K, ATT_KV))
    k_ref[...] = k
    kb_ref[...] = k.astype(BF16)
    v_ref[...] = mm(C_V, ATT_KV)
    vt = _dot_nt(wvt_ref[...], xb)
    ones_rows = jnp.where(lax.broadcasted_iota(I32, (PACK16, tm), 0) == 0, 1.0, 0.0)
    vt = jnp.concatenate([piece for g in range(N_KV_HEADS)
                          for piece in (vt[g * HEAD_DIM:(g + 1) * HEAD_DIM], ones_rows)], axis=0).astype(BF16)
    for b in range(nblk):
        vt_ref[b] = vt[:, b * BLK:(b + 1) * BLK]
    iq_ref[...] = rope_wide(mm(C_IQ, N_IDX_HEADS * IDX_DIM)).astype(BF16)
    z = mm(C_IKW, LANES)
    is_key = lane < IDX_DIM
    ikw = jnp.where(is_key, rope(z), z)
    ikw_ref[...] = ikw
    ikb_ref[...] = jnp.where(is_key, ikw, 0.0).astype(BF16)
    iwt_ref[...] = _dot_nt(wiwt_ref[...], xb)
    p_ref[...] = mm(C_P, POOL_WIDTH)
    u_ref[...] = mm(C_U, GMLP_WIDTH)
    gvn_ref[...] = _layer_norm(mm(C_GV, GMLP_WIDTH), lng_ref[...], lnb_ref[...])


def _proj(x, wl, cos_t, sin_t, tm):
    m = x.shape[0]
    nt = cos_t.shape[0] // tm
    row = lambda w: pl.BlockSpec((tm, w), lambda i: (i, 0))
    full = lambda a: pl.BlockSpec(a.shape, lambda i: (0,) * a.ndim)
    tab = pl.BlockSpec((tm, LANES), lambda i: (i % nt, 0))
    nb = tm // BLK
    out_shape = (
        jax.ShapeDtypeStruct((m, ATT_Q), BF16),
        jax.ShapeDtypeStruct((m, ATT_KV), F32),
        jax.ShapeDtypeStruct((m, ATT_KV), F32),
        jax.ShapeDtypeStruct((m, ATT_KV), BF16),
        jax.ShapeDtypeStruct((m // BLK, N_KV_HEADS * VT_ROWS, BLK), BF16),
        jax.ShapeDtypeStruct((m, N_IDX_HEADS * IDX_DIM), BF16),
        jax.ShapeDtypeStruct((m, LANES), F32),
        jax.ShapeDtypeStruct((m, LANES), BF16),
        jax.ShapeDtypeStruct((N_IDX_HEADS, m), F32),
        jax.ShapeDtypeStruct((m, POOL_WIDTH), F32),
        jax.ShapeDtypeStruct((m, GMLP_WIDTH), F32),
        jax.ShapeDtypeStruct((m, GMLP_WIDTH), F32),
    )
    out_specs = (
        row(ATT_Q),
        row(ATT_KV), row(ATT_KV), row(ATT_KV),
        pl.BlockSpec((nb, N_KV_HEADS * VT_ROWS, BLK), lambda i: (i, 0, 0)),
        row(N_IDX_HEADS * IDX_DIM),
        row(LANES), row(LANES),
        pl.BlockSpec((N_IDX_HEADS, tm), lambda i: (0, i)),
        row(POOL_WIDTH), row(GMLP_WIDTH), row(GMLP_WIDTH),
    )
    weights = [wl["w_cat"], wl["wvt"], wl["wiwt"]]
    return pl.pallas_call(
        _proj_kernel,
        grid=(m // tm,),
        in_specs=[row(D_MODEL)] + [full(w) for w in weights] + [tab, tab, full(wl["g_ln_g"]), full(wl["g_ln_b"])],
        out_specs=out_specs,
        out_shape=out_shape,
        compiler_params=_cparams(("parallel",)),
        name="proj",
    )(x, *weights, cos_t, sin_t, wl["g_ln_g"], wl["g_ln_b"])


def _pool_kernel(p_ref, halo_ref, d_ref, ext_ref, s2_ref, s4_ref, s8_ref, *, pos_base, halo_is_prefix):
    tp = p_ref.shape[1]
    r_end = tp + 32
    i = pl.program_id(1)
    p = p_ref[0]
    halo = halo_ref[0]
    if not halo_is_prefix:
        halo = jnp.where(i == 0, 0.0, halo)
    ext_ref[0:16, :] = jnp.zeros((16, POOL_WIDTH), F32)
    ext_ref[16:32, :] = halo
    ext_ref[32:r_end, :] = p
    s2_ref[8:r_end, :] = ext_ref[8:r_end, :] + ext_ref[7:r_end - 1, :]
    s4_ref[16:r_end, :] = s2_ref[16:r_end, :] + s2_ref[14:r_end - 2, :]
    s8_ref[24:r_end, :] = s4_ref[24:r_end, :] + s4_ref[20:r_end - 4, :]
    s16 = s8_ref[32:r_end, :] + s8_ref[24:r_end - 8, :]
    lane = lax.broadcasted_iota(I32, (tp, POOL_WIDTH), 1)
    grp = lane // POOL_GC
    win = jnp.where(grp == 0, s2_ref[32:r_end, :],
                    jnp.where(grp == 1, s4_ref[32:r_end, :],
                              jnp.where(grp == 2, s8_ref[32:r_end, :], s16)))
    width = jnp.where(grp == 0, POOL_WINDOWS[0],
                      jnp.where(grp == 1, POOL_WINDOWS[1],
                                jnp.where(grp == 2, POOL_WINDOWS[2], POOL_WINDOWS[3])))
    pos = pos_base + i * tp + lax.broadcasted_iota(I32, (tp, POOL_WIDTH), 0)
    cnt = jnp.minimum(width, pos + 1).astype(F32)
    d_ref[0] = win / cnt - p


def _pool(p3, prefix16, pos_base, tp):
    n, t, _ = p3.shape
    halo_is_prefix = prefix16 is not None
    if halo_is_prefix:
        halo = prefix16
        halo_spec = pl.BlockSpec((1, 16, POOL_WIDTH), lambda b, i: (b, 0, 0))
    else:
        halo = p3
        step = tp // 16
        halo_spec = pl.BlockSpec((1, 16, POOL_WIDTH), lambda b, i: (b, jnp.maximum(i * step - 1, 0), 0))
    rows = tp + 32
    return pl.pallas_call(
        functools.partial(_pool_kernel, pos_base=pos_base, halo_is_prefix=halo_is_prefix),
        grid=(n, t // tp),
        in_specs=[pl.BlockSpec((1, tp, POOL_WIDTH), lambda b, i: (b, i, 0)), halo_spec],
        out_specs=pl.BlockSpec((1, tp, POOL_WIDTH), lambda b, i: (b, i, 0)),
        out_shape=jax.ShapeDtypeStruct((n, t, POOL_WIDTH), F32),
        scratch_shapes=[pltpu.VMEM((rows, POOL_WIDTH), F32) for _ in range(4)],
        compiler_params=_cparams(("parallel", "parallel")),
        name="pool",
    )(p3, halo)


def _kth_largest(sk_ref, hi_ref, lo_ref, nch, topk, keys_on_lanes, unroll=1):
    rows, cols = sk_ref.shape[1:]
    if keys_on_lanes:
        vec = (rows, 1)
        fold, acc_shape = (lambda x: x), (rows, cols)
        fold16, acc16_shape = fold, acc_shape
        total = lambda cnt: jnp.sum(cnt.astype(F32), axis=1, keepdims=True)
    else:
        vec = (1, cols)
        fold, acc_shape = (lambda x: jnp.sum(x.reshape(rows // SUBLANES, SUBLANES, cols), axis=0)), (SUBLANES, cols)
        fold16 = lambda x: functools.reduce(jnp.add, [x[g * PACK16:(g + 1) * PACK16] for g in range(rows // PACK16)])
        acc16_shape = (PACK16, cols)
        total = lambda cnt: jnp.sum(cnt.astype(F32), axis=0, keepdims=True)

    if not isinstance(nch, int):
        sk_ref[nch] = jnp.full((rows, cols), INT_MIN, I32)
        hi_ref[nch] = jnp.full((rows, cols), -HALF16, I16)
        lo_ref[nch] = jnp.full((rows, cols), -HALF16, I16)

    def over_chunks(body, init):
        if isinstance(nch, int):
            return lax.fori_loop(0, nch, body, init, unroll=unroll)
        return lax.fori_loop(0, (nch + 1) // 2, lambda j, carry: body(2 * j + 1, body(2 * j, carry)), init)

    def count(pred_fn):
        def body(c, cnt):
            return cnt + fold(jnp.where(pred_fn(sk_ref[c]), 1, 0))
        return total(over_chunks(body, jnp.zeros(acc_shape, I32)))

    def count16(ref, pred_fn):
        def body(c, cnt):
            return cnt + fold16(jnp.where(pred_fn(ref[c]), jnp.int16(1), jnp.int16(0)))
        return total(over_chunks(body, jnp.zeros(acc16_shape, I16)))

    def broadcast16(v):
        return jnp.broadcast_to(v.astype(I16), (rows, cols))

    def search16(ref, wanted):
        def bit_body(it, ubits):
            cand = ubits | lax.shift_left(jnp.int32(1), 15 - it)
            cand_b = broadcast16(cand - HALF16)
            tot = count16(ref, lambda s: s >= cand_b)
            return jnp.where(tot >= wanted, cand, ubits)
        return lax.fori_loop(0, 16, bit_body, jnp.zeros(vec, I32))

    hi = search16(hi_ref, topk) - HALF16
    hi_b = broadcast16(hi)
    wanted_lo = topk - count16(hi_ref, lambda s: s > hi_b)

    def low_halves(c, carry):
        low = ((sk_ref[c] & jnp.int32(0xFFFF)) - HALF16).astype(I16)
        lo_ref[c] = jnp.where(hi_ref[c] == hi_b, low, jnp.int16(-HALF16))
        return carry

    lax.fori_loop(0, nch, low_halves, 0)
    thr = hi * (2 * HALF16) + search16(lo_ref, wanted_lo)
    thr_b = jnp.broadcast_to(thr, (rows, cols))
    need_b = jnp.broadcast_to(topk - count(lambda s: s > thr_b), (rows, cols))
    surplus = jnp.where(thr == jnp.int32(INT_MIN), 0.0, count(lambda s: s >= thr_b) - topk)
    return jnp.maximum(thr_b, jnp.int32(INT_MIN + 1)), need_b, jnp.max(surplus) > 0.0


def _threshold_bias(skc, thr_b):
    return jnp.where(skc >= thr_b, 0.0, MASKED)


def _select_bias(skc, thr_b, need_b, eq_before, tri, keys_on_lanes):
    rows, cols = skc.shape
    eq = skc == thr_b
    eqf = jnp.where(eq, 1.0, 0.0).astype(BF16)
    if keys_on_lanes:
        res = _dot(eqf, tri)
        prefix, chunk_total = res[:, :cols], res[:, cols:]
    else:
        res = _dot(tri, eqf)
        prefix, chunk_total = res[:rows], res[rows:]
    keep = (skc > thr_b) | (eq & (prefix + eq_before <= need_b))
    return jnp.where(keep, 0.0, MASKED), eq_before + chunk_total


def _tri_matrix(n, keys_on_lanes):
    r = np.arange(n)
    ones = np.ones((n, n), np.float32)
    if keys_on_lanes:
        return jnp.asarray(np.concatenate([(r[:, None] <= r[None, :]).astype(np.float32), ones], axis=1), BF16)
    return jnp.asarray(np.concatenate([(r[:, None] >= r[None, :]).astype(np.float32), ones], axis=0), BF16)


def _pad_heads(q, iq, qpad_ref, iqpad_ref):
    rows = q.shape[0]
    low_half = lax.broadcasted_iota(I32, (rows, LANES), 1) < HEAD_DIM
    zero_slab = jnp.zeros((rows, LANES), F32)

    def head_slabs(z):
        z = z.astype(F32)
        for s in range(z.shape[1] // LANES):
            slab = z[:, s * LANES:(s + 1) * LANES]
            swapped = pltpu.roll(slab, HEAD_DIM, 1)
            for r in range(2):
                yield 2 * s + r, (slab, swapped) if r == 0 else (swapped, slab)

    for h, (head_low, head_high) in head_slabs(q):
        g = h // (N_HEADS // N_KV_HEADS)
        own = jnp.where(low_half, head_low, 0.0) if g % 2 == 0 else jnp.where(low_half, 0.0, head_high)
        qpad_ref[h] = jnp.concatenate([own, zero_slab] if g // 2 == 0 else [zero_slab, own], axis=1).astype(BF16)
    for h, (head_low, _) in head_slabs(iq):
        iqpad_ref[h * rows:(h + 1) * rows, :] = jnp.where(low_half, head_low, 0.0).astype(BF16)


def _dsa_prompt_kernel(iq_ref, iwt_ref, q_ref, ikb_ref, kb_ref, vt_ref, tri_ref, a_ref,
                       sk_ref, hi_ref, lo_ref, m_ref, acc_ref, qpad_ref, iqpad_ref, *, topk):
    i = pl.program_id(1)
    nch = i + 1
    _pad_heads(q_ref[0], iq_ref[0], qpad_ref, iqpad_ref)
    iwt = iwt_ref[...]
    key_id = lax.broadcasted_iota(I32, (BLK, BLK), 0)
    q_id = lax.broadcasted_iota(I32, (BLK, BLK), 1)

    def score_chunk(c, carry):
        ikc = ikb_ref[0, c]
        score = jnp.zeros((BLK, BLK), F32)
        for h in range(N_IDX_HEADS):
            s = _dot_nt(ikc, iqpad_ref[h * BLK:(h + 1) * BLK, :])
            score = score + jnp.maximum(s, 0.0) * iwt[h:h + 1, :]
        key = jnp.where(key_id > q_id + (i - c) * BLK, jnp.int32(INT_MIN), _sort_key(score))
        sk_ref[c] = key
        hi_ref[c] = (key >> 16).astype(I16)
        return carry

    lax.fori_loop(0, (nch + 1) // 2, lambda j, carry: score_chunk(2 * j + 1, score_chunk(2 * j, carry)), 0)
    thr_b, need_b, any_tie = _kth_largest(sk_ref, hi_ref, lo_ref, nch, topk, keys_on_lanes=False)

    m_ref[...] = jnp.full(m_ref.shape, MASKED, F32)
    acc_ref[...] = jnp.zeros(acc_ref.shape, F32)
    heads_per_group = N_HEADS // N_KV_HEADS

    def attend_chunk(c, bias):
        kc = kb_ref[0, c]
        cols = [slice(h * BLK, (h + 1) * BLK) for h in range(N_HEADS)]
        for h0 in range(0, N_HEADS, ATTEND_HEADS_PER_STAGE):
            heads = range(h0, h0 + ATTEND_HEADS_PER_STAGE)
            lgs = {h: _dot_nt(kc, qpad_ref[h]) + bias for h in heads}
            m_prev = {h: m_ref[:, cols[h]] for h in heads}
            m_new = {h: jnp.maximum(m_prev[h], jnp.max(lgs[h], axis=0, keepdims=True)) for h in heads}
            alpha = {h: jnp.exp2(m_prev[h] - m_new[h]) for h in heads}
            ps = {h: jnp.exp2(lgs[h] - m_new[h]) for h in heads}
            for h in heads:
                m_ref[:, cols[h]] = m_new[h]
            for h in heads:
                g = h // heads_per_group
                vg = vt_ref[0, c, g * VT_ROWS:(g + 1) * VT_ROWS, :]
                acc_ref[:, cols[h]] = alpha[h] * acc_ref[:, cols[h]] + _dot(vg, ps[h].astype(BF16))

    @pl.when(any_tie)
    def _():
        tri = tri_ref[...]

        def body(c, eq_before):
            bias, eq_after = _select_bias(sk_ref[c], thr_b, need_b, eq_before, tri, keys_on_lanes=False)
            attend_chunk(c, bias)
            return eq_after

        lax.fori_loop(0, nch, body, jnp.zeros((BLK, BLK), F32))

    @pl.when(jnp.logical_not(any_tie))
    def _():
        def body(c, carry):
            attend_chunk(c, _threshold_bias(sk_ref[c], thr_b))
            return carry

        lax.fori_loop(0, nch // 2, lambda j, carry: body(2 * j + 1, body(2 * j, carry)), 0)

        @pl.when(nch % 2 == 1)
        def _():
            body(nch - 1, 0)

    a_t = acc_ref[0:HEAD_DIM, :] / acc_ref[HEAD_DIM:HEAD_DIM + 1, :]
    a_t = jnp.concatenate([a_t[:, h * BLK:(h + 1) * BLK] for h in range(N_HEADS)], axis=0)
    a_ref[0] = a_t.T.astype(BF16)


def _dsa_prompt(iq, iwt, q, ikb, kb, vt, tri, n, t, topk):
    nb = t // BLK
    assert nb % 2 == 0, "the score loop reads key chunks in pairs"
    iq = iq.reshape(n * nb, BLK, N_IDX_HEADS * IDX_DIM)
    q = q.reshape(n * nb, BLK, ATT_Q)
    ikb4 = ikb.reshape(n, nb, BLK, LANES)
    kb4 = kb.reshape(n, nb, BLK, ATT_KV)
    vt4 = vt.reshape(n, nb, N_KV_HEADS * VT_ROWS, BLK)
    cols = N_HEADS * BLK
    return pl.pallas_call(
        functools.partial(_dsa_prompt_kernel, topk=topk),
        grid=(n, nb),
        in_specs=[
            pl.BlockSpec((1, BLK, N_IDX_HEADS * IDX_DIM), lambda b, i: (b * nb + i, 0, 0)),
            pl.BlockSpec((N_IDX_HEADS, BLK), lambda b, i: (0, b * nb + i)),
            pl.BlockSpec((1, BLK, ATT_Q), lambda b, i: (b * nb + i, 0, 0)),
            pl.BlockSpec((1, nb, BLK, LANES), lambda b, i: (b, 0, 0, 0)),
            pl.BlockSpec((1, nb, BLK, ATT_KV), lambda b, i: (b, 0, 0, 0)),
            pl.BlockSpec((1, nb, N_KV_HEADS * VT_ROWS, BLK), lambda b, i: (b, 0, 0, 0)),
            pl.BlockSpec(tri.shape, lambda b, i: (0, 0)),
        ],
        out_specs=pl.BlockSpec((1, BLK, ATT_Q), lambda b, i: (b * nb + i, 0, 0)),
        out_shape=jax.ShapeDtypeStruct((n * nb, BLK, ATT_Q), BF16),
        scratch_shapes=[
            pltpu.VMEM((nb + 1, BLK, BLK), I32),
            pltpu.VMEM((nb + 1, BLK, BLK), I16),
            pltpu.VMEM((nb + 1, BLK, BLK), I16),
            pltpu.VMEM((1, cols), F32),
            pltpu.VMEM((VT_ROWS, cols), F32),
            pltpu.VMEM((N_HEADS, BLK, ATT_KV), BF16),
            pltpu.VMEM((N_IDX_HEADS * BLK, LANES), BF16),
        ],
        compiler_params=_cparams(("parallel", "arbitrary")),
        name="dsa_prompt",
    )(iq, iwt, q, ikb4, kb4, vt4, tri)


def _sample_scores_kernel(pt_ref, iq_ref, wb_ref, *refs, pages_per_step):
    del pt_ref
    page_refs = refs[:pages_per_step]
    sc_ref = refs[pages_per_step]
    iq = iq_ref[0]
    wb = wb_ref[0]
    s_q = iq.shape[0] // N_IDX_HEADS
    for r in range(pages_per_step):
        s = _dot(iq, page_refs[r][0].astype(BF16))
        t = jnp.maximum(s, 0.0) * wb
        sc_ref[0, r] = jnp.sum(t.reshape(N_IDX_HEADS, s_q, LANES), axis=0)


def _sample_scores(page_table, iq_s, wb_s, cache_ikt, layer_off, pages_per_step):
    ns, npages = page_table.shape
    rows = iq_s.shape[1]
    s_q = rows // N_IDX_HEADS
    steps = npages // pages_per_step

    def page_spec(r):
        return pl.BlockSpec((1, IDX_DIM, PAGE_SIZE),
                            lambda b, j, pt: (layer_off + pt[b, j * pages_per_step + r], 0, 0))

    grid_spec = pltpu.PrefetchScalarGridSpec(
        num_scalar_prefetch=1,
        grid=(ns, steps),
        in_specs=[pl.BlockSpec((1, rows, IDX_DIM), lambda b, j, pt: (b, 0, 0)),
                  pl.BlockSpec((1, rows, LANES), lambda b, j, pt: (b, 0, 0))]
                 + [page_spec(r) for r in range(pages_per_step)],
        out_specs=pl.BlockSpec((1, pages_per_step, s_q, LANES), lambda b, j, pt: (b, j, 0, 0)),
    )
    return pl.pallas_call(
        functools.partial(_sample_scores_kernel, pages_per_step=pages_per_step),
        grid_spec=grid_spec,
        out_shape=jax.ShapeDtypeStruct((ns, npages, s_q, LANES), F32),
        compiler_params=_cparams(("parallel", "arbitrary")),
        name="sample_scores",
    )(page_table, iq_s, wb_s, *([cache_ikt] * pages_per_step))


def _sample_select_kernel(sc_ref, iq_ref, wb_ref, ikn_ref, tri_ref, bias_ref, sk_ref, hi_ref, lo_ref,
                          *, topk, s_q):
    sb = sc_ref.shape[0]
    npages = sc_ref.shape[1]
    rows = sb * s_q

    def key_chunk(c, carry):
        key = _sort_key(sc_ref[:, c].reshape(rows, LANES))
        sk_ref[c] = key
        hi_ref[c] = (key >> 16).astype(I16)
        return carry

    lax.fori_loop(0, npages, key_chunk, 0)
    row_id = lax.broadcasted_iota(I32, (s_q, LANES), 0)
    col_id = lax.broadcasted_iota(I32, (s_q, LANES), 1)
    new_keys = []
    for b in range(sb):
        s = _dot_nt(iq_ref[b], ikn_ref[b])
        t = jnp.maximum(s, 0.0) * wb_ref[b]
        score = jnp.sum(t.reshape(N_IDX_HEADS, s_q, LANES), axis=0)
        new_keys.append(jnp.where(col_id <= row_id, _sort_key(score), jnp.int32(INT_MIN)))
    key = jnp.concatenate(new_keys, axis=0)
    sk_ref[npages] = key
    hi_ref[npages] = (key >> 16).astype(I16)
    nch = npages + 1
    thr_b, need_b, any_tie = _kth_largest(sk_ref, hi_ref, lo_ref, nch, topk, keys_on_lanes=True, unroll=4)

    @pl.when(any_tie)
    def _():
        tri = tri_ref[...]

        def body(c, eq_before):
            bias, eq_after = _select_bias(sk_ref[c], thr_b, need_b, eq_before, tri, keys_on_lanes=True)
            bias_ref[:, c] = bias.reshape(sb, s_q, LANES)
            return eq_after

        lax.fori_loop(0, nch, body, jnp.zeros((rows, LANES), F32))

    @pl.when(jnp.logical_not(any_tie))
    def _():
        def body(c, carry):
            bias_ref[:, c] = _threshold_bias(sk_ref[c], thr_b).reshape(sb, s_q, LANES)
            return carry

        lax.fori_loop(0, nch, body, 0, unroll=4)


def _sample_select(scores, iq_s, wb_s, ikn, tri, topk, sb):
    ns, npages, s_q, _ = scores.shape
    rows = iq_s.shape[1]
    return pl.pallas_call(
        functools.partial(_sample_select_kernel, topk=topk, s_q=s_q),
        grid=(ns // sb,),
        in_specs=[
            pl.BlockSpec((sb, npages, s_q, LANES), lambda g: (g, 0, 0, 0)),
            pl.BlockSpec((sb, rows, IDX_DIM), lambda g: (g, 0, 0)),
            pl.BlockSpec((sb, rows, LANES), lambda g: (g, 0, 0)),
            pl.BlockSpec((sb, PAGE_SIZE, IDX_DIM), lambda g: (g, 0, 0)),
            pl.BlockSpec(tri.shape, lambda g: (0, 0)),
        ],
        out_specs=pl.BlockSpec((sb, npages + 1, s_q, LANES), lambda g: (g, 0, 0, 0)),
        out_shape=jax.ShapeDtypeStruct((ns, npages + 1, s_q, LANES), F32),
        scratch_shapes=[pltpu.VMEM((npages + 1, sb * s_q, LANES), dt) for dt in (I32, I16, I16)],
        compiler_params=_cparams(("parallel",)),
        name="sample_select",
    )(scores, iq_s, wb_s, ikn, tri)


def _sample_attend_kernel(pt_ref, q_ref, bias_ref, bias_new_ref, kn_ref, vn_ref, *refs, pages_per_step, s_q):
    del pt_ref
    k_refs = refs[:pages_per_step]
    v_refs = refs[pages_per_step:2 * pages_per_step]
    a_ref, m_ref, l_ref, acc_ref = refs[2 * pages_per_step:]
    j = pl.program_id(1)
    rows = N_HEADS * s_q

    @pl.when(j == 0)
    def _():
        m_ref[...] = jnp.full(m_ref.shape, MASKED, F32)
        l_ref[...] = jnp.zeros(l_ref.shape, F32)
        acc_ref[...] = jnp.zeros(acc_ref.shape, F32)

    q = q_ref[0]

    def masked(lg, bias):
        return (lg.reshape(N_HEADS, s_q, LANES) + bias[None]).reshape(rows, LANES)

    def update(lgs, weighted_values):
        top = functools.reduce(jnp.maximum, lgs)
        m_prev = m_ref[...]
        m_new = jnp.maximum(m_prev, jnp.max(top, axis=1, keepdims=True))
        alpha = jnp.exp2(m_prev - m_new)
        ps = [jnp.exp2(lg - m_new) for lg in lgs]
        l_ref[...] = alpha * l_ref[...] + jnp.sum(functools.reduce(jnp.add, ps), axis=1, keepdims=True)
        m_ref[...] = m_new
        acc_ref[...] = alpha * acc_ref[...] + weighted_values([p.astype(BF16) for p in ps])

    lgs = [masked(_dot(q, k_refs[r][0].astype(BF16)), bias_ref[0, r]) for r in range(pages_per_step)]
    update(lgs, lambda ps: functools.reduce(
        jnp.add, [_dot_nt(p, v_refs[r][0].astype(BF16)) for r, p in enumerate(ps)]))

    @pl.when(j == pl.num_programs(1) - 1)
    def _():
        lg = masked(_dot_nt(q, kn_ref[0].astype(BF16)), bias_new_ref[0, 0])
        update([lg], lambda ps: _dot(ps[0], vn_ref[0].astype(BF16)))
        a_ref[0] = (acc_ref[...] / l_ref[...]).astype(BF16)


def _sample_attend(page_table, q_s, bias, kn, vn, cache_kt, cache_vt, layer_off, pages_per_step):
    ns, npages = page_table.shape
    rows = q_s.shape[1]
    s_q = rows // N_HEADS
    steps = npages // pages_per_step

    def page_spec(r):
        return pl.BlockSpec((1, ATT_KV, PAGE_SIZE),
                            lambda b, j, pt: (layer_off + pt[b, j * pages_per_step + r], 0, 0))

    new_spec = pl.BlockSpec((1, PAGE_SIZE, ATT_KV), lambda b, j, pt: (b, 0, 0))
    grid_spec = pltpu.PrefetchScalarGridSpec(
        num_scalar_prefetch=1,
        grid=(ns, steps),
        in_specs=[pl.BlockSpec((1, rows, ATT_KV), lambda b, j, pt: (b, 0, 0)),
                  pl.BlockSpec((1, pages_per_step, s_q, LANES), lambda b, j, pt: (b, j, 0, 0)),
                  pl.BlockSpec((1, 1, s_q, LANES), lambda b, j, pt: (b, npages, 0, 0)),
                  new_spec, new_spec]
                 + [page_spec(r) for r in range(pages_per_step)] * 2,
        out_specs=pl.BlockSpec((1, rows, ATT_KV), lambda b, j, pt: (b, 0, 0)),
        scratch_shapes=[
            pltpu.VMEM((rows, 1), F32),
            pltpu.VMEM((rows, 1), F32),
            pltpu.VMEM((rows, ATT_KV), F32),
        ],
    )
    return pl.pallas_call(
        functools.partial(_sample_attend_kernel, pages_per_step=pages_per_step, s_q=s_q),
        grid_spec=grid_spec,
        out_shape=jax.ShapeDtypeStruct((ns, rows, ATT_KV), BF16),
        compiler_params=_cparams(("parallel", "arbitrary")),
        name="sample_attend",
    )(page_table, q_s, bias, bias, kn, vn,
      *([cache_kt] * pages_per_step), *([cache_vt] * pages_per_step))


def _merge_kernel(x_ref, a_ref, d_ref, u_ref, gvn_ref, wc_ref, gb_ref, wg_ref, wba_ref, pbd_ref, psc_ref,
                  wbp_ref, wbg_ref, wo_ref, lng_ref, lnb_ref, rwt_ref, rb_ref,
                  x1_ref, gate_ref, *, alpha):
    tm = x_ref.shape[0]
    x = x_ref[...]
    xb = x.astype(BF16)

    def gate(idx):
        return _sigmoid(_dot(xb, wg_ref[:, idx * D_MODEL:(idx + 1) * D_MODEL]))

    m = gate(0) * _dot(a_ref[...], wba_ref[...])
    y = _dot(d_ref[...].astype(BF16), pbd_ref[...]) * psc_ref[...]
    m = m + gate(1) * _dot(y.astype(BF16), wbp_ref[...])
    gv = gvn_ref[...].astype(BF16)
    grp = lax.broadcasted_iota(I32, (tm, GMLP_WIDTH), 1) // GMLP_GC
    mix = gb_ref[...]
    for g in range(GMLP_GROUPS):
        mix = mix + jnp.where(grp == g, _dot(wc_ref[g], gv), 0.0)
    c = u_ref[...] * mix
    m = m + gate(2) * _dot(c.astype(BF16), wbg_ref[...])
    y = _dot(m.astype(BF16), wo_ref[...])
    x1 = _layer_norm(alpha * x + y, lng_ref[...], lnb_ref[...])
    x1_ref[...] = x1

    scores = _sigmoid(_dot_nt(rwt_ref[...], x1.astype(BF16)))
    sel = scores + rb_ref[...]
    per = N_EXPERTS // N_EXPERT_GROUPS
    g3 = sel.reshape(N_EXPERT_GROUPS, per, tm)
    sub = lax.broadcasted_iota(I32, (N_EXPERT_GROUPS, per, tm), 1)
    m1 = jnp.max(g3, axis=1, keepdims=True)
    first = jnp.min(jnp.where(g3 == m1, sub, per), axis=1, keepdims=True)
    m2 = jnp.max(jnp.where(sub == first, -jnp.inf, g3), axis=1, keepdims=True)
    gs = (m1 + m2).reshape(N_EXPERT_GROUPS, tm)

    def rank_of(vals, count):
        idx = lax.broadcasted_iota(I32, vals.shape, 0)
        rank = jnp.zeros(vals.shape, I32)
        for o in range(count):
            other = vals[o:o + 1]
            beats = (other > vals) | ((other == vals) & (o < idx))
            rank = rank + jnp.where(beats, 1, 0)
        return rank

    gkeep = rank_of(gs, N_EXPERT_GROUPS) < TOPK_GROUPS
    ekeep = jnp.broadcast_to(gkeep.reshape(N_EXPERT_GROUPS, 1, tm), (N_EXPERT_GROUPS, per, tm)).reshape(N_EXPERTS, tm)
    sel = jnp.where(ekeep, sel, -jnp.inf)
    chosen = rank_of(sel, N_EXPERTS) < TOPK_EXPERTS
    wsel = jnp.where(chosen, scores, 0.0)
    gate_t = wsel / jnp.sum(wsel, axis=0, keepdims=True) * ROUTED_SCALE
    gate_ref[...] = gate_t.T


def _merge(x, a_bd, d, u, gvn, wl, tm, alpha):
    m = x.shape[0]
    row = lambda w: pl.BlockSpec((tm, w), lambda i: (i, 0))
    full = lambda a: pl.BlockSpec(a.shape, lambda i: (0,) * a.ndim)
    weights = [wl["wc"], wl["gb"], wl["wg"], wl["wba"], wl["pbd"], wl["psc"], wl["wbp"], wl["wbg"], wl["wo"],
               wl["ln1_g"], wl["ln1_b"], wl["rwt"], wl["rb"]]
    return pl.pallas_call(
        functools.partial(_merge_kernel, alpha=alpha),
        grid=(m // tm,),
        in_specs=[row(D_MODEL), row(ATT_Q), row(POOL_WIDTH), row(GMLP_WIDTH), row(GMLP_WIDTH)]
                 + [full(w) for w in weights],
        out_specs=(row(D_MODEL), row(N_EXPERTS)),
        out_shape=(jax.ShapeDtypeStruct((m, D_MODEL), F32), jax.ShapeDtypeStruct((m, N_EXPERTS), F32)),
        compiler_params=_cparams(("parallel",)),
        name="merge",
    )(x, a_bd, d, u, gvn, *weights)


def _swiglu_act(xb, w_gu):
    h = _dot(xb, w_gu)
    g = h[:, :EXPERT_FF]
    return g * _sigmoid(g) * h[:, EXPERT_FF:]


def _moe_kernel(x_ref, gate_ref, wgu_ref, wd_ref, sgu_ref, sd_ref, lng_ref, lnb_ref, o_ref, xb_ref, *, alpha):
    step = pl.program_id(1)
    tm = x_ref.shape[0]
    row_chunks = [slice(r, r + MOE_ROW_CHUNK) for r in range(0, tm, MOE_ROW_CHUNK)] if tm > MOE_ROW_CHUNK \
        else [slice(0, tm)]

    @pl.when(step == 0)
    def _():
        for rows in row_chunks:
            xb_ref[rows, :] = x_ref[rows, :].astype(BF16)
            o_ref[rows, :] = _dot(_swiglu_act(xb_ref[rows, :], sgu_ref[...]).astype(BF16), sd_ref[...])

    w_gu = [wgu_ref[j].astype(BF16) for j in range(MOE_EXPERTS_PER_STEP)]
    w_down = wd_ref[...].astype(BF16).reshape(MOE_EXPERTS_PER_STEP * EXPERT_FF, D_MODEL)
    for rows in row_chunks:
        xb = xb_ref[rows, :]
        gate = gate_ref[rows, :]
        lane = lax.broadcasted_iota(I32, gate.shape, 1)
        acts = []
        for j in range(MOE_EXPERTS_PER_STEP):
            e = step * MOE_EXPERTS_PER_STEP + j
            gcol = jnp.sum(jnp.where(lane == e, gate, 0.0), axis=1, keepdims=True)
            acts.append((gcol * _swiglu_act(xb, w_gu[j])).astype(BF16))
        o_ref[rows, :] += _dot(jnp.concatenate(acts, axis=1), w_down)

    @pl.when(step == pl.num_programs(1) - 1)
    def _():
        for rows in row_chunks:
            o_ref[rows, :] = _layer_norm(alpha * x_ref[rows, :] + o_ref[rows, :], lng_ref[...], lnb_ref[...])


def _moe(x1, gate, w_gu, w_down, layer, sh_gu, sh_down, ln_g, ln_b, tm, alpha):
    m = x1.shape[0]
    full = lambda a: pl.BlockSpec(a.shape, lambda i, e: (0,) * a.ndim)
    per = MOE_EXPERTS_PER_STEP
    steps = N_EXPERTS // per
    once = pl.Buffered(1)
    return pl.pallas_call(
        functools.partial(_moe_kernel, alpha=alpha),
        grid=(m // tm, steps),
        in_specs=[
            pl.BlockSpec((tm, D_MODEL), lambda i, e: (i, 0), pipeline_mode=once),
            pl.BlockSpec((tm, N_EXPERTS), lambda i, e: (i, 0), pipeline_mode=once),
            pl.BlockSpec((per, D_MODEL, 2 * EXPERT_FF), lambda i, e: (layer * steps + e, 0, 0)),
            pl.BlockSpec((per, EXPERT_FF, D_MODEL), lambda i, e: (layer * steps + e, 0, 0)),
            full(sh_gu), full(sh_down), full(ln_g), full(ln_b),
        ],
        out_specs=pl.BlockSpec((tm, D_MODEL), lambda i, e: (i, 0), pipeline_mode=once),
        out_shape=jax.ShapeDtypeStruct((m, D_MODEL), F32),
        scratch_shapes=[pltpu.VMEM((tm, D_MODEL), BF16)],
        compiler_params=_cparams(("parallel", "arbitrary")),
        name="moe",
    )(x1, gate, w_gu, w_down, sh_gu, sh_down, ln_g, ln_b)


def _rope_tables(pos):
    half = HEAD_DIM // 2
    inv = ROPE_THETA ** (-jnp.arange(half, dtype=F32) / half)
    ang = pos.astype(F32)[:, None] * inv[None, :]
    cos, sin = jnp.cos(ang), jnp.sin(ang)
    cos_t = jnp.tile(cos, (1, LANES // half))
    sin_t = jnp.tile(jnp.concatenate([-sin, sin], axis=1), (1, LANES // HEAD_DIM))
    return cos_t, sin_t


def _block_diag(blocks):
    g, r, c = blocks.shape
    eye = jnp.eye(g, dtype=blocks.dtype)
    return jnp.einsum("grc,gh->grhc", blocks, eye).reshape(g * r, g * c)


def _layer_weights(l, w_in, w_ba, w_bp, w_bg, w_out, pool_w, pool_scale, g_ln_g, g_ln_b, g_ws, g_b,
                   ln1_g, ln1_b, router_w, router_bias, sh_gu, sh_down, ln2_g, ln2_b):
    w = w_in[l]
    sizes = (ATT_Q, ATT_KV, ATT_KV, N_IDX_HEADS * IDX_DIM, IDX_DIM, N_IDX_HEADS,
             POOL_WIDTH, GMLP_WIDTH, GMLP_WIDTH, N_BRANCH * D_MODEL)
    offs = np.concatenate([[0], np.cumsum(sizes)]).tolist()
    wq, wk, wv, wiq, wik, wiw, wp, wu, wgv, wg = [w[:, offs[j]:offs[j + 1]] for j in range(len(sizes))]
    wq = wq * (HEAD_DIM ** -0.5 * LOG2E)
    wikw = jnp.pad(jnp.concatenate([wik, wiw], axis=1), ((0, 0), (0, LANES - IDX_DIM - N_IDX_HEADS)))
    w_cat = jnp.concatenate([wq, wk, wv, wiq, wikw, wp, wu, wgv], axis=1).astype(BF16)
    return dict(
        w_cat=w_cat, wvt=wv.T.astype(BF16), wiwt=wiw.T.astype(BF16), wg=wg.astype(BF16),
        wba=w_ba[l].astype(BF16),
        pbd=_block_diag(pool_w[l]).astype(BF16), psc=pool_scale[l].reshape(1, POOL_WIDTH),
        wbp=w_bp[l].astype(BF16), wbg=w_bg[l].astype(BF16), wo=w_out[l].astype(BF16),
        g_ln_g=g_ln_g[l].reshape(1, GMLP_WIDTH), g_ln_b=g_ln_b[l].reshape(1, GMLP_WIDTH),
        ws=g_ws[l], gbias=g_b[l],
        ln1_g=ln1_g[l].reshape(1, D_MODEL), ln1_b=ln1_b[l].reshape(1, D_MODEL),
        rwt=router_w[l].T.astype(BF16), rbias=router_bias[l],
        sh_gu=sh_gu[l].astype(BF16), sh_down=sh_down[l].astype(BF16),
        ln2_g=ln2_g[l].reshape(1, D_MODEL), ln2_b=ln2_b[l].reshape(1, D_MODEL),
    )


def _chunk_mix(wl, cl, tm):
    tril = jnp.tril(jnp.ones((cl, cl), F32))
    wm = wl["ws"][:, :cl, :cl] * tril
    reps = tm // cl
    wc = jnp.stack([_block_diag(jnp.broadcast_to(wm[g], (reps, cl, cl))) for g in range(GMLP_GROUPS)])
    gb = jnp.tile(jnp.repeat(wl["gbias"][:, :cl].T, GMLP_GC, axis=1), (reps, 1))
    return wc.astype(BF16), gb


def _heads_by_seq(a, ns, s_q):
    heads = a.shape[1] // HEAD_DIM
    return a.reshape(ns, s_q, heads, HEAD_DIM).transpose(0, 2, 1, 3).reshape(ns, heads * s_q, HEAD_DIM)


def kernel(x_prompt, x_sample, cache_k, cache_v, cache_idx_k, state_pool, page_table, w_in, w_branch_attn,
           w_branch_pool, w_branch_gmlp, w_out, pool_w, pool_scale, gmlp_ln_g, gmlp_ln_b, gmlp_ws, gmlp_b,
           ln1_g, ln1_b, router_w, router_bias, expert_w_gu, expert_w_down, shared_w_gu, shared_w_down,
           ln2_g, ln2_b):
    n_p, t_p, _ = x_prompt.shape
    n_s, t_s, _ = x_sample.shape
    depth = w_in.shape[0]
    n_pool = cache_k.shape[1]
    npages = page_table.shape[1]
    past = npages * PAGE_SIZE
    m_p, m_s = n_p * t_p, n_s * t_s
    alpha = (2 * depth) ** 0.25
    assert t_p % BLK == 0 and m_s % BLK == 0 and BLK % t_s == 0 and t_s <= 16

    tm_p = BLK
    tm_s = BLK
    tm_moe = 2048 if m_p % 2048 == 0 else tm_p
    tp_pool = 512 if t_p % 512 == 0 else BLK
    pages_per_step = 16 if npages % 16 == 0 else 1
    score_pages_per_step = 32 if npages % 32 == 0 else pages_per_step
    sel_batch = 8 if n_s % 8 == 0 else 1
    topk_p = min(TOPK_MAX, t_p // 4)
    topk_s = min(TOPK_MAX, (past + t_s) // 4)

    cos_p, sin_p = _rope_tables(jnp.arange(t_p, dtype=I32))
    cos_s, sin_s = _rope_tables(past + jnp.arange(t_s, dtype=I32))
    cos_s, sin_s = jnp.tile(cos_s, (m_s // t_s, 1)), jnp.tile(sin_s, (m_s // t_s, 1))
    tri_p = _tri_matrix(BLK, keys_on_lanes=False)
    group_of_row = jnp.asarray(np.eye(N_KV_HEADS, dtype=np.float32)[
        np.repeat(np.arange(N_HEADS) // (N_HEADS // N_KV_HEADS), t_s)])
    tri_s = _tri_matrix(LANES, keys_on_lanes=True)
    cache_kt = cache_k.transpose(0, 1, 3, 4, 2).reshape(depth * n_pool, ATT_KV, PAGE_SIZE)
    cache_vt = cache_v.transpose(0, 1, 3, 4, 2).reshape(depth * n_pool, ATT_KV, PAGE_SIZE)
    cache_ikt = cache_idx_k.transpose(0, 1, 3, 2).reshape(depth * n_pool, IDX_DIM, PAGE_SIZE)

    w_gu_all = expert_w_gu.reshape(depth * N_EXPERTS, D_MODEL, 2 * EXPERT_FF)
    w_down_all = expert_w_down.reshape(depth * N_EXPERTS, EXPERT_FF, D_MODEL)

    hp = x_prompt.reshape(m_p, D_MODEL)
    hs = x_sample.reshape(m_s, D_MODEL)
    outs = {name: [] for name in ("kp", "vp", "ikp", "pp", "ks", "vs", "iks", "ps", "gs")}
    for l in range(depth):
        wl = _layer_weights(l, w_in, w_branch_attn, w_branch_pool, w_branch_gmlp, w_out, pool_w, pool_scale,
                            gmlp_ln_g, gmlp_ln_b, gmlp_ws, gmlp_b, ln1_g, ln1_b, router_w, router_bias,
                            shared_w_gu, shared_w_down, ln2_g, ln2_b)

        def finish(x, a_bd, d, u, gvn, cl, tm, tm_e):
            wc, gb = _chunk_mix(wl, cl, tm)
            wm = dict(wl, wc=wc, gb=gb, rb=jnp.broadcast_to(wl["rbias"][:, None], (N_EXPERTS, tm)))
            x1, gate = _merge(x, a_bd, d, u, gvn, wm, tm, alpha)
            return _moe(x1, gate, w_gu_all, w_down_all, l, wl["sh_gu"], wl["sh_down"],
                        wl["ln2_g"], wl["ln2_b"], tm_e, alpha)

        q, k, v, kb, vt, iq, ikw, ikb, iwt, p, u, gvn = _proj(hp, wl, cos_p, sin_p, tm_p)
        p3 = p.reshape(n_p, t_p, POOL_WIDTH)
        d = _pool(p3, None, 0, tp_pool).reshape(m_p, POOL_WIDTH)
        a = _dsa_prompt(iq, iwt, q, ikb, kb, vt, tri_p, n_p, t_p, topk_p).reshape(m_p, ATT_Q)
        hp = finish(hp, a, d, u, gvn, CHUNK, tm_p, tm_moe)
        outs["kp"].append(k.reshape(n_p, t_p, N_KV_HEADS, HEAD_DIM))
        outs["vp"].append(v.reshape(n_p, t_p, N_KV_HEADS, HEAD_DIM))
        outs["ikp"].append(ikw[:, :IDX_DIM].reshape(n_p, t_p, IDX_DIM))
        outs["pp"].append(p3[:, t_p - POOL_STATE:])

        q, k, v, kb, vt, iq, ikw, ikb, iwt, p, u, gvn = _proj(hs, wl, cos_s, sin_s, tm_s)
        p3 = p.reshape(n_s, t_s, POOL_WIDTH)
        prefix16 = jnp.pad(state_pool[l], ((0, 0), (16 - POOL_STATE, 0), (0, 0)))
        d = _pool(p3, prefix16, past, t_s).reshape(m_s, POOL_WIDTH)
        iq_s = _heads_by_seq(iq, n_s, t_s)
        q_s = _heads_by_seq(q, n_s, t_s)
        q_s = jnp.einsum("nrc,rg->nrgc", q_s, group_of_row.astype(q_s.dtype)).reshape(n_s, -1, ATT_KV)
        wb_s = jnp.broadcast_to(iwt.reshape(N_IDX_HEADS, n_s, t_s).transpose(1, 0, 2).reshape(n_s, -1, 1),
                                (n_s, N_IDX_HEADS * t_s, LANES))
        pad_rows = lambda a: jnp.pad(a.reshape(n_s, t_s, -1), ((0, 0), (0, PAGE_SIZE - t_s), (0, 0)))
        ikn = pad_rows(ikb[:, :IDX_DIM])
        scores = _sample_scores(page_table, iq_s, wb_s, cache_ikt, l * n_pool, score_pages_per_step)
        bias = _sample_select(scores, iq_s, wb_s, ikn, tri_s, topk_s, sel_batch)
        a_s = _sample_attend(page_table, q_s, bias, pad_rows(k), pad_rows(v), cache_kt, cache_vt,
                             l * n_pool, pages_per_step)
        a_s = a_s.reshape(n_s, N_HEADS, t_s, N_KV_HEADS, HEAD_DIM)
        a_s = jnp.stack([a_s[:, h, :, h // (N_HEADS // N_KV_HEADS)] for h in range(N_HEADS)], axis=2)
        hs = finish(hs, a_s.reshape(m_s, ATT_Q), d, u, gvn, t_s, tm_s, tm_s)
        outs["ks"].append(k.reshape(n_s, t_s, N_KV_HEADS, HEAD_DIM))
        outs["vs"].append(v.reshape(n_s, t_s, N_KV_HEADS, HEAD_DIM))
        outs["iks"].append(ikw[:, :IDX_DIM].reshape(n_s, t_s, IDX_DIM))
        outs["ps"].append(jnp.concatenate([state_pool[l], p3], axis=1)[:, -POOL_STATE:])
        outs["gs"].append(gvn.reshape(n_s, t_s, GMLP_WIDTH))

    st = lambda name: jnp.stack(outs[name])
    return (hp.reshape(n_p, t_p, D_MODEL), hs.reshape(n_s, t_s, D_MODEL),
            st("kp"), st("vp"), st("ikp"), st("ks"), st("vs"), st("iks"), st("pp"), st("ps"), st("gs"))
```

```python
import functools

import jax
import jax.numpy as jnp
import numpy as np
from jax import lax
from jax.experimental import pallas as pl
from jax.experimental.pallas import tpu as pltpu

F32 = jnp.float32
BF16 = jnp.bfloat16
I32 = jnp.int32
I16 = jnp.int16

D_MODEL = 1024
N_HEADS = 8
N_KV_HEADS = 4
HEAD_DIM = 64
N_IDX_HEADS = 8
IDX_DIM = 64
TOPK_MAX = 256
PAGE_SIZE = 128
ROPE_THETA = 10000.0
POOL_WINDOWS = (2, 4, 8, 16)
POOL_WIDTH = 256
POOL_GC = 64
POOL_STATE = 15
GMLP_WIDTH = 256
GMLP_GROUPS = 4
GMLP_GC = 64
CHUNK = 128
N_BRANCH = 3
ATT_Q = N_HEADS * HEAD_DIM
ATT_KV = N_KV_HEADS * HEAD_DIM
N_EXPERTS = 64
TOPK_EXPERTS = 8
N_EXPERT_GROUPS = 8
TOPK_GROUPS = 4
EXPERT_FF = 256
ROUTED_SCALE = 2.5
LN_EPS = 1e-5

LANES = 128
SUBLANES = 8
PACK16 = 16
HALF16 = 2 ** 15
VT_ROWS = HEAD_DIM + PACK16
BLK = 256
ATTEND_HEADS_PER_STAGE = 8
MOE_EXPERTS_PER_STEP = 2
MOE_TOKEN_TILE = 2048
MOE_ROW_CHUNK = 1024
POOL_TOKEN_TILE = 512
ATTEND_PAGES_PER_STEP = 16
SCORE_PAGES_PER_STEP = 32
SELECT_SEQS_PER_STEP = 8
INT_MIN = -2 ** 31
MASKED = -1e30
LOG2E = 1.4426950408889634
VMEM_LIMIT = 56 * 1024 * 1024

C_Q = 0
C_K = C_Q + ATT_Q
C_V = C_K + ATT_KV
C_IQ = C_V + ATT_KV
C_IKW = C_IQ + N_IDX_HEADS * IDX_DIM
C_P = C_IKW + LANES
C_U = C_P + POOL_WIDTH
C_GV = C_U + GMLP_WIDTH
C_END = C_GV + GMLP_WIDTH


def _cparams(sem):
    return pltpu.CompilerParams(dimension_semantics=sem, vmem_limit_bytes=VMEM_LIMIT)


def _layer_norm(x, g, b):
    mu = jnp.mean(x, axis=-1, keepdims=True)
    xc = x - mu
    var = jnp.mean(xc * xc, axis=-1, keepdims=True)
    return xc * lax.rsqrt(var + LN_EPS) * g + b


def _sigmoid(x):
    return 1.0 / (1.0 + jnp.exp(-x))


def _dot(a, b):
    return jnp.dot(a, b, preferred_element_type=F32)


def _dot_nt(a, b):
    return lax.dot_general(a, b, (((1,), (1,)), ((), ())), preferred_element_type=F32)


def _sort_key(score):
    score = jnp.where(score == 0.0, 0.0, score)
    bits = lax.bitcast_convert_type(score, I32)
    return bits ^ ((bits >> 31) & jnp.int32(0x7FFFFFFF))


def _proj_kernel(x_ref, w_ref, wvt_ref, wiwt_ref, cos_ref, sin_ref, lng_ref, lnb_ref,
                 q_ref, k_ref, v_ref, kb_ref, vt_ref, iq_ref, ikw_ref, ikb_ref, iwt_ref,
                 p_ref, u_ref, gvn_ref):
    tm = x_ref.shape[0]
    xb = x_ref[...].astype(BF16)
    cos = cos_ref[...]
    sin = sin_ref[...]
    lane = lax.broadcasted_iota(I32, (tm, LANES), 1)
    first_half = (lane % HEAD_DIM) < (HEAD_DIM // 2)

    def mm(c0, width):
        return _dot(xb, w_ref[:, c0:c0 + width])

    def rope(z):
        partner = jnp.where(first_half, pltpu.roll(z, LANES - 32, 1), pltpu.roll(z, 32, 1))
        return z * cos + partner * sin

    def rope_wide(z):
        return jnp.concatenate([rope(z[:, s * LANES:(s + 1) * LANES]) for s in range(z.shape[1] // LANES)], axis=1)

    nblk = tm // BLK
    q_ref[...] = rope_wide(mm(C_Q, ATT_Q)).astype(BF16)
    k = rope_wide(mm(C_K, ATT_KV))
    k_ref[...] = k
    kb_ref[...] = k.astype(BF16)
    v_ref[...] = mm(C_V, ATT_KV)
    vt = _dot_nt(wvt_ref[...], xb)
    ones_rows = jnp.where(lax.broadcasted_iota(I32, (PACK16, tm), 0) == 0, 1.0, 0.0)
    vt = jnp.concatenate([piece for g in range(N_KV_HEADS)
                          for piece in (vt[g * HEAD_DIM:(g + 1) * HEAD_DIM], ones_rows)], axis=0).astype(BF16)
    for b in range(nblk):
        vt_ref[b] = vt[:, b * BLK:(b + 1) * BLK]
    iq_ref[...] = rope_wide(mm(C_IQ, N_IDX_HEADS * IDX_DIM)).astype(BF16)
    z = mm(C_IKW, LANES)
    is_key = lane < IDX_DIM
    ikw = jnp.where(is_key, rope(z), z)
    ikw_ref[...] = ikw
    ikb_ref[...] = jnp.where(is_key, ikw, 0.0).astype(BF16)
    iwt_ref[...] = _dot_nt(wiwt_ref[...], xb)
    p_ref[...] = mm(C_P, POOL_WIDTH)
    u_ref[...] = mm(C_U, GMLP_WIDTH)
    gvn_ref[...] = _layer_norm(mm(C_GV, GMLP_WIDTH), lng_ref[...], lnb_ref[...])


def _proj(x, wl, cos_t, sin_t, tm):
    m = x.shape[0]
    nt = cos_t.shape[0] // tm
    row = lambda w: pl.BlockSpec((tm, w), lambda i: (i, 0))
    full = lambda a: pl.BlockSpec(a.shape, lambda i: (0,) * a.ndim)
    tab = pl.BlockSpec((tm, LANES), lambda i: (i % nt, 0))
    nb = tm // BLK
    out_shape = (
        jax.ShapeDtypeStruct((m, ATT_Q), BF16),
        jax.ShapeDtypeStruct((m, ATT_KV), F32),
        jax.ShapeDtypeStruct((m, ATT_KV), F32),
        jax.ShapeDtypeStruct((m, ATT_KV), BF16),
        jax.ShapeDtypeStruct((m // BLK, N_KV_HEADS * VT_ROWS, BLK), BF16),
        jax.ShapeDtypeStruct((m, N_IDX_HEADS * IDX_DIM), BF16),
        jax.ShapeDtypeStruct((m, LANES), F32),
        jax.ShapeDtypeStruct((m, LANES), BF16),
        jax.ShapeDtypeStruct((N_IDX_HEADS, m), F32),
        jax.ShapeDtypeStruct((m, POOL_WIDTH), F32),
        jax.ShapeDtypeStruct((m, GMLP_WIDTH), F32),
        jax.ShapeDtypeStruct((m, GMLP_WIDTH), F32),
    )
    out_specs = (
        row(ATT_Q),
        row(ATT_KV), row(ATT_KV), row(ATT_KV),
        pl.BlockSpec((nb, N_KV_HEADS * VT_ROWS, BLK), lambda i: (i, 0, 0)),
        row(N_IDX_HEADS * IDX_DIM),
        row(LANES), row(LANES),
        pl.BlockSpec((N_IDX_HEADS, tm), lambda i: (0, i)),
        row(POOL_WIDTH), row(GMLP_WIDTH), row(GMLP_WIDTH),
    )
    weights = [wl["w_cat"], wl["wvt"], wl["wiwt"]]
    return pl.pallas_call(
        _proj_kernel,
        grid=(m // tm,),
        in_specs=[row(D_MODEL)] + [full(w) for w in weights] + [tab, tab, full(wl["g_ln_g"]), full(wl["g_ln_b"])],
        out_specs=out_specs,
        out_shape=out_shape,
        compiler_params=_cparams(("parallel",)),
        name="proj",
    )(x, *weights, cos_t, sin_t, wl["g_ln_g"], wl["g_ln_b"])


def _pool_kernel(p_ref, halo_ref, d_ref, ext_ref, s2_ref, s4_ref, s8_ref, *, pos_base, halo_is_prefix):
    tp = p_ref.shape[1]
    r_end = tp + 32
    i = pl.program_id(1)
    p = p_ref[0]
    halo = halo_ref[0]
    if not halo_is_prefix:
        halo = jnp.where(i == 0, 0.0, halo)
    ext_ref[0:16, :] = jnp.zeros((16, POOL_WIDTH), F32)
    ext_ref[16:32, :] = halo
    ext_ref[32:r_end, :] = p
    s2_ref[8:r_end, :] = ext_ref[8:r_end, :] + ext_ref[7:r_end - 1, :]
    s4_ref[16:r_end, :] = s2_ref[16:r_end, :] + s2_ref[14:r_end - 2, :]
    s8_ref[24:r_end, :] = s4_ref[24:r_end, :] + s4_ref[20:r_end - 4, :]
    s16 = s8_ref[32:r_end, :] + s8_ref[24:r_end - 8, :]
    lane = lax.broadcasted_iota(I32, (tp, POOL_WIDTH), 1)
    grp = lane // POOL_GC
    win = jnp.where(grp == 0, s2_ref[32:r_end, :],
                    jnp.where(grp == 1, s4_ref[32:r_end, :],
                              jnp.where(grp == 2, s8_ref[32:r_end, :], s16)))
    width = jnp.where(grp == 0, POOL_WINDOWS[0],
                      jnp.where(grp == 1, POOL_WINDOWS[1],
                                jnp.where(grp == 2, POOL_WINDOWS[2], POOL_WINDOWS[3])))
    pos = pos_base + i * tp + lax.broadcasted_iota(I32, (tp, POOL_WIDTH), 0)
    cnt = jnp.minimum(width, pos + 1).astype(F32)
    d_ref[0] = win / cnt - p


def _pool(p3, prefix16, pos_base, tp):
    n, t, _ = p3.shape
    halo_is_prefix = prefix16 is not None
    if halo_is_prefix:
        halo = prefix16
        halo_spec = pl.BlockSpec((1, 16, POOL_WIDTH), lambda b, i: (b, 0, 0))
    else:
        halo = p3
        step = tp // 16
        halo_spec = pl.BlockSpec((1, 16, POOL_WIDTH), lambda b, i: (b, jnp.maximum(i * step - 1, 0), 0))
    rows = tp + 32
    return pl.pallas_call(
        functools.partial(_pool_kernel, pos_base=pos_base, halo_is_prefix=halo_is_prefix),
        grid=(n, t // tp),
        in_specs=[pl.BlockSpec((1, tp, POOL_WIDTH), lambda b, i: (b, i, 0)), halo_spec],
        out_specs=pl.BlockSpec((1, tp, POOL_WIDTH), lambda b, i: (b, i, 0)),
        out_shape=jax.ShapeDtypeStruct((n, t, POOL_WIDTH), F32),
        scratch_shapes=[pltpu.VMEM((rows, POOL_WIDTH), F32) for _ in range(4)],
        compiler_params=_cparams(("parallel", "parallel")),
        name="pool",
    )(p3, halo)


def _kth_largest(sk_ref, hi_ref, lo_ref, nch, topk, keys_on_lanes, unroll=1):
    rows, cols = sk_ref.shape[1:]
    if keys_on_lanes:
        vec = (rows, 1)
        fold, acc_shape = (lambda x: x), (rows, cols)
        fold16, acc16_shape = fold, acc_shape
        total = lambda cnt: jnp.sum(cnt.astype(F32), axis=1, keepdims=True)
    else:
        vec = (1, cols)
        fold, acc_shape = (lambda x: jnp.sum(x.reshape(rows // SUBLANES, SUBLANES, cols), axis=0)), (SUBLANES, cols)
        fold16 = lambda x: functools.reduce(jnp.add, [x[g * PACK16:(g + 1) * PACK16] for g in range(rows // PACK16)])
        acc16_shape = (PACK16, cols)
        total = lambda cnt: jnp.sum(cnt.astype(F32), axis=0, keepdims=True)

    if not isinstance(nch, int):
        sk_ref[nch] = jnp.full((rows, cols), INT_MIN, I32)
        hi_ref[nch] = jnp.full((rows, cols), -HALF16, I16)
        lo_ref[nch] = jnp.full((rows, cols), -HALF16, I16)

    def over_chunks(body, init):
        if isinstance(nch, int):
            return lax.fori_loop(0, nch, body, init, unroll=unroll)
        return lax.fori_loop(0, (nch + 1) // 2, lambda j, carry: body(2 * j + 1, body(2 * j, carry)), init)

    def count(pred_fn):
        def body(c, cnt):
            return cnt + fold(jnp.where(pred_fn(sk_ref[c]), 1, 0))
        return total(over_chunks(body, jnp.zeros(acc_shape, I32)))

    def count16(ref, pred_fn):
        def body(c, cnt):
            return cnt + fold16(jnp.where(pred_fn(ref[c]), jnp.int16(1), jnp.int16(0)))
        return total(over_chunks(body, jnp.zeros(acc16_shape, I16)))

    def broadcast16(v):
        return jnp.broadcast_to(v.astype(I16), (rows, cols))

    def search16(ref, wanted):
        def bit_body(it, ubits):
            cand = ubits | lax.shift_left(jnp.int32(1), 15 - it)
            cand_b = broadcast16(cand - HALF16)
            tot = count16(ref, lambda s: s >= cand_b)
            return jnp.where(tot >= wanted, cand, ubits)
        return lax.fori_loop(0, 16, bit_body, jnp.zeros(vec, I32))

    hi = search16(hi_ref, topk) - HALF16
    hi_b = broadcast16(hi)
    wanted_lo = topk - count16(hi_ref, lambda s: s > hi_b)

    def low_halves(c, carry):
        low = ((sk_ref[c] & jnp.int32(0xFFFF)) - HALF16).astype(I16)
        lo_ref[c] = jnp.where(hi_ref[c] == hi_b, low, jnp.int16(-HALF16))
        return carry

    lax.fori_loop(0, nch, low_halves, 0)
    thr = hi * (2 * HALF16) + search16(lo_ref, wanted_lo)
    thr_b = jnp.broadcast_to(thr, (rows, cols))
    need_b = jnp.broadcast_to(topk - count(lambda s: s > thr_b), (rows, cols))
    surplus = jnp.where(thr == jnp.int32(INT_MIN), 0.0, count(lambda s: s >= thr_b) - topk)
    return jnp.maximum(thr_b, jnp.int32(INT_MIN + 1)), need_b, jnp.max(surplus) > 0.0


def _threshold_bias(skc, thr_b):
    return jnp.where(skc >= thr_b, 0.0, MASKED)


def _select_bias(skc, thr_b, need_b, eq_before, tri, keys_on_lanes):
    rows, cols = skc.shape
    eq = skc == thr_b
    eqf = jnp.where(eq, 1.0, 0.0).astype(BF16)
    if keys_on_lanes:
        res = _dot(eqf, tri)
        prefix, chunk_total = res[:, :cols], res[:, cols:]
    else:
        res = _dot(tri, eqf)
        prefix, chunk_total = res[:rows], res[rows:]
    keep = (skc > thr_b) | (eq & (prefix + eq_before <= need_b))
    return jnp.where(keep, 0.0, MASKED), eq_before + chunk_total


def _tri_matrix(n, keys_on_lanes):
    r = np.arange(n)
    ones = np.ones((n, n), np.float32)
    if keys_on_lanes:
        return jnp.asarray(np.concatenate([(r[:, None] <= r[None, :]).astype(np.float32), ones], axis=1), BF16)
    return jnp.asarray(np.concatenate([(r[:, None] >= r[None, :]).astype(np.float32), ones], axis=0), BF16)


def _pad_heads(q, iq, qpad_ref, iqpad_ref):
    rows = q.shape[0]
    low_half = lax.broadcasted_iota(I32, (rows, LANES), 1) < HEAD_DIM
    zero_slab = jnp.zeros((rows, LANES), F32)

    def head_slabs(z):
        z = z.astype(F32)
        for s in range(z.shape[1] // LANES):
            slab = z[:, s * LANES:(s + 1) * LANES]
            swapped = pltpu.roll(slab, HEAD_DIM, 1)
            for r in range(2):
                yield 2 * s + r, (slab, swapped) if r == 0 else (swapped, slab)

    for h, (head_low, head_high) in head_slabs(q):
        g = h // (N_HEADS // N_KV_HEADS)
        own = jnp.where(low_half, head_low, 0.0) if g % 2 == 0 else jnp.where(low_half, 0.0, head_high)
        qpad_ref[h] = jnp.concatenate([own, zero_slab] if g // 2 == 0 else [zero_slab, own], axis=1).astype(BF16)
    for h, (head_low, _) in head_slabs(iq):
        iqpad_ref[h * rows:(h + 1) * rows, :] = jnp.where(low_half, head_low, 0.0).astype(BF16)


def _dsa_prompt_kernel(iq_ref, iwt_ref, q_ref, ikb_ref, kb_ref, vt_ref, tri_ref, a_ref,
                       sk_ref, hi_ref, lo_ref, m_ref, acc_ref, qpad_ref, iqpad_ref, *, topk):
    i = pl.program_id(1)
    nch = i + 1
    _pad_heads(q_ref[0], iq_ref[0], qpad_ref, iqpad_ref)
    iwt = iwt_ref[...]
    key_id = lax.broadcasted_iota(I32, (BLK, BLK), 0)
    q_id = lax.broadcasted_iota(I32, (BLK, BLK), 1)

    def score_chunk(c, carry):
        ikc = ikb_ref[0, c]
        score = jnp.zeros((BLK, BLK), F32)
        for h in range(N_IDX_HEADS):
            s = _dot_nt(ikc, iqpad_ref[h * BLK:(h + 1) * BLK, :])
            score = score + jnp.maximum(s, 0.0) * iwt[h:h + 1, :]
        key = jnp.where(key_id > q_id + (i - c) * BLK, jnp.int32(INT_MIN), _sort_key(score))
        sk_ref[c] = key
        hi_ref[c] = (key >> 16).astype(I16)
        return carry

    lax.fori_loop(0, (nch + 1) // 2, lambda j, carry: score_chunk(2 * j + 1, score_chunk(2 * j, carry)), 0)
    thr_b, need_b, any_tie = _kth_largest(sk_ref, hi_ref, lo_ref, nch, topk, keys_on_lanes=False)

    m_ref[...] = jnp.full(m_ref.shape, MASKED, F32)
    acc_ref[...] = jnp.zeros(acc_ref.shape, F32)
    heads_per_group = N_HEADS // N_KV_HEADS

    def attend_chunk(c, bias):
        kc = kb_ref[0, c]
        cols = [slice(h * BLK, (h + 1) * BLK) for h in range(N_HEADS)]
        for h0 in range(0, N_HEADS, ATTEND_HEADS_PER_STAGE):
            heads = range(h0, h0 + ATTEND_HEADS_PER_STAGE)
            lgs = {h: _dot_nt(kc, qpad_ref[h]) + bias for h in heads}
            m_prev = {h: m_ref[:, cols[h]] for h in heads}
            m_new = {h: jnp.maximum(m_prev[h], jnp.max(lgs[h], axis=0, keepdims=True)) for h in heads}
            alpha = {h: jnp.exp2(m_prev[h] - m_new[h]) for h in heads}
            ps = {h: jnp.exp2(lgs[h] - m_new[h]) for h in heads}
            for h in heads:
                m_ref[:, cols[h]] = m_new[h]
            for h in heads:
                g = h // heads_per_group
                vg = vt_ref[0, c, g * VT_ROWS:(g + 1) * VT_ROWS, :]
                acc_ref[:, cols[h]] = alpha[h] * acc_ref[:, cols[h]] + _dot(vg, ps[h].astype(BF16))

    @pl.when(any_tie)
    def _():
        tri = tri_ref[...]

        def body(c, eq_before):
            bias, eq_after = _select_bias(sk_ref[c], thr_b, need_b, eq_before, tri, keys_on_lanes=False)
            attend_chunk(c, bias)
            return eq_after

        lax.fori_loop(0, nch, body, jnp.zeros((BLK, BLK), F32))

    @pl.when(jnp.logical_not(any_tie))
    def _():
        def body(c, carry):
            attend_chunk(c, _threshold_bias(sk_ref[c], thr_b))
            return carry

        lax.fori_loop(0, nch // 2, lambda j, carry: body(2 * j + 1, body(2 * j, carry)), 0)

        @pl.when(nch % 2 == 1)
        def _():
            body(nch - 1, 0)

    a_t = acc_ref[0:HEAD_DIM, :] / acc_ref[HEAD_DIM:HEAD_DIM + 1, :]
    a_t = jnp.concatenate([a_t[:, h * BLK:(h + 1) * BLK] for h in range(N_HEADS)], axis=0)
    a_ref[0] = a_t.T.astype(BF16)


def _dsa_prompt(iq, iwt, q, ikb, kb, vt, tri, n, t, topk):
    nb = t // BLK
    assert nb % 2 == 0, "the score loop reads key chunks in pairs"
    iq = iq.reshape(n * nb, BLK, N_IDX_HEADS * IDX_DIM)
    q = q.reshape(n * nb, BLK, ATT_Q)
    ikb4 = ikb.reshape(n, nb, BLK, LANES)
    kb4 = kb.reshape(n, nb, BLK, ATT_KV)
    vt4 = vt.reshape(n, nb, N_KV_HEADS * VT_ROWS, BLK)
    cols = N_HEADS * BLK
    return pl.pallas_call(
        functools.partial(_dsa_prompt_kernel, topk=topk),
        grid=(n, nb),
        in_specs=[
            pl.BlockSpec((1, BLK, N_IDX_HEADS * IDX_DIM), lambda b, i: (b * nb + i, 0, 0)),
            pl.BlockSpec((N_IDX_HEADS, BLK), lambda b, i: (0, b * nb + i)),
            pl.BlockSpec((1, BLK, ATT_Q), lambda b, i: (b * nb + i, 0, 0)),
            pl.BlockSpec((1, nb, BLK, LANES), lambda b, i: (b, 0, 0, 0)),
            pl.BlockSpec((1, nb, BLK, ATT_KV), lambda b, i: (b, 0, 0, 0)),
            pl.BlockSpec((1, nb, N_KV_HEADS * VT_ROWS, BLK), lambda b, i: (b, 0, 0, 0)),
            pl.BlockSpec(tri.shape, lambda b, i: (0, 0)),
        ],
        out_specs=pl.BlockSpec((1, BLK, ATT_Q), lambda b, i: (b * nb + i, 0, 0)),
        out_shape=jax.ShapeDtypeStruct((n * nb, BLK, ATT_Q), BF16),
        scratch_shapes=[
            pltpu.VMEM((nb + 1, BLK, BLK), I32),
            pltpu.VMEM((nb + 1, BLK, BLK), I16),
            pltpu.VMEM((nb + 1, BLK, BLK), I16),
            pltpu.VMEM((1, cols), F32),
            pltpu.VMEM((VT_ROWS, cols), F32),
            pltpu.VMEM((N_HEADS, BLK, ATT_KV), BF16),
            pltpu.VMEM((N_IDX_HEADS * BLK, LANES), BF16),
        ],
        compiler_params=_cparams(("parallel", "arbitrary")),
        name="dsa_prompt",
    )(iq, iwt, q, ikb4, kb4, vt4, tri)


def _sample_scores_kernel(pt_ref, iq_ref, wb_ref, *refs, pages_per_step):
    del pt_ref
    page_refs = refs[:pages_per_step]
    sc_ref = refs[pages_per_step]
    iq = iq_ref[0]
    wb = wb_ref[0]
    s_q = iq.shape[0] // N_IDX_HEADS
    for r in range(pages_per_step):
        s = _dot(iq, page_refs[r][0].astype(BF16))
        t = jnp.maximum(s, 0.0) * wb
        sc_ref[0, r] = jnp.sum(t.reshape(N_IDX_HEADS, s_q, LANES), axis=0)


def _sample_scores(page_table, iq_s, wb_s, cache_ikt, layer_off, pages_per_step):
    ns, npages = page_table.shape
    rows = iq_s.shape[1]
    s_q = rows // N_IDX_HEADS
    steps = npages // pages_per_step

    def page_spec(r):
        return pl.BlockSpec((1, IDX_DIM, PAGE_SIZE),
                            lambda b, j, pt: (layer_off + pt[b, j * pages_per_step + r], 0, 0))

    grid_spec = pltpu.PrefetchScalarGridSpec(
        num_scalar_prefetch=1,
        grid=(ns, steps),
        in_specs=[pl.BlockSpec((1, rows, IDX_DIM), lambda b, j, pt: (b, 0, 0)),
                  pl.BlockSpec((1, rows, LANES), lambda b, j, pt: (b, 0, 0))]
                 + [page_spec(r) for r in range(pages_per_step)],
        out_specs=pl.BlockSpec((1, pages_per_step, s_q, LANES), lambda b, j, pt: (b, j, 0, 0)),
    )
    return pl.pallas_call(
        functools.partial(_sample_scores_kernel, pages_per_step=pages_per_step),
        grid_spec=grid_spec,
        out_shape=jax.ShapeDtypeStruct((ns, npages, s_q, LANES), F32),
        compiler_params=_cparams(("parallel", "arbitrary")),
        name="sample_scores",
    )(page_table, iq_s, wb_s, *([cache_ikt] * pages_per_step))


def _sample_select_kernel(sc_ref, iq_ref, wb_ref, ikn_ref, tri_ref, bias_ref, sk_ref, hi_ref, lo_ref,
                          *, topk, s_q):
    sb = sc_ref.shape[0]
    npages = sc_ref.shape[1]
    rows = sb * s_q

    def key_chunk(c, carry):
        key = _sort_key(sc_ref[:, c].reshape(rows, LANES))
        sk_ref[c] = key
        hi_ref[c] = (key >> 16).astype(I16)
        return carry

    lax.fori_loop(0, npages, key_chunk, 0)
    row_id = lax.broadcasted_iota(I32, (s_q, LANES), 0)
    col_id = lax.broadcasted_iota(I32, (s_q, LANES), 1)
    new_keys = []
    for b in range(sb):
        s = _dot_nt(iq_ref[b], ikn_ref[b])
        t = jnp.maximum(s, 0.0) * wb_ref[b]
        score = jnp.sum(t.reshape(N_IDX_HEADS, s_q, LANES), axis=0)
        new_keys.append(jnp.where(col_id <= row_id, _sort_key(score), jnp.int32(INT_MIN)))
    key = jnp.concatenate(new_keys, axis=0)
    sk_ref[npages] = key
    hi_ref[npages] = (key >> 16).astype(I16)
    nch = npages + 1
    thr_b, need_b, any_tie = _kth_largest(sk_ref, hi_ref, lo_ref, nch, topk, keys_on_lanes=True, unroll=4)

    @pl.when(any_tie)
    def _():
        tri = tri_ref[...]

        def body(c, eq_before):
            bias, eq_after = _select_bias(sk_ref[c], thr_b, need_b, eq_before, tri, keys_on_lanes=True)
            bias_ref[:, c] = bias.reshape(sb, s_q, LANES)
            return eq_after

        lax.fori_loop(0, nch, body, jnp.zeros((rows, LANES), F32))

    @pl.when(jnp.logical_not(any_tie))
    def _():
        def body(c, carry):
            bias_ref[:, c] = _threshold_bias(sk_ref[c], thr_b).reshape(sb, s_q, LANES)
            return carry

        lax.fori_loop(0, nch, body, 0, unroll=4)


def _sample_select(scores, iq_s, wb_s, ikn, tri, topk, sb):
    ns, npages, s_q, _ = scores.shape
    rows = iq_s.shape[1]
    return pl.pallas_call(
        functools.partial(_sample_select_kernel, topk=topk, s_q=s_q),
        grid=(ns // sb,),
        in_specs=[
            pl.BlockSpec((sb, npages, s_q, LANES), lambda g: (g, 0, 0, 0)),
            pl.BlockSpec((sb, rows, IDX_DIM), lambda g: (g, 0, 0)),
            pl.BlockSpec((sb, rows, LANES), lambda g: (g, 0, 0)),
            pl.BlockSpec((sb, PAGE_SIZE, IDX_DIM), lambda g: (g, 0, 0)),
            pl.BlockSpec(tri.shape, lambda g: (0, 0)),
        ],
        out_specs=pl.BlockSpec((sb, npages + 1, s_q, LANES), lambda g: (g, 0, 0, 0)),
        out_shape=jax.ShapeDtypeStruct((ns, npages + 1, s_q, LANES), F32),
        scratch_shapes=[pltpu.VMEM((npages + 1, sb * s_q, LANES), dt) for dt in (I32, I16, I16)],
        compiler_params=_cparams(("parallel",)),
        name="sample_select",
    )(scores, iq_s, wb_s, ikn, tri)


def _sample_attend_kernel(pt_ref, q_ref, bias_ref, bias_new_ref, kn_ref, vn_ref, *refs, pages_per_step, s_q):
    del pt_ref
    k_refs = refs[:pages_per_step]
    v_refs = refs[pages_per_step:2 * pages_per_step]
    a_ref, m_ref, l_ref, acc_ref = refs[2 * pages_per_step:]
    j = pl.program_id(1)
    rows = N_HEADS * s_q

    @pl.when(j == 0)
    def _():
        m_ref[...] = jnp.full(m_ref.shape, MASKED, F32)
        l_ref[...] = jnp.zeros(l_ref.shape, F32)
        acc_ref[...] = jnp.zeros(acc_ref.shape, F32)

    q = q_ref[0]

    def masked(lg, bias):
        return (lg.reshape(N_HEADS, s_q, LANES) + bias[None]).reshape(rows, LANES)

    def update(lgs, weighted_values):
        top = functools.reduce(jnp.maximum, lgs)
        m_prev = m_ref[...]
        m_new = jnp.maximum(m_prev, jnp.max(top, axis=1, keepdims=True))
        alpha = jnp.exp2(m_prev - m_new)
        ps = [jnp.exp2(lg - m_new) for lg in lgs]
        l_ref[...] = alpha * l_ref[...] + jnp.sum(functools.reduce(jnp.add, ps), axis=1, keepdims=True)
        m_ref[...] = m_new
        acc_ref[...] = alpha * acc_ref[...] + weighted_values([p.astype(BF16) for p in ps])

    lgs = [masked(_dot(q, k_refs[r][0].astype(BF16)), bias_ref[0, r]) for r in range(pages_per_step)]
    update(lgs, lambda ps: functools.reduce(
        jnp.add, [_dot_nt(p, v_refs[r][0].astype(BF16)) for r, p in enumerate(ps)]))

    @pl.when(j == pl.num_programs(1) - 1)
    def _():
        lg = masked(_dot_nt(q, kn_ref[0].astype(BF16)), bias_new_ref[0, 0])
        update([lg], lambda ps: _dot(ps[0], vn_ref[0].astype(BF16)))
        a_ref[0] = (acc_ref[...] / l_ref[...]).astype(BF16)


def _sample_attend(page_table, q_s, bias, kn, vn, cache_kt, cache_vt, layer_off, pages_per_step):
    ns, npages = page_table.shape
    rows = q_s.shape[1]
    s_q = rows // N_HEADS
    steps = npages // pages_per_step

    def page_spec(r):
        return pl.BlockSpec((1, ATT_KV, PAGE_SIZE),
                            lambda b, j, pt: (layer_off + pt[b, j * pages_per_step + r], 0, 0))

    new_spec = pl.BlockSpec((1, PAGE_SIZE, ATT_KV), lambda b, j, pt: (b, 0, 0))
    grid_spec = pltpu.PrefetchScalarGridSpec(
        num_scalar_prefetch=1,
        grid=(ns, steps),
        in_specs=[pl.BlockSpec((1, rows, ATT_KV), lambda b, j, pt: (b, 0, 0)),
                  pl.BlockSpec((1, pages_per_step, s_q, LANES), lambda b, j, pt: (b, j, 0, 0)),
                  pl.BlockSpec((1, 1, s_q, LANES), lambda b, j, pt: (b, npages, 0, 0)),
                  new_spec, new_spec]
                 + [page_spec(r) for r in range(pages_per_step)] * 2,
        out_specs=pl.BlockSpec((1, rows, ATT_KV), lambda b, j, pt: (b, 0, 0)),
        scratch_shapes=[
            pltpu.VMEM((rows, 1), F32),
            pltpu.VMEM((rows, 1), F32),
            pltpu.VMEM((rows, ATT_KV), F32),
        ],
    )
    return pl.pallas_call(
        functools.partial(_sample_attend_kernel, pages_per_step=pages_per_step, s_q=s_q),
        grid_spec=grid_spec,
        out_shape=jax.ShapeDtypeStruct((ns, rows, ATT_KV), BF16),
        compiler_params=_cparams(("parallel", "arbitrary")),
        name="sample_attend",
    )(page_table, q_s, bias, bias, kn, vn,
      *([cache_kt] * pages_per_step), *([cache_vt] * pages_per_step))


def _merge_kernel(x_ref, a_ref, d_ref, u_ref, gvn_ref, wc_ref, gb_ref, wg_ref, wba_ref, pbd_ref, psc_ref,
                  wbp_ref, wbg_ref, wo_ref, lng_ref, lnb_ref, rwt_ref, rb_ref,
                  x1_ref, gate_ref, *, alpha):
    tm = x_ref.shape[0]
    x = x_ref[...]
    xb = x.astype(BF16)

    def gate(idx):
        return _sigmoid(_dot(xb, wg_ref[:, idx * D_MODEL:(idx + 1) * D_MODEL]))

    m = gate(0) * _dot(a_ref[...], wba_ref[...])
    y = _dot(d_ref[...].astype(BF16), pbd_ref[...]) * psc_ref[...]
    m = m + gate(1) * _dot(y.astype(BF16), wbp_ref[...])
    gv = gvn_ref[...].astype(BF16)
    grp = lax.broadcasted_iota(I32, (tm, GMLP_WIDTH), 1) // GMLP_GC
    mix = gb_ref[...]
    for g in range(GMLP_GROUPS):
        mix = mix + jnp.where(grp == g, _dot(wc_ref[g], gv), 0.0)
    c = u_ref[...] * mix
    m = m + gate(2) * _dot(c.astype(BF16), wbg_ref[...])
    y = _dot(m.astype(BF16), wo_ref[...])
    x1 = _layer_norm(alpha * x + y, lng_ref[...], lnb_ref[...])
    x1_ref[...] = x1

    scores = _sigmoid(_dot_nt(rwt_ref[...], x1.astype(BF16)))
    sel = scores + rb_ref[...]
    per = N_EXPERTS // N_EXPERT_GROUPS
    g3 = sel.reshape(N_EXPERT_GROUPS, per, tm)
    sub = lax.broadcasted_iota(I32, (N_EXPERT_GROUPS, per, tm), 1)
    m1 = jnp.max(g3, axis=1, keepdims=True)
    first = jnp.min(jnp.where(g3 == m1, sub, per), axis=1, keepdims=True)
    m2 = jnp.max(jnp.where(sub == first, -jnp.inf, g3), axis=1, keepdims=True)
    gs = (m1 + m2).reshape(N_EXPERT_GROUPS, tm)

    def rank_of(vals, count):
        idx = lax.broadcasted_iota(I32, vals.shape, 0)
        rank = jnp.zeros(vals.shape, I32)
        for o in range(count):
            other = vals[o:o + 1]
            beats = (other > vals) | ((other == vals) & (o < idx))
            rank = rank + jnp.where(beats, 1, 0)
        return rank

    gkeep = rank_of(gs, N_EXPERT_GROUPS) < TOPK_GROUPS
    ekeep = jnp.broadcast_to(gkeep.reshape(N_EXPERT_GROUPS, 1, tm), (N_EXPERT_GROUPS, per, tm)).reshape(N_EXPERTS, tm)
    sel = jnp.where(ekeep, sel, -jnp.inf)
    chosen = rank_of(sel, N_EXPERTS) < TOPK_EXPERTS
    wsel = jnp.where(chosen, scores, 0.0)
    gate_t = wsel / jnp.sum(wsel, axis=0, keepdims=True) * ROUTED_SCALE
    gate_ref[...] = gate_t.T


def _merge(x, a_bd, d, u, gvn, wl, tm, alpha):
    m = x.shape[0]
    row = lambda w: pl.BlockSpec((tm, w), lambda i: (i, 0))
    full = lambda a: pl.BlockSpec(a.shape, lambda i: (0,) * a.ndim)
    weights = [wl["wc"], wl["gb"], wl["wg"], wl["wba"], wl["pbd"], wl["psc"], wl["wbp"], wl["wbg"], wl["wo"],
               wl["ln1_g"], wl["ln1_b"], wl["rwt"], wl["rb"]]
    return pl.pallas_call(
        functools.partial(_merge_kernel, alpha=alpha),
        grid=(m // tm,),
        in_specs=[row(D_MODEL), row(ATT_Q), row(POOL_WIDTH), row(GMLP_WIDTH), row(GMLP_WIDTH)]
                 + [full(w) for w in weights],
        out_specs=(row(D_MODEL), row(N_EXPERTS)),
        out_shape=(jax.ShapeDtypeStruct((m, D_MODEL), F32), jax.ShapeDtypeStruct((m, N_EXPERTS), F32)),
        compiler_params=_cparams(("parallel",)),
        name="merge",
    )(x, a_bd, d, u, gvn, *weights)


def _swiglu_act(xb, w_gu):
    h = _dot(xb, w_gu)
    g = h[:, :EXPERT_FF]
    return g * _sigmoid(g) * h[:, EXPERT_FF:]


def _moe_kernel(x_ref, gate_ref, wgu_ref, wd_ref, sgu_ref, sd_ref, lng_ref, lnb_ref, o_ref, xb_ref, *, alpha):
    step = pl.program_id(1)
    tm = x_ref.shape[0]
    row_chunks = [slice(r, r + MOE_ROW_CHUNK) for r in range(0, tm, MOE_ROW_CHUNK)] if tm > MOE_ROW_CHUNK \
        else [slice(0, tm)]

    @pl.when(step == 0)
    def _():
        for rows in row_chunks:
            xb_ref[rows, :] = x_ref[rows, :].astype(BF16)
            o_ref[rows, :] = _dot(_swiglu_act(xb_ref[rows, :], sgu_ref[...]).astype(BF16), sd_ref[...])

    w_gu = [wgu_ref[j].astype(BF16) for j in range(MOE_EXPERTS_PER_STEP)]
    w_down = wd_ref[...].astype(BF16).reshape(MOE_EXPERTS_PER_STEP * EXPERT_FF, D_MODEL)
    for rows in row_chunks:
        xb = xb_ref[rows, :]
        gate = gate_ref[rows, :]
        lane = lax.broadcasted_iota(I32, gate.shape, 1)
        acts = []
        for j in range(MOE_EXPERTS_PER_STEP):
            e = step * MOE_EXPERTS_PER_STEP + j
            gcol = jnp.sum(jnp.where(lane == e, gate, 0.0), axis=1, keepdims=True)
            acts.append((gcol * _swiglu_act(xb, w_gu[j])).astype(BF16))
        o_ref[rows, :] += _dot(jnp.concatenate(acts, axis=1), w_down)

    @pl.when(step == pl.num_programs(1) - 1)
    def _():
        for rows in row_chunks:
            o_ref[rows, :] = _layer_norm(alpha * x_ref[rows, :] + o_ref[rows, :], lng_ref[...], lnb_ref[...])


def _moe(x1, gate, w_gu, w_down, layer, sh_gu, sh_down, ln_g, ln_b, tm, alpha):
    m = x1.shape[0]
    full = lambda a: pl.BlockSpec(a.shape, lambda i, e: (0,) * a.ndim)
    per = MOE_EXPERTS_PER_STEP
    steps = N_EXPERTS // per
    once = pl.Buffered(1)
    return pl.pallas_call(
        functools.partial(_moe_kernel, alpha=alpha),
        grid=(m // tm, steps),
        in_specs=[
            pl.BlockSpec((tm, D_MODEL), lambda i, e: (i, 0), pipeline_mode=once),
            pl.BlockSpec((tm, N_EXPERTS), lambda i, e: (i, 0), pipeline_mode=once),
            pl.BlockSpec((per, D_MODEL, 2 * EXPERT_FF), lambda i, e: (layer * steps + e, 0, 0)),
            pl.BlockSpec((per, EXPERT_FF, D_MODEL), lambda i, e: (layer * steps + e, 0, 0)),
            full(sh_gu), full(sh_down), full(ln_g), full(ln_b),
        ],
        out_specs=pl.BlockSpec((tm, D_MODEL), lambda i, e: (i, 0), pipeline_mode=once),
        out_shape=jax.ShapeDtypeStruct((m, D_MODEL), F32),
        scratch_shapes=[pltpu.VMEM((tm, D_MODEL), BF16)],
        compiler_params=_cparams(("parallel", "arbitrary")),
        name="moe",
    )(x1, gate, w_gu, w_down, sh_gu, sh_down, ln_g, ln_b)


def _rope_tables(pos):
    half = HEAD_DIM // 2
    inv = ROPE_THETA ** (-jnp.arange(half, dtype=F32) / half)
    ang = pos.astype(F32)[:, None] * inv[None, :]
    cos, sin = jnp.cos(ang), jnp.sin(ang)
    cos_t = jnp.tile(cos, (1, LANES // half))
    sin_t = jnp.tile(jnp.concatenate([-sin, sin], axis=1), (1, LANES // HEAD_DIM))
    return cos_t, sin_t


def _block_diag(blocks):
    g, r, c = blocks.shape
    eye = jnp.eye(g, dtype=blocks.dtype)
    return jnp.einsum("grc,gh->grhc", blocks, eye).reshape(g * r, g * c)


def _layer_weights(l, w_in, w_ba, w_bp, w_bg, w_out, pool_w, pool_scale, g_ln_g, g_ln_b, g_ws, g_b,
                   ln1_g, ln1_b, router_w, router_bias, sh_gu, sh_down, ln2_g, ln2_b):
    w = w_in[l]
    sizes = (ATT_Q, ATT_KV, ATT_KV, N_IDX_HEADS * IDX_DIM, IDX_DIM, N_IDX_HEADS,
             POOL_WIDTH, GMLP_WIDTH, GMLP_WIDTH, N_BRANCH * D_MODEL)
    offs = np.concatenate([[0], np.cumsum(sizes)]).tolist()
    wq, wk, wv, wiq, wik, wiw, wp, wu, wgv, wg = [w[:, offs[j]:offs[j + 1]] for j in range(len(sizes))]
    wq = wq * (HEAD_DIM ** -0.5 * LOG2E)
    wikw = jnp.pad(jnp.concatenate([wik, wiw], axis=1), ((0, 0), (0, LANES - IDX_DIM - N_IDX_HEADS)))
    w_cat = jnp.concatenate([wq, wk, wv, wiq, wikw, wp, wu, wgv], axis=1).astype(BF16)
    return dict(
        w_cat=w_cat, wvt=wv.T.astype(BF16), wiwt=wiw.T.astype(BF16), wg=wg.astype(BF16),
        wba=w_ba[l].astype(BF16),
        pbd=_block_diag(pool_w[l]).astype(BF16), psc=pool_scale[l].reshape(1, POOL_WIDTH),
        wbp=w_bp[l].astype(BF16), wbg=w_bg[l].astype(BF16), wo=w_out[l].astype(BF16),
        g_ln_g=g_ln_g[l].reshape(1, GMLP_WIDTH), g_ln_b=g_ln_b[l].reshape(1, GMLP_WIDTH),
        ws=g_ws[l], gbias=g_b[l],
        ln1_g=ln1_g[l].reshape(1, D_MODEL), ln1_b=ln1_b[l].reshape(1, D_MODEL),
        rwt=router_w[l].T.astype(BF16), rbias=router_bias[l],
        sh_gu=sh_gu[l].astype(BF16), sh_down=sh_down[l].astype(BF16),
        ln2_g=ln2_g[l].reshape(1, D_MODEL), ln2_b=ln2_b[l].reshape(1, D_MODEL),
    )


def _chunk_mix(wl, cl, tm):
    tril = jnp.tril(jnp.ones((cl, cl), F32))
    wm = wl["ws"][:, :cl, :cl] * tril
    reps = tm // cl
    wc = jnp.stack([_block_diag(jnp.broadcast_to(wm[g], (reps, cl, cl))) for g in range(GMLP_GROUPS)])
    gb = jnp.tile(jnp.repeat(wl["gbias"][:, :cl].T, GMLP_GC, axis=1), (reps, 1))
    return wc.astype(BF16), gb


def _heads_by_seq(a, ns, s_q):
    heads = a.shape[1] // HEAD_DIM
    return a.reshape(ns, s_q, heads, HEAD_DIM).transpose(0, 2, 1, 3).reshape(ns, heads * s_q, HEAD_DIM)


def kernel(x_prompt, x_sample, cache_k, cache_v, cache_idx_k, state_pool, page_table, w_in, w_branch_attn,
           w_branch_pool, w_branch_gmlp, w_out, pool_w, pool_scale, gmlp_ln_g, gmlp_ln_b, gmlp_ws, gmlp_b,
           ln1_g, ln1_b, router_w, router_bias, expert_w_gu, expert_w_down, shared_w_gu, shared_w_down,
           ln2_g, ln2_b):
    n_p, t_p, _ = x_prompt.shape
    n_s, t_s, _ = x_sample.shape
    depth = w_in.shape[0]
    n_pool = cache_k.shape[1]
    npages = page_table.shape[1]
    past = npages * PAGE_SIZE
    m_p, m_s = n_p * t_p, n_s * t_s
    alpha = (2 * depth) ** 0.25
    assert t_p % BLK == 0 and m_s % BLK == 0 and BLK % t_s == 0 and t_s <= 16

    tm_p = BLK
    tm_s = BLK
    tm_moe = MOE_TOKEN_TILE if m_p % MOE_TOKEN_TILE == 0 else tm_p
    tp_pool = POOL_TOKEN_TILE if t_p % POOL_TOKEN_TILE == 0 else BLK
    pages_per_step = ATTEND_PAGES_PER_STEP if npages % ATTEND_PAGES_PER_STEP == 0 else 1
    score_pages_per_step = SCORE_PAGES_PER_STEP if npages % SCORE_PAGES_PER_STEP == 0 else pages_per_step
    sel_batch = SELECT_SEQS_PER_STEP if n_s % SELECT_SEQS_PER_STEP == 0 else 1
    topk_p = min(TOPK_MAX, t_p // 4)
    topk_s = min(TOPK_MAX, (past + t_s) // 4)

    cos_p, sin_p = _rope_tables(jnp.arange(t_p, dtype=I32))
    cos_s, sin_s = _rope_tables(past + jnp.arange(t_s, dtype=I32))
    cos_s, sin_s = jnp.tile(cos_s, (m_s // t_s, 1)), jnp.tile(sin_s, (m_s // t_s, 1))
    tri_p = _tri_matrix(BLK, keys_on_lanes=False)
    group_of_row = jnp.asarray(np.eye(N_KV_HEADS, dtype=np.float32)[
        np.repeat(np.arange(N_HEADS) // (N_HEADS // N_KV_HEADS), t_s)])
    tri_s = _tri_matrix(LANES, keys_on_lanes=True)
    cache_kt = cache_k.transpose(0, 1, 3, 4, 2).reshape(depth * n_pool, ATT_KV, PAGE_SIZE)
    cache_vt = cache_v.transpose(0, 1, 3, 4, 2).reshape(depth * n_pool, ATT_KV, PAGE_SIZE)
    cache_ikt = cache_idx_k.transpose(0, 1, 3, 2).reshape(depth * n_pool, IDX_DIM, PAGE_SIZE)

    w_gu_all = expert_w_gu.reshape(depth * N_EXPERTS, D_MODEL, 2 * EXPERT_FF)
    w_down_all = expert_w_down.reshape(depth * N_EXPERTS, EXPERT_FF, D_MODEL)

    hp = x_prompt.reshape(m_p, D_MODEL)
    hs = x_sample.reshape(m_s, D_MODEL)
    outs = {name: [] for name in ("kp", "vp", "ikp", "pp", "ks", "vs", "iks", "ps", "gs")}
    for l in range(depth):
        wl = _layer_weights(l, w_in, w_branch_attn, w_branch_pool, w_branch_gmlp, w_out, pool_w, pool_scale,
                            gmlp_ln_g, gmlp_ln_b, gmlp_ws, gmlp_b, ln1_g, ln1_b, router_w, router_bias,
                            shared_w_gu, shared_w_down, ln2_g, ln2_b)

        def finish(x, a_bd, d, u, gvn, cl, tm, tm_e):
            wc, gb = _chunk_mix(wl, cl, tm)
            wm = dict(wl, wc=wc, gb=gb, rb=jnp.broadcast_to(wl["rbias"][:, None], (N_EXPERTS, tm)))
            x1, gate = _merge(x, a_bd, d, u, gvn, wm, tm, alpha)
            return _moe(x1, gate, w_gu_all, w_down_all, l, wl["sh_gu"], wl["sh_down"],
                        wl["ln2_g"], wl["ln2_b"], tm_e, alpha)

        q, k, v, kb, vt, iq, ikw, ikb, iwt, p, u, gvn = _proj(hp, wl, cos_p, sin_p, tm_p)
        p3 = p.reshape(n_p, t_p, POOL_WIDTH)
        d = _pool(p3, None, 0, tp_pool).reshape(m_p, POOL_WIDTH)
        a = _dsa_prompt(iq, iwt, q, ikb, kb, vt, tri_p, n_p, t_p, topk_p).reshape(m_p, ATT_Q)
        hp = finish(hp, a, d, u, gvn, CHUNK, tm_p, tm_moe)
        outs["kp"].append(k.reshape(n_p, t_p, N_KV_HEADS, HEAD_DIM))
        outs["vp"].append(v.reshape(n_p, t_p, N_KV_HEADS, HEAD_DIM))
        outs["ikp"].append(ikw[:, :IDX_DIM].reshape(n_p, t_p, IDX_DIM))
        outs["pp"].append(p3[:, t_p - POOL_STATE:])

        q, k, v, kb, vt, iq, ikw, ikb, iwt, p, u, gvn = _proj(hs, wl, cos_s, sin_s, tm_s)
        p3 = p.reshape(n_s, t_s, POOL_WIDTH)
        prefix16 = jnp.pad(state_pool[l], ((0, 0), (16 - POOL_STATE, 0), (0, 0)))
        d = _pool(p3, prefix16, past, t_s).reshape(m_s, POOL_WIDTH)
        iq_s = _heads_by_seq(iq, n_s, t_s)
        q_s = _heads_by_seq(q, n_s, t_s)
        q_s = jnp.einsum("nrc,rg->nrgc", q_s, group_of_row.astype(q_s.dtype)).reshape(n_s, -1, ATT_KV)
        wb_s = jnp.broadcast_to(iwt.reshape(N_IDX_HEADS, n_s, t_s).transpose(1, 0, 2).reshape(n_s, -1, 1),
                                (n_s, N_IDX_HEADS * t_s, LANES))
        pad_rows = lambda a: jnp.pad(a.reshape(n_s, t_s, -1), ((0, 0), (0, PAGE_SIZE - t_s), (0, 0)))
        ikn = pad_rows(ikb[:, :IDX_DIM])
        scores = _sample_scores(page_table, iq_s, wb_s, cache_ikt, l * n_pool, score_pages_per_step)
        bias = _sample_select(scores, iq_s, wb_s, ikn, tri_s, topk_s, sel_batch)
        a_s = _sample_attend(page_table, q_s, bias, pad_rows(k), pad_rows(v), cache_kt, cache_vt,
                             l * n_pool, pages_per_step)
        a_s = a_s.reshape(n_s, N_HEADS, t_s, N_KV_HEADS, HEAD_DIM)
        a_s = jnp.stack([a_s[:, h, :, h // (N_HEADS // N_KV_HEADS)] for h in range(N_HEADS)], axis=2)
        hs = finish(hs, a_s.reshape(m_s, ATT_Q), d, u, gvn, t_s, tm_s, tm_s)
        outs["ks"].append(k.reshape(n_s, t_s, N_KV_HEADS, HEAD_DIM))
        outs["vs"].append(v.reshape(n_s, t_s, N_KV_HEADS, HEAD_DIM))
        outs["iks"].append(ikw[:, :IDX_DIM].reshape(n_s, t_s, IDX_DIM))
        outs["ps"].append(jnp.concatenate([state_pool[l], p3], axis=1)[:, -POOL_STATE:])
        outs["gs"].append(gvn.reshape(n_s, t_s, GMLP_WIDTH))

    st = lambda name: jnp.stack(outs[name])
    return (hp.reshape(n_p, t_p, D_MODEL), hs.reshape(n_s, t_s, D_MODEL),
            st("kp"), st("vp"), st("ikp"), st("ks"), st("vs"), st("iks"), st("pp"), st("ps"), st("gs"))
```

```python
import functools

import jax
import jax.numpy as jnp
import numpy as np
from jax import lax
from jax.experimental import pallas as pl
from jax.experimental.pallas import tpu as pltpu

F32 = jnp.float32
BF16 = jnp.bfloat16
I32 = jnp.int32
I16 = jnp.int16

D_MODEL = 1024
N_HEADS = 8
N_KV_HEADS = 4
HEAD_DIM = 64
N_IDX_HEADS = 8
IDX_DIM = 64
TOPK_MAX = 256
PAGE_SIZE = 128
ROPE_THETA = 10000.0
POOL_WINDOWS = (2, 4, 8, 16)
POOL_WIDTH = 256
POOL_GC = 64
POOL_STATE = 15
GMLP_WIDTH = 256
GMLP_GROUPS = 4
GMLP_GC = 64
CHUNK = 128
N_BRANCH = 3
ATT_Q = N_HEADS * HEAD_DIM
ATT_KV = N_KV_HEADS * HEAD_DIM
N_EXPERTS = 64
TOPK_EXPERTS = 8
N_EXPERT_GROUPS = 8
TOPK_GROUPS = 4
EXPERT_FF = 256
ROUTED_SCALE = 2.5
LN_EPS = 1e-5

LANES = 128
SUBLANES = 8
PACK16 = 16
HALF16 = 2 ** 15
VT_ROWS = HEAD_DIM + PACK16
BLK = 256
ATTEND_HEADS_PER_STAGE = 8
MOE_EXPERTS_PER_STEP = 2
MOE_TOKEN_TILE = 2048
MOE_ROW_CHUNK = 1024
POOL_TOKEN_TILE = 512
ATTEND_PAGES_PER_STEP = 16
SCORE_PAGES_PER_STEP = 32
SELECT_SEQS_PER_STEP = 8
INT_MIN = -2 ** 31
MASKED = -1e30
LOG2E = 1.4426950408889634
VMEM_LIMIT = 56 * 1024 * 1024

C_Q = 0
C_K = C_Q + ATT_Q
C_V = C_K + ATT_KV
C_IQ = C_V + ATT_KV
C_IKW = C_IQ + N_IDX_HEADS * IDX_DIM
C_P = C_IKW + LANES
C_U = C_P + POOL_WIDTH
C_GV = C_U + GMLP_WIDTH
C_END = C_GV + GMLP_WIDTH


def _cparams(sem):
    return pltpu.CompilerParams(dimension_semantics=sem, vmem_limit_bytes=VMEM_LIMIT)


def _layer_norm(x, g, b):
    mu = jnp.mean(x, axis=-1, keepdims=True)
    xc = x - mu
    var = jnp.mean(xc * xc, axis=-1, keepdims=True)
    return xc * lax.rsqrt(var + LN_EPS) * g + b


def _sigmoid(x):
    return 1.0 / (1.0 + jnp.exp(-x))


def _dot(a, b):
    return jnp.dot(a, b, preferred_element_type=F32)


def _dot_nt(a, b):
    return lax.dot_general(a, b, (((1,), (1,)), ((), ())), preferred_element_type=F32)


def _sort_key(score):
    score = jnp.where(score == 0.0, 0.0, score)
    bits = lax.bitcast_convert_type(score, I32)
    return bits ^ ((bits >> 31) & jnp.int32(0x7FFFFFFF))


def _proj_kernel(x_ref, w_ref, wvt_ref, wiwt_ref, cos_ref, sin_ref, lng_ref, lnb_ref,
                 q_ref, k_ref, v_ref, kb_ref, vt_ref, iq_ref, ikw_ref, ikb_ref, iwt_ref,
                 p_ref, u_ref, gvn_ref):
    tm = x_ref.shape[0]
    xb = x_ref[...].astype(BF16)
    cos = cos_ref[...]
    sin = sin_ref[...]
    lane = lax.broadcasted_iota(I32, (tm, LANES), 1)
    first_half = (lane % HEAD_DIM) < (HEAD_DIM // 2)

    def mm(c0, width):
        return _dot(xb, w_ref[:, c0:c0 + width])

    def rope(z):
        partner = jnp.where(first_half, pltpu.roll(z, LANES - 32, 1), pltpu.roll(z, 32, 1))
        return z * cos + partner * sin

    def rope_wide(z):
        return jnp.concatenate([rope(z[:, s * LANES:(s + 1) * LANES]) for s in range(z.shape[1] // LANES)], axis=1)

    nblk = tm // BLK
    q_ref[...] = rope_wide(mm(C_Q, ATT_Q)).astype(BF16)
    k = rope_wide(mm(C_K, ATT_KV))
    k_ref[...] = k
    kb_ref[...] = k.astype(BF16)
    v_ref[...] = mm(C_V, ATT_KV)
    vt = _dot_nt(wvt_ref[...], xb)
    ones_rows = jnp.where(lax.broadcasted_iota(I32, (PACK16, tm), 0) == 0, 1.0, 0.0)
    vt = jnp.concatenate([piece for g in range(N_KV_HEADS)
                          for piece in (vt[g * HEAD_DIM:(g + 1) * HEAD_DIM], ones_rows)], axis=0).astype(BF16)
    for b in range(nblk):
        vt_ref[b] = vt[:, b * BLK:(b + 1) * BLK]
    iq_ref[...] = rope_wide(mm(C_IQ, N_IDX_HEADS * IDX_DIM)).astype(BF16)
    z = mm(C_IKW, LANES)
    is_key = lane < IDX_DIM
    ikw = jnp.where(is_key, rope(z), z)
    ikw_ref[...] = ikw
    ikb_ref[...] = jnp.where(is_key, ikw, 0.0).astype(BF16)
    iwt_ref[...] = _dot_nt(wiwt_ref[...], xb)
    p_ref[...] = mm(C_P, POOL_WIDTH)
    u_ref[...] = mm(C_U, GMLP_WIDTH)
    gvn_ref[...] = _layer_norm(mm(C_GV, GMLP_WIDTH), lng_ref[...], lnb_ref[...])


def _proj(x, wl, cos_t, sin_t, tm):
    m = x.shape[0]
    nt = cos_t.shape[0] // tm
    row = lambda w: pl.BlockSpec((tm, w), lambda i: (i, 0))
    full = lambda a: pl.BlockSpec(a.shape, lambda i: (0,) * a.ndim)
    tab = pl.BlockSpec((tm, LANES), lambda i: (i % nt, 0))
    nb = tm // BLK
    out_shape = (
        jax.ShapeDtypeStruct((m, ATT_Q), BF16),
        jax.ShapeDtypeStruct((m, ATT_KV), F32),
        jax.ShapeDtypeStruct((m, ATT_KV), F32),
        jax.ShapeDtypeStruct((m, ATT_KV), BF16),
        jax.ShapeDtypeStruct((m // BLK, N_KV_HEADS * VT_ROWS, BLK), BF16),
        jax.ShapeDtypeStruct((m, N_IDX_HEADS * IDX_DIM), BF16),
        jax.ShapeDtypeStruct((m, LANES), F32),
        jax.ShapeDtypeStruct((m, LANES), BF16),
        jax.ShapeDtypeStruct((N_IDX_HEADS, m), F32),
        jax.ShapeDtypeStruct((m, POOL_WIDTH), F32),
        jax.ShapeDtypeStruct((m, GMLP_WIDTH), F32),
        jax.ShapeDtypeStruct((m, GMLP_WIDTH), F32),
    )
    out_specs = (
        row(ATT_Q),
        row(ATT_KV), row(ATT_KV), row(ATT_KV),
        pl.BlockSpec((nb, N_KV_HEADS * VT_ROWS, BLK), lambda i: (i, 0, 0)),
        row(N_IDX_HEADS * IDX_DIM),
        row(LANES), row(LANES),
        pl.BlockSpec((N_IDX_HEADS, tm), lambda i: (0, i)),
        row(POOL_WIDTH), row(GMLP_WIDTH), row(GMLP_WIDTH),
    )
    weights = [wl["w_cat"], wl["wvt"], wl["wiwt"]]
    return pl.pallas_call(
        _proj_kernel,
        grid=(m // tm,),
        in_specs=[row(D_MODEL)] + [full(w) for w in weights] + [tab, tab, full(wl["g_ln_g"]), full(wl["g_ln_b"])],
        out_specs=out_specs,
        out_shape=out_shape,
        compiler_params=_cparams(("parallel",)),
        name="proj",
    )(x, *weights, cos_t, sin_t, wl["g_ln_g"], wl["g_ln_b"])


def _pool_kernel(p_ref, halo_ref, d_ref, ext_ref, s2_ref, s4_ref, s8_ref, *, pos_base, halo_is_prefix):
    tp = p_ref.shape[1]
    r_end = tp + 32
    i = pl.program_id(1)
    p = p_ref[0]
    halo = halo_ref[0]
    if not halo_is_prefix:
        halo = jnp.where(i == 0, 0.0, halo)
    ext_ref[0:16, :] = jnp.zeros((16, POOL_WIDTH), F32)
    ext_ref[16:32, :] = halo
    ext_ref[32:r_end, :] = p
    s2_ref[8:r_end, :] = ext_ref[8:r_end, :] + ext_ref[7:r_end - 1, :]
    s4_ref[16:r_end, :] = s2_ref[16:r_end, :] + s2_ref[14:r_end - 2, :]
    s8_ref[24:r_end, :] = s4_ref[24:r_end, :] + s4_ref[20:r_end - 4, :]
    s16 = s8_ref[32:r_end, :] + s8_ref[24:r_end - 8, :]
    lane = lax.broadcasted_iota(I32, (tp, POOL_WIDTH), 1)
    grp = lane // POOL_GC
    win = jnp.where(grp == 0, s2_ref[32:r_end, :],
                    jnp.where(grp == 1, s4_ref[32:r_end, :],
                              jnp.where(grp == 2, s8_ref[32:r_end, :], s16)))
    width = jnp.where(grp == 0, POOL_WINDOWS[0],
                      jnp.where(grp == 1, POOL_WINDOWS[1],
                                jnp.where(grp == 2, POOL_WINDOWS[2], POOL_WINDOWS[3])))
    pos = pos_base + i * tp + lax.broadcasted_iota(I32, (tp, POOL_WIDTH), 0)
    cnt = jnp.minimum(width, pos + 1).astype(F32)
    d_ref[0] = win / cnt - p


def _pool(p3, prefix16, pos_base, tp):
    n, t, _ = p3.shape
    halo_is_prefix = prefix16 is not None
    if halo_is_prefix:
        halo = prefix16
        halo_spec = pl.BlockSpec((1, 16, POOL_WIDTH), lambda b, i: (b, 0, 0))
    else:
        halo = p3
        step = tp // 16
        halo_spec = pl.BlockSpec((1, 16, POOL_WIDTH), lambda b, i: (b, jnp.maximum(i * step - 1, 0), 0))
    rows = tp + 32
    return pl.pallas_call(
        functools.partial(_pool_kernel, pos_base=pos_base, halo_is_prefix=halo_is_prefix),
        grid=(n, t // tp),
        in_specs=[pl.BlockSpec((1, tp, POOL_WIDTH), lambda b, i: (b, i, 0)), halo_spec],
        out_specs=pl.BlockSpec((1, tp, POOL_WIDTH), lambda b, i: (b, i, 0)),
        out_shape=jax.ShapeDtypeStruct((n, t, POOL_WIDTH), F32),
        scratch_shapes=[pltpu.VMEM((rows, POOL_WIDTH), F32) for _ in range(4)],
        compiler_params=_cparams(("parallel", "parallel")),
        name="pool",
    )(p3, halo)


def _kth_largest(sk_ref, hi_ref, lo_ref, nch, topk, keys_on_lanes, unroll=1):
    rows, cols = sk_ref.shape[1:]
    if keys_on_lanes:
        vec = (rows, 1)
        fold, acc_shape = (lambda x: x), (rows, cols)
        fold16, acc16_shape = fold, acc_shape
        total = lambda cnt: jnp.sum(cnt.astype(F32), axis=1, keepdims=True)
    else:
        vec = (1, cols)
        fold, acc_shape = (lambda x: jnp.sum(x.reshape(rows // SUBLANES, SUBLANES, cols), axis=0)), (SUBLANES, cols)
        fold16 = lambda x: functools.reduce(jnp.add, [x[g * PACK16:(g + 1) * PACK16] for g in range(rows // PACK16)])
        acc16_shape = (PACK16, cols)
        total = lambda cnt: jnp.sum(cnt.astype(F32), axis=0, keepdims=True)

    if not isinstance(nch, int):
        sk_ref[nch] = jnp.full((rows, cols), INT_MIN, I32)
        hi_ref[nch] = jnp.full((rows, cols), -HALF16, I16)
        lo_ref[nch] = jnp.full((rows, cols), -HALF16, I16)

    def over_chunks(body, init):
        if isinstance(nch, int):
            return lax.fori_loop(0, nch, body, init, unroll=unroll)
        return lax.fori_loop(0, (nch + 1) // 2, lambda j, carry: body(2 * j + 1, body(2 * j, carry)), init)

    def count(pred_fn):
        def body(c, cnt):
            return cnt + fold(jnp.where(pred_fn(sk_ref[c]), 1, 0))
        return total(over_chunks(body, jnp.zeros(acc_shape, I32)))

    def count16(ref, pred_fn):
        def body(c, cnt):
            return cnt + fold16(jnp.where(pred_fn(ref[c]), jnp.int16(1), jnp.int16(0)))
        return total(over_chunks(body, jnp.zeros(acc16_shape, I16)))

    def broadcast16(v):
        return jnp.broadcast_to(v.astype(I16), (rows, cols))

    def search16(ref, wanted):
        def bit_body(it, ubits):
            cand = ubits | lax.shift_left(jnp.int32(1), 15 - it)
            cand_b = broadcast16(cand - HALF16)
            tot = count16(ref, lambda s: s >= cand_b)
            return jnp.where(tot >= wanted, cand, ubits)
        return lax.fori_loop(0, 16, bit_body, jnp.zeros(vec, I32))

    hi = search16(hi_ref, topk) - HALF16
    hi_b = broadcast16(hi)
    wanted_lo = topk - count16(hi_ref, lambda s: s > hi_b)

    def low_halves(c, carry):
        low = ((sk_ref[c] & jnp.int32(0xFFFF)) - HALF16).astype(I16)
        lo_ref[c] = jnp.where(hi_ref[c] == hi_b, low, jnp.int16(-HALF16))
        return carry

    lax.fori_loop(0, nch, low_halves, 0)
    thr = hi * (2 * HALF16) + search16(lo_ref, wanted_lo)
    thr_b = jnp.broadcast_to(thr, (rows, cols))
    need_b = jnp.broadcast_to(topk - count(lambda s: s > thr_b), (rows, cols))
    surplus = jnp.where(thr == jnp.int32(INT_MIN), 0.0, count(lambda s: s >= thr_b) - topk)
    return jnp.maximum(thr_b, jnp.int32(INT_MIN + 1)), need_b, jnp.max(surplus) > 0.0


def _threshold_bias(skc, thr_b):
    return jnp.where(skc >= thr_b, 0.0, MASKED)


def _select_bias(skc, thr_b, need_b, eq_before, tri, keys_on_lanes):
    rows, cols = skc.shape
    eq = skc == thr_b
    eqf = jnp.where(eq, 1.0, 0.0).astype(BF16)
    if keys_on_lanes:
        res = _dot(eqf, tri)
        prefix, chunk_total = res[:, :cols], res[:, cols:]
    else:
        res = _dot(tri, eqf)
        prefix, chunk_total = res[:rows], res[rows:]
    keep = (skc > thr_b) | (eq & (prefix + eq_before <= need_b))
    return jnp.where(keep, 0.0, MASKED), eq_before + chunk_total


def _tri_matrix(n, keys_on_lanes):
    r = np.arange(n)
    ones = np.ones((n, n), np.float32)
    if keys_on_lanes:
        return jnp.asarray(np.concatenate([(r[:, None] <= r[None, :]).astype(np.float32), ones], axis=1), BF16)
    return jnp.asarray(np.concatenate([(r[:, None] >= r[None, :]).astype(np.float32), ones], axis=0), BF16)


def _pad_heads(q, iq, qpad_ref, iqpad_ref):
    rows = q.shape[0]
    low_half = lax.broadcasted_iota(I32, (rows, LANES), 1) < HEAD_DIM
    zero_slab = jnp.zeros((rows, LANES), F32)

    def head_slabs(z):
        z = z.astype(F32)
        for s in range(z.shape[1] // LANES):
            slab = z[:, s * LANES:(s + 1) * LANES]
            swapped = pltpu.roll(slab, HEAD_DIM, 1)
            for r in range(2):
                yield 2 * s + r, (slab, swapped) if r == 0 else (swapped, slab)

    for h, (head_low, head_high) in head_slabs(q):
        g = h // (N_HEADS // N_KV_HEADS)
        own = jnp.where(low_half, head_low, 0.0) if g % 2 == 0 else jnp.where(low_half, 0.0, head_high)
        qpad_ref[h] = jnp.concatenate([own, zero_slab] if g // 2 == 0 else [zero_slab, own], axis=1).astype(BF16)
    for h, (head_low, _) in head_slabs(iq):
        iqpad_ref[h * rows:(h + 1) * rows, :] = jnp.where(low_half, head_low, 0.0).astype(BF16)


def _dsa_prompt_kernel(iq_ref, iwt_ref, q_ref, ikb_ref, kb_ref, vt_ref, tri_ref, a_ref,
                       sk_ref, hi_ref, lo_ref, m_ref, acc_ref, qpad_ref, iqpad_ref, *, topk):
    i = pl.program_id(1)
    nch = i + 1
    _pad_heads(q_ref[0], iq_ref[0], qpad_ref, iqpad_ref)
    iwt = iwt_ref[...]
    key_id = lax.broadcasted_iota(I32, (BLK, BLK), 0)
    q_id = lax.broadcasted_iota(I32, (BLK, BLK), 1)

    def score_chunk(c, carry):
        ikc = ikb_ref[0, c]
        score = jnp.zeros((BLK, BLK), F32)
        for h in range(N_IDX_HEADS):
            s = _dot_nt(ikc, iqpad_ref[h * BLK:(h + 1) * BLK, :])
            score = score + jnp.maximum(s, 0.0) * iwt[h:h + 1, :]
        key = jnp.where(key_id > q_id + (i - c) * BLK, jnp.int32(INT_MIN), _sort_key(score))
        sk_ref[c] = key
        hi_ref[c] = (key >> 16).astype(I16)
        return carry

    lax.fori_loop(0, nch // 2, lambda j, carry: score_chunk(2 * j + 1, score_chunk(2 * j, carry)), 0)

    @pl.when(nch % 2 == 1)
    def _():
        score_chunk(nch - 1, 0)
    thr_b, need_b, any_tie = _kth_largest(sk_ref, hi_ref, lo_ref, nch, topk, keys_on_lanes=False)

    m_ref[...] = jnp.full(m_ref.shape, MASKED, F32)
    acc_ref[...] = jnp.zeros(acc_ref.shape, F32)
    heads_per_group = N_HEADS // N_KV_HEADS

    def attend_chunk(c, bias):
        kc = kb_ref[0, c]
        cols = [slice(h * BLK, (h + 1) * BLK) for h in range(N_HEADS)]
        for h0 in range(0, N_HEADS, ATTEND_HEADS_PER_STAGE):
            heads = range(h0, h0 + ATTEND_HEADS_PER_STAGE)
            lgs = {h: _dot_nt(kc, qpad_ref[h]) + bias for h in heads}
            m_prev = {h: m_ref[:, cols[h]] for h in heads}
            m_new = {h: jnp.maximum(m_prev[h], jnp.max(lgs[h], axis=0, keepdims=True)) for h in heads}
            alpha = {h: jnp.exp2(m_prev[h] - m_new[h]) for h in heads}
            ps = {h: jnp.exp2(lgs[h] - m_new[h]) for h in heads}
            for h in heads:
                m_ref[:, cols[h]] = m_new[h]
            for h in heads:
                g = h // heads_per_group
                vg = vt_ref[0, c, g * VT_ROWS:(g + 1) * VT_ROWS, :]
                acc_ref[:, cols[h]] = alpha[h] * acc_ref[:, cols[h]] + _dot(vg, ps[h].astype(BF16))

    @pl.when(any_tie)
    def _():
        tri = tri_ref[...]

        def body(c, eq_before):
            bias, eq_after = _select_bias(sk_ref[c], thr_b, need_b, eq_before, tri, keys_on_lanes=False)
            attend_chunk(c, bias)
            return eq_after

        ties = lax.fori_loop(0, nch // 2, lambda j, carry: body(2 * j + 1, body(2 * j, carry)),
                             jnp.zeros((BLK, BLK), F32))

        @pl.when(nch % 2 == 1)
        def _():
            body(nch - 1, ties)

    @pl.when(jnp.logical_not(any_tie))
    def _():
        def body(c, carry):
            attend_chunk(c, _threshold_bias(sk_ref[c], thr_b))
            return carry

        lax.fori_loop(0, nch // 2, lambda j, carry: body(2 * j + 1, body(2 * j, carry)), 0)

        @pl.when(nch % 2 == 1)
        def _():
            body(nch - 1, 0)

    a_t = acc_ref[0:HEAD_DIM, :] / acc_ref[HEAD_DIM:HEAD_DIM + 1, :]
    a_t = jnp.concatenate([a_t[:, h * BLK:(h + 1) * BLK] for h in range(N_HEADS)], axis=0)
    a_ref[0] = a_t.T.astype(BF16)


def _dsa_prompt(iq, iwt, q, ikb, kb, vt, tri, n, t, topk):
    nb = t // BLK
    assert nb % 2 == 0, "the score loop reads key chunks in pairs"
    iq = iq.reshape(n * nb, BLK, N_IDX_HEADS * IDX_DIM)
    q = q.reshape(n * nb, BLK, ATT_Q)
    ikb4 = ikb.reshape(n, nb, BLK, LANES)
    kb4 = kb.reshape(n, nb, BLK, ATT_KV)
    vt4 = vt.reshape(n, nb, N_KV_HEADS * VT_ROWS, BLK)
    cols = N_HEADS * BLK
    return pl.pallas_call(
        functools.partial(_dsa_prompt_kernel, topk=topk),
        grid=(n, nb),
        in_specs=[
            pl.BlockSpec((1, BLK, N_IDX_HEADS * IDX_DIM), lambda b, i: (b * nb + i, 0, 0)),
            pl.BlockSpec((N_IDX_HEADS, BLK), lambda b, i: (0, b * nb + i)),
            pl.BlockSpec((1, BLK, ATT_Q), lambda b, i: (b * nb + i, 0, 0)),
            pl.BlockSpec((1, nb, BLK, LANES), lambda b, i: (b, 0, 0, 0)),
            pl.BlockSpec((1, nb, BLK, ATT_KV), lambda b, i: (b, 0, 0, 0)),
            pl.BlockSpec((1, nb, N_KV_HEADS * VT_ROWS, BLK), lambda b, i: (b, 0, 0, 0)),
            pl.BlockSpec(tri.shape, lambda b, i: (0, 0)),
        ],
        out_specs=pl.BlockSpec((1, BLK, ATT_Q), lambda b, i: (b * nb + i, 0, 0)),
        out_shape=jax.ShapeDtypeStruct((n * nb, BLK, ATT_Q), BF16),
        scratch_shapes=[
            pltpu.VMEM((nb + 1, BLK, BLK), I32),
            pltpu.VMEM((nb + 1, BLK, BLK), I16),
            pltpu.VMEM((nb + 1, BLK, BLK), I16),
            pltpu.VMEM((1, cols), F32),
            pltpu.VMEM((VT_ROWS, cols), F32),
            pltpu.VMEM((N_HEADS, BLK, ATT_KV), BF16),
            pltpu.VMEM((N_IDX_HEADS * BLK, LANES), BF16),
        ],
        compiler_params=_cparams(("parallel", "arbitrary")),
        name="dsa_prompt",
    )(iq, iwt, q, ikb4, kb4, vt4, tri)


def _sample_scores_kernel(pt_ref, iq_ref, wb_ref, *refs, pages_per_step):
    del pt_ref
    page_refs = refs[:pages_per_step]
    sc_ref = refs[pages_per_step]
    iq = iq_ref[0]
    wb = wb_ref[0]
    s_q = iq.shape[0] // N_IDX_HEADS
    for r in range(pages_per_step):
        s = _dot(iq, page_refs[r][0].astype(BF16))
        t = jnp.maximum(s, 0.0) * wb
        sc_ref[0, r] = jnp.sum(t.reshape(N_IDX_HEADS, s_q, LANES), axis=0)


def _sample_scores(page_table, iq_s, wb_s, cache_ikt, layer_off, pages_per_step):
    ns, npages = page_table.shape
    rows = iq_s.shape[1]
    s_q = rows // N_IDX_HEADS
    steps = npages // pages_per_step

    def page_spec(r):
        return pl.BlockSpec((1, IDX_DIM, PAGE_SIZE),
                            lambda b, j, pt: (layer_off + pt[b, j * pages_per_step + r], 0, 0))

    grid_spec = pltpu.PrefetchScalarGridSpec(
        num_scalar_prefetch=1,
        grid=(ns, steps),
        in_specs=[pl.BlockSpec((1, rows, IDX_DIM), lambda b, j, pt: (b, 0, 0)),
                  pl.BlockSpec((1, rows, LANES), lambda b, j, pt: (b, 0, 0))]
                 + [page_spec(r) for r in range(pages_per_step)],
        out_specs=pl.BlockSpec((1, pages_per_step, s_q, LANES), lambda b, j, pt: (b, j, 0, 0)),
    )
    return pl.pallas_call(
        functools.partial(_sample_scores_kernel, pages_per_step=pages_per_step),
        grid_spec=grid_spec,
        out_shape=jax.ShapeDtypeStruct((ns, npages, s_q, LANES), F32),
        compiler_params=_cparams(("parallel", "arbitrary")),
        name="sample_scores",
    )(page_table, iq_s, wb_s, *([cache_ikt] * pages_per_step))


def _sample_select_kernel(sc_ref, iq_ref, wb_ref, ikn_ref, tri_ref, bias_ref, sk_ref, hi_ref, lo_ref,
                          *, topk, s_q):
    sb = sc_ref.shape[0]
    npages = sc_ref.shape[1]
    rows = sb * s_q

    def key_chunk(c, carry):
        key = _sort_key(sc_ref[:, c].reshape(rows, LANES))
        sk_ref[c] = key
        hi_ref[c] = (key >> 16).astype(I16)
        return carry

    lax.fori_loop(0, npages, key_chunk, 0)
    row_id = lax.broadcasted_iota(I32, (s_q, LANES), 0)
    col_id = lax.broadcasted_iota(I32, (s_q, LANES), 1)
    new_keys = []
    for b in range(sb):
        s = _dot_nt(iq_ref[b], ikn_ref[b])
        t = jnp.maximum(s, 0.0) * wb_ref[b]
        score = jnp.sum(t.reshape(N_IDX_HEADS, s_q, LANES), axis=0)
        new_keys.append(jnp.where(col_id <= row_id, _sort_key(score), jnp.int32(INT_MIN)))
    key = jnp.concatenate(new_keys, axis=0)
    sk_ref[npages] = key
    hi_ref[npages] = (key >> 16).astype(I16)
    nch = npages + 1
    thr_b, need_b, any_tie = _kth_largest(sk_ref, hi_ref, lo_ref, nch, topk, keys_on_lanes=True, unroll=4)

    @pl.when(any_tie)
    def _():
        tri = tri_ref[...]

        def body(c, eq_before):
            bias, eq_after = _select_bias(sk_ref[c], thr_b, need_b, eq_before, tri, keys_on_lanes=True)
            bias_ref[:, c] = bias.reshape(sb, s_q, LANES)
            return eq_after

        lax.fori_loop(0, nch, body, jnp.zeros((rows, LANES), F32))

    @pl.when(jnp.logical_not(any_tie))
    def _():
        def body(c, carry):
            bias_ref[:, c] = _threshold_bias(sk_ref[c], thr_b).reshape(sb, s_q, LANES)
            return carry

        lax.fori_loop(0, nch, body, 0, unroll=4)


def _sample_select(scores, iq_s, wb_s, ikn, tri, topk, sb):
    ns, npages, s_q, _ = scores.shape
    rows = iq_s.shape[1]
    return pl.pallas_call(
        functools.partial(_sample_select_kernel, topk=topk, s_q=s_q),
        grid=(ns // sb,),
        in_specs=[
            pl.BlockSpec((sb, npages, s_q, LANES), lambda g: (g, 0, 0, 0)),
            pl.BlockSpec((sb, rows, IDX_DIM), lambda g: (g, 0, 0)),
            pl.BlockSpec((sb, rows, LANES), lambda g: (g, 0, 0)),
            pl.BlockSpec((sb, PAGE_SIZE, IDX_DIM), lambda g: (g, 0, 0)),
            pl.BlockSpec(tri.shape, lambda g: (0, 0)),
        ],
        out_specs=pl.BlockSpec((sb, npages + 1, s_q, LANES), lambda g: (g, 0, 0, 0)),
        out_shape=jax.ShapeDtypeStruct((ns, npages + 1, s_q, LANES), F32),
        scratch_shapes=[pltpu.VMEM((npages + 1, sb * s_q, LANES), dt) for dt in (I32, I16, I16)],
        compiler_params=_cparams(("parallel",)),
        name="sample_select",
    )(scores, iq_s, wb_s, ikn, tri)


def _sample_attend_kernel(pt_ref, q_ref, bias_ref, bias_new_ref, kn_ref, vn_ref, *refs, pages_per_step, s_q):
    del pt_ref
    k_refs = refs[:pages_per_step]
    v_refs = refs[pages_per_step:2 * pages_per_step]
    a_ref, m_ref, l_ref, acc_ref = refs[2 * pages_per_step:]
    j = pl.program_id(1)
    rows = N_HEADS * s_q

    @pl.when(j == 0)
    def _():
        m_ref[...] = jnp.full(m_ref.shape, MASKED, F32)
        l_ref[...] = jnp.zeros(l_ref.shape, F32)
        acc_ref[...] = jnp.zeros(acc_ref.shape, F32)

    q = q_ref[0]

    def masked(lg, bias):
        return (lg.reshape(N_HEADS, s_q, LANES) + bias[None]).reshape(rows, LANES)

    def update(lgs, weighted_values):
        top = functools.reduce(jnp.maximum, lgs)
        m_prev = m_ref[...]
        m_new = jnp.maximum(m_prev, jnp.max(top, axis=1, keepdims=True))
        alpha = jnp.exp2(m_prev - m_new)
        ps = [jnp.exp2(lg - m_new) for lg in lgs]
        l_ref[...] = alpha * l_ref[...] + jnp.sum(functools.reduce(jnp.add, ps), axis=1, keepdims=True)
        m_ref[...] = m_new
        acc_ref[...] = alpha * acc_ref[...] + weighted_values([p.astype(BF16) for p in ps])

    lgs = [masked(_dot(q, k_refs[r][0].astype(BF16)), bias_ref[0, r]) for r in range(pages_per_step)]
    update(lgs, lambda ps: functools.reduce(
        jnp.add, [_dot_nt(p, v_refs[r][0].astype(BF16)) for r, p in enumerate(ps)]))

    @pl.when(j == pl.num_programs(1) - 1)
    def _():
        lg = masked(_dot_nt(q, kn_ref[0].astype(BF16)), bias_new_ref[0, 0])
        update([lg], lambda ps: _dot(ps[0], vn_ref[0].astype(BF16)))
        a_ref[0] = (acc_ref[...] / l_ref[...]).astype(BF16)


def _sample_attend(page_table, q_s, bias, kn, vn, cache_kt, cache_vt, layer_off, pages_per_step):
    ns, npages = page_table.shape
    rows = q_s.shape[1]
    s_q = rows // N_HEADS
    steps = npages // pages_per_step

    def page_spec(r):
        return pl.BlockSpec((1, ATT_KV, PAGE_SIZE),
                            lambda b, j, pt: (layer_off + pt[b, j * pages_per_step + r], 0, 0))

    new_spec = pl.BlockSpec((1, PAGE_SIZE, ATT_KV), lambda b, j, pt: (b, 0, 0))
    grid_spec = pltpu.PrefetchScalarGridSpec(
        num_scalar_prefetch=1,
        grid=(ns, steps),
        in_specs=[pl.BlockSpec((1, rows, ATT_KV), lambda b, j, pt: (b, 0, 0)),
                  pl.BlockSpec((1, pages_per_step, s_q, LANES), lambda b, j, pt: (b, j, 0, 0)),
                  pl.BlockSpec((1, 1, s_q, LANES), lambda b, j, pt: (b, npages, 0, 0)),
                  new_spec, new_spec]
                 + [page_spec(r) for r in range(pages_per_step)] * 2,
        out_specs=pl.BlockSpec((1, rows, ATT_KV), lambda b, j, pt: (b, 0, 0)),
        scratch_shapes=[
            pltpu.VMEM((rows, 1), F32),
            pltpu.VMEM((rows, 1), F32),
            pltpu.VMEM((rows, ATT_KV), F32),
        ],
    )
    return pl.pallas_call(
        functools.partial(_sample_attend_kernel, pages_per_step=pages_per_step, s_q=s_q),
        grid_spec=grid_spec,
        out_shape=jax.ShapeDtypeStruct((ns, rows, ATT_KV), BF16),
        compiler_params=_cparams(("parallel", "arbitrary")),
        name="sample_attend",
    )(page_table, q_s, bias, bias, kn, vn,
      *([cache_kt] * pages_per_step), *([cache_vt] * pages_per_step))


def _merge_kernel(x_ref, a_ref, d_ref, u_ref, gvn_ref, wc_ref, gb_ref, wg_ref, wba_ref, pbd_ref, psc_ref,
                  wbp_ref, wbg_ref, wo_ref, lng_ref, lnb_ref, rwt_ref, rb_ref,
                  x1_ref, gate_ref, *, alpha):
    tm = x_ref.shape[0]
    x = x_ref[...]
    xb = x.astype(BF16)

    def gate(idx):
        return _sigmoid(_dot(xb, wg_ref[:, idx * D_MODEL:(idx + 1) * D_MODEL]))

    m = gate(0) * _dot(a_ref[...], wba_ref[...])
    y = _dot(d_ref[...].astype(BF16), pbd_ref[...]) * psc_ref[...]
    m = m + gate(1) * _dot(y.astype(BF16), wbp_ref[...])
    gv = gvn_ref[...].astype(BF16)
    grp = lax.broadcasted_iota(I32, (tm, GMLP_WIDTH), 1) // GMLP_GC
    mix = gb_ref[...]
    for g in range(GMLP_GROUPS):
        mix = mix + jnp.where(grp == g, _dot(wc_ref[g], gv), 0.0)
    c = u_ref[...] * mix
    m = m + gate(2) * _dot(c.astype(BF16), wbg_ref[...])
    y = _dot(m.astype(BF16), wo_ref[...])
    x1 = _layer_norm(alpha * x + y, lng_ref[...], lnb_ref[...])
    x1_ref[...] = x1

    scores = _sigmoid(_dot_nt(rwt_ref[...], x1.astype(BF16)))
    sel = scores + rb_ref[...]
    per = N_EXPERTS // N_EXPERT_GROUPS
    g3 = sel.reshape(N_EXPERT_GROUPS, per, tm)
    sub = lax.broadcasted_iota(I32, (N_EXPERT_GROUPS, per, tm), 1)
    m1 = jnp.max(g3, axis=1, keepdims=True)
    first = jnp.min(jnp.where(g3 == m1, sub, per), axis=1, keepdims=True)
    m2 = jnp.max(jnp.where(sub == first, -jnp.inf, g3), axis=1, keepdims=True)
    gs = (m1 + m2).reshape(N_EXPERT_GROUPS, tm)

    def rank_of(vals, count):
        idx = lax.broadcasted_iota(I32, vals.shape, 0)
        rank = jnp.zeros(vals.shape, I32)
        for o in range(count):
            other = vals[o:o + 1]
            beats = (other > vals) | ((other == vals) & (o < idx))
            rank = rank + jnp.where(beats, 1, 0)
        return rank

    gkeep = rank_of(gs, N_EXPERT_GROUPS) < TOPK_GROUPS
    ekeep = jnp.broadcast_to(gkeep.reshape(N_EXPERT_GROUPS, 1, tm), (N_EXPERT_GROUPS, per, tm)).reshape(N_EXPERTS, tm)
    sel = jnp.where(ekeep, sel, -jnp.inf)
    chosen = rank_of(sel, N_EXPERTS) < TOPK_EXPERTS
    wsel = jnp.where(chosen, scores, 0.0)
    gate_t = wsel / jnp.sum(wsel, axis=0, keepdims=True) * ROUTED_SCALE
    gate_ref[...] = gate_t.T


def _merge(x, a_bd, d, u, gvn, wl, tm, alpha):
    m = x.shape[0]
    row = lambda w: pl.BlockSpec((tm, w), lambda i: (i, 0))
    full = lambda a: pl.BlockSpec(a.shape, lambda i: (0,) * a.ndim)
    weights = [wl["wc"], wl["gb"], wl["wg"], wl["wba"], wl["pbd"], wl["psc"], wl["wbp"], wl["wbg"], wl["wo"],
               wl["ln1_g"], wl["ln1_b"], wl["rwt"], wl["rb"]]
    return pl.pallas_call(
        functools.partial(_merge_kernel, alpha=alpha),
        grid=(m // tm,),
        in_specs=[row(D_MODEL), row(ATT_Q), row(POOL_WIDTH), row(GMLP_WIDTH), row(GMLP_WIDTH)]
                 + [full(w) for w in weights],
        out_specs=(row(D_MODEL), row(N_EXPERTS)),
        out_shape=(jax.ShapeDtypeStruct((m, D_MODEL), F32), jax.ShapeDtypeStruct((m, N_EXPERTS), F32)),
        compiler_params=_cparams(("parallel",)),
        name="merge",
    )(x, a_bd, d, u, gvn, *weights)


def _swiglu_act(xb, w_gu):
    h = _dot(xb, w_gu)
    g = h[:, :EXPERT_FF]
    return g * _sigmoid(g) * h[:, EXPERT_FF:]


def _moe_kernel(x_ref, gate_ref, wgu_ref, wd_ref, sgu_ref, sd_ref, lng_ref, lnb_ref, o_ref, xb_ref, *, alpha):
    step = pl.program_id(1)
    tm = x_ref.shape[0]
    row_chunks = [slice(r, r + MOE_ROW_CHUNK) for r in range(0, tm, MOE_ROW_CHUNK)] if tm > MOE_ROW_CHUNK \
        else [slice(0, tm)]

    @pl.when(step == 0)
    def _():
        for rows in row_chunks:
            xb_ref[rows, :] = x_ref[rows, :].astype(BF16)
            o_ref[rows, :] = _dot(_swiglu_act(xb_ref[rows, :], sgu_ref[...]).astype(BF16), sd_ref[...])

    w_gu = [wgu_ref[j].astype(BF16) for j in range(MOE_EXPERTS_PER_STEP)]
    w_down = wd_ref[...].astype(BF16).reshape(MOE_EXPERTS_PER_STEP * EXPERT_FF, D_MODEL)
    for rows in row_chunks:
        xb = xb_ref[rows, :]
        gate = gate_ref[rows, :]
        lane = lax.broadcasted_iota(I32, gate.shape, 1)
        acts = []
        for j in range(MOE_EXPERTS_PER_STEP):
            e = step * MOE_EXPERTS_PER_STEP + j
            gcol = jnp.sum(jnp.where(lane == e, gate, 0.0), axis=1, keepdims=True)
            acts.append((gcol * _swiglu_act(xb, w_gu[j])).astype(BF16))
        o_ref[rows, :] += _dot(jnp.concatenate(acts, axis=1), w_down)

    @pl.when(step == pl.num_programs(1) - 1)
    def _():
        for rows in row_chunks:
            o_ref[rows, :] = _layer_norm(alpha * x_ref[rows, :] + o_ref[rows, :], lng_ref[...], lnb_ref[...])


def _moe(x1, gate, w_gu, w_down, layer, sh_gu, sh_down, ln_g, ln_b, tm, alpha):
    m = x1.shape[0]
    full = lambda a: pl.BlockSpec(a.shape, lambda i, e: (0,) * a.ndim)
    per = MOE_EXPERTS_PER_STEP
    steps = N_EXPERTS // per
    once = pl.Buffered(1)
    return pl.pallas_call(
        functools.partial(_moe_kernel, alpha=alpha),
        grid=(m // tm, steps),
        in_specs=[
            pl.BlockSpec((tm, D_MODEL), lambda i, e: (i, 0), pipeline_mode=once),
            pl.BlockSpec((tm, N_EXPERTS), lambda i, e: (i, 0), pipeline_mode=once),
            pl.BlockSpec((per, D_MODEL, 2 * EXPERT_FF), lambda i, e: (layer * steps + e, 0, 0)),
            pl.BlockSpec((per, EXPERT_FF, D_MODEL), lambda i, e: (layer * steps + e, 0, 0)),
            full(sh_gu), full(sh_down), full(ln_g), full(ln_b),
        ],
        out_specs=pl.BlockSpec((tm, D_MODEL), lambda i, e: (i, 0), pipeline_mode=once),
        out_shape=jax.ShapeDtypeStruct((m, D_MODEL), F32),
        scratch_shapes=[pltpu.VMEM((tm, D_MODEL), BF16)],
        compiler_params=_cparams(("parallel", "arbitrary")),
        name="moe",
    )(x1, gate, w_gu, w_down, sh_gu, sh_down, ln_g, ln_b)


def _rope_tables(pos):
    half = HEAD_DIM // 2
    inv = ROPE_THETA ** (-jnp.arange(half, dtype=F32) / half)
    ang = pos.astype(F32)[:, None] * inv[None, :]
    cos, sin = jnp.cos(ang), jnp.sin(ang)
    cos_t = jnp.tile(cos, (1, LANES // half))
    sin_t = jnp.tile(jnp.concatenate([-sin, sin], axis=1), (1, LANES // HEAD_DIM))
    return cos_t, sin_t


def _block_diag(blocks):
    g, r, c = blocks.shape
    eye = jnp.eye(g, dtype=blocks.dtype)
    return jnp.einsum("grc,gh->grhc", blocks, eye).reshape(g * r, g * c)


def _layer_weights(l, w_in, w_ba, w_bp, w_bg, w_out, pool_w, pool_scale, g_ln_g, g_ln_b, g_ws, g_b,
                   ln1_g, ln1_b, router_w, router_bias, sh_gu, sh_down, ln2_g, ln2_b):
    w = w_in[l]
    sizes = (ATT_Q, ATT_KV, ATT_KV, N_IDX_HEADS * IDX_DIM, IDX_DIM, N_IDX_HEADS,
             POOL_WIDTH, GMLP_WIDTH, GMLP_WIDTH, N_BRANCH * D_MODEL)
    offs = np.concatenate([[0], np.cumsum(sizes)]).tolist()
    wq, wk, wv, wiq, wik, wiw, wp, wu, wgv, wg = [w[:, offs[j]:offs[j + 1]] for j in range(len(sizes))]
    wq = wq * (HEAD_DIM ** -0.5 * LOG2E)
    wikw = jnp.pad(jnp.concatenate([wik, wiw], axis=1), ((0, 0), (0, LANES - IDX_DIM - N_IDX_HEADS)))
    w_cat = jnp.concatenate([wq, wk, wv, wiq, wikw, wp, wu, wgv], axis=1).astype(BF16)
    return dict(
        w_cat=w_cat, wvt=wv.T.astype(BF16), wiwt=wiw.T.astype(BF16), wg=wg.astype(BF16),
        wba=w_ba[l].astype(BF16),
        pbd=_block_diag(pool_w[l]).astype(BF16), psc=pool_scale[l].reshape(1, POOL_WIDTH),
        wbp=w_bp[l].astype(BF16), wbg=w_bg[l].astype(BF16), wo=w_out[l].astype(BF16),
        g_ln_g=g_ln_g[l].reshape(1, GMLP_WIDTH), g_ln_b=g_ln_b[l].reshape(1, GMLP_WIDTH),
        ws=g_ws[l], gbias=g_b[l],
        ln1_g=ln1_g[l].reshape(1, D_MODEL), ln1_b=ln1_b[l].reshape(1, D_MODEL),
        rwt=router_w[l].T.astype(BF16), rbias=router_bias[l],
        sh_gu=sh_gu[l].astype(BF16), sh_down=sh_down[l].astype(BF16),
        ln2_g=ln2_g[l].reshape(1, D_MODEL), ln2_b=ln2_b[l].reshape(1, D_MODEL),
    )


def _chunk_mix(wl, cl, tm):
    tril = jnp.tril(jnp.ones((cl, cl), F32))
    wm = wl["ws"][:, :cl, :cl] * tril
    reps = tm // cl
    wc = jnp.stack([_block_diag(jnp.broadcast_to(wm[g], (reps, cl, cl))) for g in range(GMLP_GROUPS)])
    gb = jnp.tile(jnp.repeat(wl["gbias"][:, :cl].T, GMLP_GC, axis=1), (reps, 1))
    return wc.astype(BF16), gb


def _heads_by_seq(a, ns, s_q):
    heads = a.shape[1] // HEAD_DIM
    return a.reshape(ns, s_q, heads, HEAD_DIM).transpose(0, 2, 1, 3).reshape(ns, heads * s_q, HEAD_DIM)


def kernel(x_prompt, x_sample, cache_k, cache_v, cache_idx_k, state_pool, page_table, w_in, w_branch_attn,
           w_branch_pool, w_branch_gmlp, w_out, pool_w, pool_scale, gmlp_ln_g, gmlp_ln_b, gmlp_ws, gmlp_b,
           ln1_g, ln1_b, router_w, router_bias, expert_w_gu, expert_w_down, shared_w_gu, shared_w_down,
           ln2_g, ln2_b):
    n_p, t_p, _ = x_prompt.shape
    n_s, t_s, _ = x_sample.shape
    depth = w_in.shape[0]
    n_pool = cache_k.shape[1]
    npages = page_table.shape[1]
    past = npages * PAGE_SIZE
    m_p, m_s = n_p * t_p, n_s * t_s
    alpha = (2 * depth) ** 0.25
    assert t_p % BLK == 0 and m_s % BLK == 0 and BLK % t_s == 0 and t_s <= 16

    tm_p = BLK
    tm_s = BLK
    tm_moe = MOE_TOKEN_TILE if m_p % MOE_TOKEN_TILE == 0 else tm_p
    tp_pool = POOL_TOKEN_TILE if t_p % POOL_TOKEN_TILE == 0 else BLK
    pages_per_step = ATTEND_PAGES_PER_STEP if npages % ATTEND_PAGES_PER_STEP == 0 else 1
    score_pages_per_step = SCORE_PAGES_PER_STEP if npages % SCORE_PAGES_PER_STEP == 0 else pages_per_step
    sel_batch = SELECT_SEQS_PER_STEP if n_s % SELECT_SEQS_PER_STEP == 0 else 1
    topk_p = min(TOPK_MAX, t_p // 4)
    topk_s = min(TOPK_MAX, (past + t_s) // 4)

    cos_p, sin_p = _rope_tables(jnp.arange(t_p, dtype=I32))
    cos_s, sin_s = _rope_tables(past + jnp.arange(t_s, dtype=I32))
    cos_s, sin_s = jnp.tile(cos_s, (m_s // t_s, 1)), jnp.tile(sin_s, (m_s // t_s, 1))
    tri_p = _tri_matrix(BLK, keys_on_lanes=False)
    group_of_row = jnp.asarray(np.eye(N_KV_HEADS, dtype=np.float32)[
        np.repeat(np.arange(N_HEADS) // (N_HEADS // N_KV_HEADS), t_s)])
    tri_s = _tri_matrix(LANES, keys_on_lanes=True)
    cache_kt = cache_k.transpose(0, 1, 3, 4, 2).reshape(depth * n_pool, ATT_KV, PAGE_SIZE)
    cache_vt = cache_v.transpose(0, 1, 3, 4, 2).reshape(depth * n_pool, ATT_KV, PAGE_SIZE)
    cache_ikt = cache_idx_k.transpose(0, 1, 3, 2).reshape(depth * n_pool, IDX_DIM, PAGE_SIZE)

    w_gu_all = expert_w_gu.reshape(depth * N_EXPERTS, D_MODEL, 2 * EXPERT_FF)
    w_down_all = expert_w_down.reshape(depth * N_EXPERTS, EXPERT_FF, D_MODEL)

    hp = x_prompt.reshape(m_p, D_MODEL)
    hs = x_sample.reshape(m_s, D_MODEL)
    outs = {name: [] for name in ("kp", "vp", "ikp", "pp", "ks", "vs", "iks", "ps", "gs")}
    for l in range(depth):
        wl = _layer_weights(l, w_in, w_branch_attn, w_branch_pool, w_branch_gmlp, w_out, pool_w, pool_scale,
                            gmlp_ln_g, gmlp_ln_b, gmlp_ws, gmlp_b, ln1_g, ln1_b, router_w, router_bias,
                            shared_w_gu, shared_w_down, ln2_g, ln2_b)

        def finish(x, a_bd, d, u, gvn, cl, tm, tm_e):
            wc, gb = _chunk_mix(wl, cl, tm)
            wm = dict(wl, wc=wc, gb=gb, rb=jnp.broadcast_to(wl["rbias"][:, None], (N_EXPERTS, tm)))
            x1, gate = _merge(x, a_bd, d, u, gvn, wm, tm, alpha)
            return _moe(x1, gate, w_gu_all, w_down_all, l, wl["sh_gu"], wl["sh_down"],
                        wl["ln2_g"], wl["ln2_b"], tm_e, alpha)

        q, k, v, kb, vt, iq, ikw, ikb, iwt, p, u, gvn = _proj(hp, wl, cos_p, sin_p, tm_p)
        p3 = p.reshape(n_p, t_p, POOL_WIDTH)
        d = _pool(p3, None, 0, tp_pool).reshape(m_p, POOL_WIDTH)
        a = _dsa_prompt(iq, iwt, q, ikb, kb, vt, tri_p, n_p, t_p, topk_p).reshape(m_p, ATT_Q)
        hp = finish(hp, a, d, u, gvn, CHUNK, tm_p, tm_moe)
        outs["kp"].append(k.reshape(n_p, t_p, N_KV_HEADS, HEAD_DIM))
        outs["vp"].append(v.reshape(n_p, t_p, N_KV_HEADS, HEAD_DIM))
        outs["ikp"].append(ikw[:, :IDX_DIM].reshape(n_p, t_p, IDX_DIM))
        outs["pp"].append(p3[:, t_p - POOL_STATE:])

        q, k, v, kb, vt, iq, ikw, ikb, iwt, p, u, gvn = _proj(hs, wl, cos_s, sin_s, tm_s)
        p3 = p.reshape(n_s, t_s, POOL_WIDTH)
        prefix16 = jnp.pad(state_pool[l], ((0, 0), (16 - POOL_STATE, 0), (0, 0)))
        d = _pool(p3, prefix16, past, t_s).reshape(m_s, POOL_WIDTH)
        iq_s = _heads_by_seq(iq, n_s, t_s)
        q_s = _heads_by_seq(q, n_s, t_s)
        q_s = jnp.einsum("nrc,rg->nrgc", q_s, group_of_row.astype(q_s.dtype)).reshape(n_s, -1, ATT_KV)
        wb_s = jnp.broadcast_to(iwt.reshape(N_IDX_HEADS, n_s, t_s).transpose(1, 0, 2).reshape(n_s, -1, 1),
                                (n_s, N_IDX_HEADS * t_s, LANES))
        pad_rows = lambda a: jnp.pad(a.reshape(n_s, t_s, -1), ((0, 0), (0, PAGE_SIZE - t_s), (0, 0)))
        ikn = pad_rows(ikb[:, :IDX_DIM])
        scores = _sample_scores(page_table, iq_s, wb_s, cache_ikt, l * n_pool, score_pages_per_step)
        bias = _sample_select(scores, iq_s, wb_s, ikn, tri_s, topk_s, sel_batch)
        a_s = _sample_attend(page_table, q_s, bias, pad_rows(k), pad_rows(v), cache_kt, cache_vt,
                             l * n_pool, pages_per_step)
        a_s = a_s.reshape(n_s, N_HEADS, t_s, N_KV_HEADS, HEAD_DIM)
        a_s = jnp.stack([a_s[:, h, :, h // (N_HEADS // N_KV_HEADS)] for h in range(N_HEADS)], axis=2)
        hs = finish(hs, a_s.reshape(m_s, ATT_Q), d, u, gvn, t_s, tm_s, tm_s)
        outs["ks"].append(k.reshape(n_s, t_s, N_KV_HEADS, HEAD_DIM))
        outs["vs"].append(v.reshape(n_s, t_s, N_KV_HEADS, HEAD_DIM))
        outs["iks"].append(ikw[:, :IDX_DIM].reshape(n_s, t_s, IDX_DIM))
        outs["ps"].append(jnp.concatenate([state_pool[l], p3], axis=1)[:, -POOL_STATE:])
        outs["gs"].append(gvn.reshape(n_s, t_s, GMLP_WIDTH))

    st = lambda name: jnp.stack(outs[name])
    return (hp.reshape(n_p, t_p, D_MODEL), hs.reshape(n_s, t_s, D_MODEL),
            st("kp"), st("vp"), st("ikp"), st("ks"), st("vs"), st("iks"), st("pp"), st("ps"), st("gs"))
```
